```python
import jax, jax.numpy as jnp
from jax import lax
import numpy as np

D_MODEL = 1024
BATCH = 1
SEQ = 16384
DEPTH = 1
DEC_BATCH = 32
DEC_SEQ = 4
PAST_LEN = 16384
PAGE_SIZE = 128

N_HEADS = 8
HEAD_DIM = 64
ATTN_WIDTH = N_HEADS * HEAD_DIM
GMLP_GROUPS = 8
GMLP_WIDTH = 512
GMLP_GROUP_DIM = GMLP_WIDTH // GMLP_GROUPS
CHUNK = 128
DILATIONS = ((128, 1), (512, 4), (2048, 16))
WINDOW_MAX = 2048
BLOCK = 128
ROT_DIM = HEAD_DIM // 4
ROPE_THETA = 500000.0
D_FF = 4 * D_MODEL
PLE_DIM = 256
IN_WIDTH = 3 * ATTN_WIDTH + 2 * GMLP_WIDTH
EPS = 1e-6
NEG = -1e30

kernel_name = "hymba_dilated_gmlp_decoder_step"


def rmsnorm(x, g):
    xf = x.astype(jnp.float32)
    y = xf * lax.rsqrt(jnp.mean(xf * xf, axis=-1, keepdims=True) + EPS)
    return (y * g.astype(jnp.float32)).astype(x.dtype)


def layernorm(x, g, b):
    xf = x.astype(jnp.float32)
    mu = jnp.mean(xf, axis=-1, keepdims=True)
    var = jnp.mean(jnp.square(xf - mu), axis=-1, keepdims=True)
    y = (xf - mu) * lax.rsqrt(var + EPS)
    return (y * g.astype(jnp.float32) + b.astype(jnp.float32)).astype(x.dtype)


def rope_partial(x, pos):
    half = ROT_DIM // 2
    inv_freq = ROPE_THETA ** (-jnp.arange(0, ROT_DIM, 2, dtype=jnp.float32) / ROT_DIM)
    ang = pos.astype(jnp.float32)[:, None] * inv_freq[None, :]
    cos = jnp.cos(ang)[:, None, :].astype(x.dtype)
    sin = jnp.sin(ang)[:, None, :].astype(x.dtype)
    x1, x2, rest = x[..., :half], x[..., half:ROT_DIM], x[..., ROT_DIM:]
    return jnp.concatenate([x1 * cos - x2 * sin, x2 * cos + x1 * sin, rest], axis=-1)


def project(h, pos, norm_g, w_in):
    B, T, _ = h.shape
    z = rmsnorm(h, norm_g) @ w_in
    q, k, v, u, vc = jnp.split(z, np.cumsum([ATTN_WIDTH] * 3 + [GMLP_WIDTH]).tolist(), axis=-1)
    q = rope_partial(q.reshape(B, T, N_HEADS, HEAD_DIM), pos)
    k = rope_partial(k.reshape(B, T, N_HEADS, HEAD_DIM), pos)
    v = v.reshape(B, T, N_HEADS, HEAD_DIM)
    return q, k, v, jax.nn.gelu(u), jax.nn.gelu(vc)


def softmax_stats(s):
    m = jnp.max(s, axis=-1, keepdims=True)
    e = jnp.exp(s - m)
    den = jnp.sum(e, axis=-1, keepdims=True)
    return e / den, (m + jnp.log(den))[..., 0]


def dilated_attn_prompt(q, k, v, window, dil):
    B, S, H, D = q.shape
    span = window // dil
    n = S // dil
    nb = -(-n // BLOCK)
    pad = nb * BLOCK - n

    def sub(a):
        return a.reshape(B, n, dil, H, D).transpose(0, 2, 1, 3, 4)

    qs = jnp.pad(sub(q), ((0, 0), (0, 0), (0, pad), (0, 0), (0, 0))).reshape(B, dil, nb, BLOCK, H, D)

    def kwin(a):
        a = jnp.pad(sub(a), ((0, 0), (0, 0), (BLOCK, pad), (0, 0), (0, 0))).reshape(B, dil, nb + 1, BLOCK, H, D)
        return jnp.concatenate([a[:, :, :-1], a[:, :, 1:]], axis=3)

    ks, vs = kwin(k), kwin(v)
    s = jnp.einsum('brnqhd,brnkhd->brnhqk', qs, ks, preferred_element_type=jnp.float32) * (HEAD_DIM ** -0.5)
    qi = jnp.arange(BLOCK)[:, None]
    kj = jnp.arange(2 * BLOCK)[None, :]
    diff = qi + BLOCK - kj
    band = (diff >= 0) & (diff <= span)
    not_before_start = (jnp.arange(nb)[:, None, None] > 0) | (kj[None] >= BLOCK)
    mask = band[None] & not_before_start
    s = jnp.where(mask[:, None], s, NEG)
    p, lse = softmax_stats(s)
    o = jnp.einsum('brnhqk,brnkhd->brnqhd', p, vs.astype(jnp.float32))
    o = o.reshape(B, dil, nb * BLOCK, H, D)[:, :, :n].transpose(0, 2, 1, 3, 4).reshape(B, S, H, D)
    lse = lse.transpose(0, 1, 2, 4, 3).reshape(B, dil, nb * BLOCK, H)[:, :, :n]
    lse = lse.transpose(0, 2, 1, 3).reshape(B, S, H)
    return o, lse


def dilated_attn_sample(q, k_all, v_all, window, dil, lbuf):
    T = q.shape[1]
    span = window // dil
    idx = lbuf + jnp.arange(T)[:, None] - dil * jnp.arange(span + 1)[None, :]
    valid = idx >= 0
    idx = jnp.maximum(idx, 0)
    kg = k_all[:, idx]
    vg = v_all[:, idx]
    s = jnp.einsum('bthd,btkhd->bthk', q, kg, preferred_element_type=jnp.float32) * (HEAD_DIM ** -0.5)
    s = jnp.where(valid[None, :, None, :], s, NEG)
    p, lse = softmax_stats(s)
    o = jnp.einsum('bthk,btkhd->bthd', p, vg.astype(jnp.float32))
    return o, lse


def combine_dilations(outs, lses):
    w = jax.nn.softmax(jnp.stack(lses, axis=0), axis=0)
    return jnp.sum(w[..., None] * jnp.stack(outs, axis=0), axis=0)


def gmlp_gate(u, vc, ln_g, ln_b, w_s, b_s):
    B, T, _ = u.shape
    vn = layernorm(vc, ln_g, ln_b)
    L = min(T, CHUNK)
    n = T // L
    tril = jnp.tril(jnp.ones((L, L), dtype=bool))
    w = jnp.where(tril[None], w_s[:, :L, :L], 0).astype(vn.dtype)
    vg = vn.reshape(B, n, L, GMLP_GROUPS, GMLP_GROUP_DIM)
    mixed = jnp.einsum('gij,bnjgc->bnigc', w, vg) + b_s[:, :L].T[:, :, None]
    return u * mixed.reshape(B, T, GMLP_WIDTH), vn


def finish(h, attn, gated, p, w_out, norm2_g, w_up, w_down, gate_norm_g, w_gate, w_ple):
    B, T, _ = h.shape
    mix = jnp.concatenate([attn.reshape(B, T, ATTN_WIDTH).astype(h.dtype), gated], axis=-1)
    h = h + mix @ w_out
    f = jnp.square(jax.nn.relu(rmsnorm(h, norm2_g) @ w_up)) @ w_down
    h = h + f
    gate = jax.nn.sigmoid(rmsnorm(h, gate_norm_g) @ w_gate)
    return h + gate * (p @ w_ple)


def setup_inputs(seed: int = 0) -> dict:
    key = jax.random.key(seed)
    ks = jax.random.split(key, 24)
    f32 = jnp.float32
    nrm = lambda k, shape, scale: jax.random.normal(k, shape, f32) * scale
    wbuf = min(WINDOW_MAX, PAST_LEN)
    return {
        "x_prompt": nrm(ks[0], (BATCH, SEQ, D_MODEL), 1.0),
        "x_sample": nrm(ks[1], (DEC_BATCH, DEC_SEQ, D_MODEL), 1.0),
        "cache_k": nrm(ks[2], (DEPTH, DEC_BATCH, wbuf, N_HEADS, HEAD_DIM), 1.0),
        "cache_v": nrm(ks[3], (DEPTH, DEC_BATCH, wbuf, N_HEADS, HEAD_DIM), 1.0),
        "p_prompt": nrm(ks[4], (DEPTH, BATCH, SEQ, PLE_DIM), 1.0),
        "p_sample": nrm(ks[5], (DEPTH, DEC_BATCH, DEC_SEQ, PLE_DIM), 1.0),
        "norm1_g": 1.0 + nrm(ks[6], (DEPTH, D_MODEL), 0.02),
        "w_in": nrm(ks[7], (DEPTH, D_MODEL, IN_WIDTH), D_MODEL ** -0.5),
        "ln_v_g": 1.0 + nrm(ks[8], (DEPTH, GMLP_WIDTH), 0.02),
        "ln_v_b": nrm(ks[9], (DEPTH, GMLP_WIDTH), 0.02),
        "w_spatial": nrm(ks[10], (DEPTH, GMLP_GROUPS, CHUNK, CHUNK), CHUNK ** -0.5),
        "b_spatial": nrm(ks[11], (DEPTH, GMLP_GROUPS, CHUNK), 0.02),
        "w_out": nrm(ks[12], (DEPTH, ATTN_WIDTH + GMLP_WIDTH, D_MODEL), (ATTN_WIDTH + GMLP_WIDTH) ** -0.5),
        "norm2_g": 1.0 + nrm(ks[13], (DEPTH, D_MODEL), 0.02),
        "w_up": nrm(ks[14], (DEPTH, D_MODEL, D_FF), D_MODEL ** -0.5),
        "w_down": nrm(ks[15], (DEPTH, D_FF, D_MODEL), D_FF ** -0.5),
        "gate_norm_g": 1.0 + nrm(ks[16], (DEPTH, D_MODEL), 0.02),
        "w_gate": nrm(ks[17], (DEPTH, D_MODEL, D_MODEL), D_MODEL ** -0.5),
        "w_ple": nrm(ks[18], (DEPTH, PLE_DIM, D_MODEL), PLE_DIM ** -0.5),
        "final_g": 1.0 + nrm(ks[19], (D_MODEL,), 0.02),
    }


def reference(x_prompt, x_sample, cache_k, cache_v, p_prompt, p_sample,
              norm1_g, w_in, ln_v_g, ln_v_b, w_spatial, b_spatial, w_out,
              norm2_g, w_up, w_down, gate_norm_g, w_gate, w_ple, final_g):
    S = x_prompt.shape[1]
    T = x_sample.shape[1]
    lbuf = cache_k.shape[2]
    past = PAST_LEN
    pos_prompt = jnp.arange(S, dtype=jnp.float32)
    pos_sample = past + jnp.arange(T, dtype=jnp.float32)
    keep = min(WINDOW_MAX, S)
    hp, hs = x_prompt, x_sample
    nk_p, nv_p, nk_s, nv_s, nvc_s = [], [], [], [], []
    for i in range(DEPTH):
        q, k, v, u, vc = project(hp, pos_prompt, norm1_g[i], w_in[i])
        outs, lses = zip(*[dilated_attn_prompt(q, k, v, w, d) for (w, d) in DILATIONS])
        attn = combine_dilations(outs, lses)
        gated, _ = gmlp_gate(u, vc, ln_v_g[i], ln_v_b[i], w_spatial[i], b_spatial[i])
        hp = finish(hp, attn, gated, p_prompt[i], w_out[i], norm2_g[i], w_up[i], w_down[i],
                    gate_norm_g[i], w_gate[i], w_ple[i])
        nk_p.append(k[:, S - keep:])
        nv_p.append(v[:, S - keep:])
        q, k, v, u, vc = project(hs, pos_sample, norm1_g[i], w_in[i])
        k_all = jnp.concatenate([cache_k[i].astype(k.dtype), k], axis=1)
        v_all = jnp.concatenate([cache_v[i].astype(v.dtype), v], axis=1)
        outs, lses = zip(*[dilated_attn_sample(q, k_all, v_all, w, d, lbuf) for (w, d) in DILATIONS])
        attn = combine_dilations(outs, lses)
        gated, vn = gmlp_gate(u, vc, ln_v_g[i], ln_v_b[i], w_spatial[i], b_spatial[i])
        hs = finish(hs, attn, gated, p_sample[i], w_out[i], norm2_g[i], w_up[i], w_down[i],
                    gate_norm_g[i], w_gate[i], w_ple[i])
        nk_s.append(k)
        nv_s.append(v)
        nvc_s.append(vn)
    y_prompt = rmsnorm(hp, final_g)
    y_sample = rmsnorm(hs, final_g)
    return (y_prompt, y_sample, jnp.stack(nk_p), jnp.stack(nv_p),
            jnp.stack(nk_s), jnp.stack(nv_s), jnp.stack(nvc_s))
```

```python
import functools

import jax
import jax.numpy as jnp
from jax import lax
from jax.experimental import pallas as pl
from jax.experimental.pallas import tpu as pltpu

D_MODEL = 1024
N_HEADS = 8
HEAD_DIM = 64
ATTN_WIDTH = N_HEADS * HEAD_DIM
GMLP_GROUPS = 8
GMLP_WIDTH = 512
CHUNK = 128
DILATIONS = ((128, 1), (512, 4), (2048, 16))
WINDOW_MAX = 2048
PAST_LEN = 16384
ROT_DIM = HEAD_DIM // 4
ROPE_THETA = 500000.0
D_FF = 4 * D_MODEL
PLE_DIM = 256
EPS = 1e-6
NEG = -1e30

LANES = 128
VMEM_LIMIT_BYTES = 56 * 1024 * 1024

N_CLASS = 16
SPAN = 128
PAIRS = ATTN_WIDTH // LANES
FF_CHUNK = 1024

BF16 = jnp.bfloat16
F32 = jnp.float32


def _rms(x, g):
    ms = jnp.mean(x * x, axis=-1, keepdims=True)
    return x * lax.rsqrt(ms + EPS) * g


def _const_spec(shape):
    nd = len(shape)
    return pl.BlockSpec(shape, lambda *_: (0,) * nd, pipeline_mode=pl.Buffered(1))


def _project_kernel(x_ref, g1_ref, w_ref, rope_ref, lng_ref, lnb_ref, wsp_ref, bsp_ref,
                    q_ref, k_ref, v_ref, kf_ref, vf_ref, gated_ref, *vn_refs,
                    rows, tail_from):
    i = pl.program_id(0)
    xn = _rms(x_ref[...], g1_ref[...]).astype(BF16)

    cosf = rope_ref[:, 0:LANES]
    sina = rope_ref[:, LANES:2 * LANES]
    sinb = rope_ref[:, 2 * LANES:3 * LANES]

    def rope(z):
        return z * cosf + pltpu.roll(z, LANES - ROT_DIM // 2, 1) * sina + pltpu.roll(z, ROT_DIM // 2, 1) * sinb

    zq = jnp.dot(xn, w_ref[:, 0:ATTN_WIDTH], preferred_element_type=F32)
    for s in range(PAIRS):
        sl = slice(s * LANES, (s + 1) * LANES)
        q_ref[:, sl] = (rope(zq[:, sl]) * (HEAD_DIM ** -0.5)).astype(BF16)

    zk = jnp.dot(xn, w_ref[:, ATTN_WIDTH:2 * ATTN_WIDTH], preferred_element_type=F32)
    zv = jnp.dot(xn, w_ref[:, 2 * ATTN_WIDTH:3 * ATTN_WIDTH], preferred_element_type=F32)
    kr = jnp.concatenate([rope(zk[:, s * LANES:(s + 1) * LANES]) for s in range(PAIRS)], axis=-1)
    k_ref[...] = kr.astype(BF16)
    v_ref[...] = zv.astype(BF16)

    @pl.when(i >= tail_from)
    def _():
        kf_ref[...] = kr
        vf_ref[...] = zv

    o_u = 3 * ATTN_WIDTH
    u = jax.nn.gelu(jnp.dot(xn, w_ref[:, o_u:o_u + GMLP_WIDTH], preferred_element_type=F32))
    vc = jax.nn.gelu(jnp.dot(xn, w_ref[:, o_u + GMLP_WIDTH:o_u + 2 * GMLP_WIDTH], preferred_element_type=F32))
    mu = jnp.mean(vc, axis=-1, keepdims=True)
    cen = vc - mu
    var = jnp.mean(cen * cen, axis=-1, keepdims=True)
    vn = cen * lax.rsqrt(var + EPS) * lng_ref[...] + lnb_ref[...]
    if vn_refs:
        vn_refs[0][...] = vn
    vnb = vn.astype(BF16)

    row = lax.broadcasted_iota(jnp.int32, (CHUNK, 2 * CHUNK), 0)
    col = lax.broadcasted_iota(jnp.int32, (CHUNK, 2 * CHUNK), 1)
    tril = (col % CHUNK) <= row
    lane = lax.broadcasted_iota(jnp.int32, (CHUNK, LANES), 1)
    first = lane < HEAD_DIM
    zero = jnp.zeros((CHUNK, LANES), BF16)
    for s in range(GMLP_WIDTH // LANES):
        sl = slice(s * LANES, (s + 1) * LANES)
        wp = jnp.where(tril, wsp_ref[s], jnp.zeros((), BF16))
        bias = bsp_ref[:, sl]
        for c in range(rows // CHUNK):
            rs = slice(c * CHUNK, (c + 1) * CHUNK)
            vblk = vnb[rs, sl]
            rhs = jnp.concatenate([jnp.where(first, vblk, zero), jnp.where(first, zero, vblk)], axis=0)
            mixed = jnp.dot(wp, rhs, preferred_element_type=F32) + bias
            gated_ref[rs, sl] = (u[rs, sl] * mixed).astype(BF16)


def _project(x, g1, w_in_b, rope_tab, ln_g, ln_b, wsp, bsp, *, rows, tail_rows, emit_vn):
    n_rows = x.shape[0]
    grid = n_rows // rows
    tail_from = (n_rows - tail_rows) // rows
    row_spec = lambda width: pl.BlockSpec((rows, width), lambda i: (i, 0))
    tail_spec = pl.BlockSpec((rows, ATTN_WIDTH), lambda i: (jnp.maximum(i - tail_from, 0), 0))
    out_shape = [
        jax.ShapeDtypeStruct((n_rows, ATTN_WIDTH), BF16),
        jax.ShapeDtypeStruct((n_rows, ATTN_WIDTH), BF16),
        jax.ShapeDtypeStruct((n_rows, ATTN_WIDTH), BF16),
        jax.ShapeDtypeStruct((tail_rows, ATTN_WIDTH), F32),
        jax.ShapeDtypeStruct((tail_rows, ATTN_WIDTH), F32),
        jax.ShapeDtypeStruct((n_rows, GMLP_WIDTH), BF16),
    ]
    out_specs = [row_spec(ATTN_WIDTH)] * 3 + [tail_spec] * 2 + [row_spec(GMLP_WIDTH)]
    if emit_vn:
        out_shape.append(jax.ShapeDtypeStruct((n_rows, GMLP_WIDTH), F32))
        out_specs.append(row_spec(GMLP_WIDTH))
    return pl.pallas_call(
        functools.partial(_project_kernel, rows=rows, tail_from=tail_from),
        grid=(grid,),
        in_specs=[
            row_spec(D_MODEL),
            _const_spec((1, D_MODEL)),
            _const_spec(w_in_b.shape),
            row_spec(3 * LANES),
            _const_spec((1, GMLP_WIDTH)),
            _const_spec((1, GMLP_WIDTH)),
            _const_spec(wsp.shape),
            _const_spec(bsp.shape),
        ],
        out_specs=out_specs,
        out_shape=out_shape,
        compiler_params=pltpu.CompilerParams(
            dimension_semantics=("arbitrary",), vmem_limit_bytes=VMEM_LIMIT_BYTES),
        name="project",
    )(x, g1, w_in_b, rope_tab, ln_g, ln_b, wsp, bsp)


def _attend_unit(qs, ks, vs, mask, state, first, last):
    mq = qs[0].shape[0]
    nk = ks[0].shape[0]
    klane = lax.broadcasted_iota(jnp.int32, (nk, LANES), 1) < HEAD_DIM
    olane = lax.broadcasted_iota(jnp.int32, (mq, LANES), 1) < HEAD_DIM
    kzero = jnp.zeros((nk, LANES), BF16)
    out = []
    for pr in range(PAIRS):
        halves = []
        for hh in range(2):
            keep = klane if hh == 0 else jnp.logical_not(klane)
            km = jnp.where(keep, ks[pr], kzero)
            s = lax.dot_general(qs[pr], km, (((1,), (1,)), ((), ())), preferred_element_type=F32)
            s = jnp.where(mask, s, NEG)
            m = jnp.max(s, axis=-1, keepdims=True)
            p = jnp.exp(s - m)
            l = jnp.sum(p, axis=-1, keepdims=True)
            pv = jnp.dot(p.astype(BF16), vs[pr], preferred_element_type=F32)
            halves.append((pv, m, l))
        pv = jnp.where(olane, halves[0][0], halves[1][0])
        m = jnp.where(olane, halves[0][1], halves[1][1])
        l = jnp.where(olane, halves[0][2], halves[1][2])
        if not first:
            acc0, m0, l0 = state[pr]
            m_new = jnp.maximum(m0, m)
            a = jnp.exp(m0 - m_new)
            b = jnp.exp(m - m_new)
            pv = acc0 * a + pv * b
            l = l0 * a + l * b
            m = m_new
        out.append(pv / l if last else (pv, m, l))
    return out


def _prompt_attn_kernel(q_ref, k_ref, v_ref, o_ref, kc_ref, vc_ref, acc_ref, m_ref, l_ref):
    i = pl.program_id(0)
    rows = q_ref.shape[1]

    @pl.when(i == 0)
    def _():
        kc_ref[:, 0:rows, :] = jnp.zeros((N_CLASS, rows, ATTN_WIDTH), BF16)
        vc_ref[:, 0:rows, :] = jnp.zeros((N_CLASS, rows, ATTN_WIDTH), BF16)

    @pl.when(i > 0)
    def _():
        kc_ref[:, 0:rows, :] = kc_ref[:, rows:2 * rows, :]
        vc_ref[:, 0:rows, :] = vc_ref[:, rows:2 * rows, :]

    kc_ref[:, rows:2 * rows, :] = k_ref[...]
    vc_ref[:, rows:2 * rows, :] = v_ref[...]
    has_prev = i > 0

    def lanes(pr):
        return slice(pr * LANES, (pr + 1) * LANES)

    qi = lax.broadcasted_iota(jnp.int32, (rows, 2 * rows), 0)
    kj = lax.broadcasted_iota(jnp.int32, (rows, 2 * rows), 1)
    diff = qi + rows - kj
    mask16 = (diff >= 0) & (diff <= SPAN) & ((kj >= rows) | has_prev)

    def body16(r, carry):
        qs = [q_ref[r, :, lanes(pr)] for pr in range(PAIRS)]
        ks = [kc_ref[r, :, lanes(pr)] for pr in range(PAIRS)]
        vs = [vc_ref[r, :, lanes(pr)] for pr in range(PAIRS)]
        res = _attend_unit(qs, ks, vs, mask16, None, True, False)
        for pr in range(PAIRS):
            acc_ref[r, :, lanes(pr)] = res[pr][0]
            m_ref[r, :, lanes(pr)] = res[pr][1]
            l_ref[r, :, lanes(pr)] = res[pr][2]
        return carry

    lax.fori_loop(0, N_CLASS, body16, 0)

    n4 = N_CLASS // 4
    qb = rows // n4
    qa = lax.broadcasted_iota(jnp.int32, (n4 * qb, n4 * 2 * qb), 0)
    ka = lax.broadcasted_iota(jnp.int32, (n4 * qb, n4 * 2 * qb), 1)
    d4 = 4 * (qa % qb - ka % (2 * qb) + qb) + (qa // qb - ka // (2 * qb))
    band4 = (d4 >= 0) & (d4 <= SPAN)

    def body4(t, carry):
        c = t // n4
        b = t % n4
        q0 = pl.multiple_of(b * qb, qb)
        k0 = pl.multiple_of(rows - qb + b * qb, qb)
        mask = band4 & ((ka % (2 * qb) + b * qb >= qb) | has_prev)
        slabs = [c + 4 * a for a in range(n4)]
        cat = lambda ref, r0, n, pr: jnp.concatenate([ref[sb, pl.ds(r0, n), lanes(pr)] for sb in slabs], axis=0)
        qs = [cat(q_ref, q0, qb, pr) for pr in range(PAIRS)]
        ks = [cat(kc_ref, k0, 2 * qb, pr) for pr in range(PAIRS)]
        vs = [cat(vc_ref, k0, 2 * qb, pr) for pr in range(PAIRS)]
        state = [(cat(acc_ref, q0, qb, pr), cat(m_ref, q0, qb, pr), cat(l_ref, q0, qb, pr)) for pr in range(PAIRS)]
        res = _attend_unit(qs, ks, vs, mask, state, False, False)
        for pr in range(PAIRS):
            for a, sb in enumerate(slabs):
                rs = slice(a * qb, (a + 1) * qb)
                acc_ref[sb, pl.ds(q0, qb), lanes(pr)] = res[pr][0][rs]
                m_ref[sb, pl.ds(q0, qb), lanes(pr)] = res[pr][1][rs]
                l_ref[sb, pl.ds(q0, qb), lanes(pr)] = res[pr][2][rs]
        return carry

    lax.fori_loop(0, N_CLASS, body4, 0)

    q1 = 2 * rows // N_CLASS
    qa1 = lax.broadcasted_iota(jnp.int32, (N_CLASS * q1, N_CLASS * 2 * q1), 0)
    ka1 = lax.broadcasted_iota(jnp.int32, (N_CLASS * q1, N_CLASS * 2 * q1), 1)
    d1 = N_CLASS * (qa1 % q1 - ka1 % (2 * q1) + q1) + (qa1 // q1 - ka1 // (2 * q1))
    band1 = (d1 >= 0) & (d1 <= SPAN)

    def body1(b, carry):
        q0 = pl.multiple_of(b * q1, q1)
        k0 = pl.multiple_of(rows - q1 + b * q1, q1)
        mask = band1 & ((ka1 % (2 * q1) + b * q1 >= q1) | has_prev)
        cat = lambda ref, r0, n, pr: jnp.concatenate(
            [ref[sb, pl.ds(r0, n), lanes(pr)] for sb in range(N_CLASS)], axis=0)
        qs = [cat(q_ref, q0, q1, pr) for pr in range(PAIRS)]
        ks = [cat(kc_ref, k0, 2 * q1, pr) for pr in range(PAIRS)]
        vs = [cat(vc_ref, k0, 2 * q1, pr) for pr in range(PAIRS)]
        state = [(cat(acc_ref, q0, q1, pr), cat(m_ref, q0, q1, pr), cat(l_ref, q0, q1, pr)) for pr in range(PAIRS)]
        res = _attend_unit(qs, ks, vs, mask, state, False, True)
        for pr in range(PAIRS):
            for sb in range(N_CLASS):
                o_ref[sb, pl.ds(q0, q1), lanes(pr)] = res[pr][sb * q1:(sb + 1) * q1].astype(BF16)
        return carry

    lax.fori_loop(0, rows // q1, body1, 0)


def _prompt_attention(q_cm, k_cm, v_cm):
    n_slab_rows = q_cm.shape[1]
    spec = pl.BlockSpec((N_CLASS, SPAN, ATTN_WIDTH), lambda i: (0, i, 0))
    return pl.pallas_call(
        _prompt_attn_kernel,
        grid=(n_slab_rows // SPAN,),
        in_specs=[spec, spec, spec],
        out_specs=spec,
        out_shape=jax.ShapeDtypeStruct(q_cm.shape, BF16),
        scratch_shapes=[
            pltpu.VMEM((N_CLASS, 2 * SPAN, ATTN_WIDTH), BF16),
            pltpu.VMEM((N_CLASS, 2 * SPAN, ATTN_WIDTH), BF16),
            pltpu.VMEM((N_CLASS, SPAN, ATTN_WIDTH), F32),
            pltpu.VMEM((N_CLASS, SPAN, ATTN_WIDTH), F32),
            pltpu.VMEM((N_CLASS, SPAN, ATTN_WIDTH), F32),
        ],
        compiler_params=pltpu.CompilerParams(
            dimension_semantics=("arbitrary",), vmem_limit_bytes=VMEM_LIMIT_BYTES),
        name="prompt_attention",
    )(q_cm, k_cm, v_cm)


def _sample_attn_kernel(q_ref, kn_ref, vn_ref, k16_ref, k4_ref, v16_ref, v4_ref, o_ref, *, t_new):
    width = ATTN_WIDTH
    nrow = t_new * N_HEADS
    q = q_ref[...]
    kn = kn_ref[...]
    vnew = vn_ref[...]

    row = lax.broadcasted_iota(jnp.int32, (nrow, width), 0)
    lane = lax.broadcasted_iota(jnp.int32, (nrow, width), 1)
    hmask = (lane // HEAD_DIM) == (row % N_HEADS)
    tok = row // N_HEADS
    qtok = jnp.concatenate([jnp.broadcast_to(q[j:j + 1, :], (N_HEADS, width)) for j in range(t_new)], axis=0)
    qrows = jnp.where(hmask, qtok, 0.0)
    qrows_b = qrows.astype(BF16)
    zero_b = jnp.zeros((nrow, width), BF16)
    qbig = jnp.concatenate([jnp.where(tok == g, qrows_b, zero_b) for g in range(t_new)], axis=-1)

    nt = (((1,), (1,)), ((), ()))
    k16 = k16_ref[...].astype(BF16)
    k4 = k4_ref[...].astype(BF16)
    v16 = v16_ref[...].astype(BF16)
    v4 = v4_ref[...].astype(BF16)
    s16 = lax.dot_general(qbig, k16, nt, preferred_element_type=F32)
    s4 = lax.dot_general(qbig, k4, nt, preferred_element_type=F32)
    tail = SPAN // t_new
    n4 = k4.shape[0]
    regroup = lambda a: jnp.concatenate(
        [a[n4 - tail:n4, g * width:(g + 1) * width] for g in range(t_new)], axis=0)
    k1 = regroup(k4)
    v1 = regroup(v4)
    s1 = lax.dot_general(qrows_b, k1, nt, preferred_element_type=F32)
    snew = [jnp.sum(qrows * kn[j:j + 1, :], axis=-1, keepdims=True) for j in range(t_new)]
    tok1 = tok[:, 0:1]

    def pick_rows(o_wide):
        return jnp.concatenate(
            [o_wide[j * N_HEADS:(j + 1) * N_HEADS, j * width:(j + 1) * width] for j in range(t_new)], axis=0)

    def group(s_buf, v_buf, wide, new_valid):
        m = jnp.max(s_buf, axis=-1, keepdims=True)
        for j in range(t_new):
            m = jnp.maximum(m, jnp.where(new_valid[j], snew[j], NEG))
        e = jnp.exp(s_buf - m)
        l = jnp.sum(e, axis=-1, keepdims=True)
        o = jnp.dot(e.astype(BF16), v_buf, preferred_element_type=F32)
        if wide:
            o = pick_rows(o)
        for j in range(t_new):
            ej = jnp.where(new_valid[j], jnp.exp(snew[j] - m), 0.0)
            l = l + ej
            o = o + ej * vnew[j:j + 1, :]
        return o / l, m + jnp.log(l)

    self_only = [tok1 == j for j in range(t_new)]
    causal = [tok1 >= j for j in range(t_new)]
    kk = lax.broadcasted_iota(jnp.int32, (nrow, SPAN), 1)
    krow = lax.broadcasted_iota(jnp.int32, (nrow, SPAN), 0) // N_HEADS
    s1 = jnp.where(t_new * (kk % tail) + kk // tail >= krow, s1, NEG)

    outs = [group(s1, v1, False, causal), group(s4, v4, True, self_only), group(s16, v16, True, self_only)]
    mx = jnp.maximum(jnp.maximum(outs[0][1], outs[1][1]), outs[2][1])
    ws = [jnp.exp(lse - mx) for _, lse in outs]
    den = ws[0] + ws[1] + ws[2]
    mixed = (ws[0] * outs[0][0] + ws[1] * outs[1][0] + ws[2] * outs[2][0]) / den
    mixed = jnp.where(hmask, mixed, 0.0)
    for j in range(t_new):
        o_ref[j:j + 1, :] = jnp.sum(mixed[j * N_HEADS:(j + 1) * N_HEADS, :], axis=0, keepdims=True)


def _sample_attention(q, k_new, v_new, cache_k, cache_v):
    nb, t_new, width = q.shape
    lbuf = cache_k.shape[1]
    assert lbuf == WINDOW_MAX and t_new == 4 and width == ATTN_WIDTH
    pad = lambda a: jnp.pad(a, ((0, 0), (0, 8 - t_new), (0, 0)))
    view16 = lambda c: c.reshape(nb, lbuf // 16, 16 * width)
    view4 = lambda c: c.reshape(nb, lbuf // 4, 4 * width)
    new_spec = pl.BlockSpec((None, 8, width), lambda b: (b, 0, 0))
    spec16 = pl.BlockSpec((None, lbuf // 16, t_new * width), lambda b: (b, 0, 0))
    last4 = (lbuf // 4) // SPAN - 1
    spec4 = pl.BlockSpec((None, SPAN, t_new * width), lambda b: (b, last4, 0))
    return pl.pallas_call(
        functools.partial(_sample_attn_kernel, t_new=t_new),
        grid=(nb,),
        in_specs=[new_spec, new_spec, new_spec, spec16, spec4, spec16, spec4],
        out_specs=pl.BlockSpec((None, t_new, width), lambda b: (b, 0, 0)),
        out_shape=jax.ShapeDtypeStruct((nb, t_new, width), F32),
        compiler_params=pltpu.CompilerParams(
            dimension_semantics=("arbitrary",), vmem_limit_bytes=VMEM_LIMIT_BYTES),
        name="sample_attention",
    )(pad(q), pad(k_new), pad(v_new), view16(cache_k), view4(cache_k), view16(cache_v), view4(cache_v))


def _finish_kernel(x_ref, attn_ref, gated_ref, p_ref, wo_ref, g2_ref, wup_ref, wdn_ref,
                   gg_ref, wg_ref, wple_ref, gf_ref, y_ref, *, apply_final):
    mix = jnp.concatenate([attn_ref[...], gated_ref[...]], axis=-1)
    h = x_ref[...] + jnp.dot(mix, wo_ref[...], preferred_element_type=F32)
    n2 = _rms(h, g2_ref[...]).astype(BF16)
    f = jnp.zeros_like(h)
    for c in range(D_FF // FF_CHUNK):
        cs = slice(c * FF_CHUNK, (c + 1) * FF_CHUNK)
        up = jnp.dot(n2, wup_ref[:, cs], preferred_element_type=F32)
        act = jnp.square(jnp.maximum(up, 0.0)).astype(BF16)
        f = f + jnp.dot(act, wdn_ref[cs, :], preferred_element_type=F32)
    h = h + f
    gate = jax.nn.sigmoid(jnp.dot(_rms(h, gg_ref[...]).astype(BF16), wg_ref[...], preferred_element_type=F32))
    ple = jnp.dot(p_ref[...].astype(BF16), wple_ref[...], preferred_element_type=F32)
    h = h + gate * ple
    y_ref[...] = _rms(h, gf_ref[...]) if apply_final else h


def _finish(x, attn, gated, p, wo, g2, wup, wdn, gg, wg, wple, gf, *, rows, apply_final):
    n_rows = x.shape[0]
    row_spec = lambda width: pl.BlockSpec((rows, width), lambda i: (i, 0))
    return pl.pallas_call(
        functools.partial(_finish_kernel, apply_final=apply_final),
        grid=(n_rows // rows,),
        in_specs=[
            row_spec(D_MODEL), row_spec(ATTN_WIDTH), row_spec(GMLP_WIDTH), row_spec(PLE_DIM),
            _const_spec(wo.shape), _const_spec((1, D_MODEL)), _const_spec(wup.shape), _const_spec(wdn.shape),
            _const_spec((1, D_MODEL)), _const_spec(wg.shape), _const_spec(wple.shape), _const_spec((1, D_MODEL)),
        ],
        out_specs=row_spec(D_MODEL),
        out_shape=jax.ShapeDtypeStruct((n_rows, D_MODEL), F32),
        compiler_params=pltpu.CompilerParams(
            dimension_semantics=("arbitrary",), vmem_limit_bytes=VMEM_LIMIT_BYTES),
        name="finish",
    )(x, attn, gated, p, wo, g2, wup, wdn, gg, wg, wple, gf)


def _rope_table(pos):
    half = ROT_DIM // 2
    inv_freq = ROPE_THETA ** (-jnp.arange(0, ROT_DIM, 2, dtype=F32) / ROT_DIM)
    ang = pos.astype(F32)[:, None] * inv_freq[None, :]
    cos, sin = jnp.cos(ang), jnp.sin(ang)
    t = pos.shape[0]
    rest = jnp.zeros((t, HEAD_DIM - ROT_DIM), F32)
    z = jnp.zeros((t, half), F32)
    per_head = [
        jnp.concatenate([cos, cos, rest + 1.0], axis=-1),
        jnp.concatenate([-sin, z, rest], axis=-1),
        jnp.concatenate([z, sin, rest], axis=-1),
    ]
    return jnp.concatenate([jnp.tile(a, (1, LANES // HEAD_DIM)) for a in per_head], axis=-1)


def _pair_spatial(w):
    g, l, _ = w.shape
    return w.reshape(g // 2, 2, l, l).transpose(0, 2, 1, 3).reshape(g // 2, l, 2 * l).astype(BF16)


def _bias_lanes(b):
    return jnp.repeat(b.T, GMLP_WIDTH // GMLP_GROUPS, axis=1)


def kernel(x_prompt, x_sample, cache_k, cache_v, p_prompt, p_sample, norm1_g, w_in, ln_v_g, ln_v_b,
           w_spatial, b_spatial, w_out, norm2_g, w_up, w_down, gate_norm_g, w_gate, w_ple, final_g):
    depth = w_in.shape[0]
    nbp, seq, _ = x_prompt.shape
    nbs, t_new, _ = x_sample.shape
    lbuf = cache_k.shape[2]
    keep = min(WINDOW_MAX, seq)
    assert nbp == 1 and seq % (N_CLASS * SPAN) == 0 and (nbs * t_new) % CHUNK == 0 and t_new <= CHUNK
    n_s = nbs * t_new

    rope_p = _rope_table(jnp.arange(seq, dtype=F32))
    rope_s = jnp.tile(_rope_table(PAST_LEN + jnp.arange(t_new, dtype=F32)), (nbs, 1))
    row2 = lambda a: a.reshape(1, -1)

    hp = x_prompt.reshape(seq, D_MODEL)
    hs = x_sample.reshape(n_s, D_MODEL)
    nk_p, nv_p, nk_s, nv_s, nvc_s = [], [], [], [], []
    for i in range(depth):
        last = i == depth - 1
        w_in_b = w_in[i].astype(BF16)
        wo_b, wup_b, wdn_b = w_out[i].astype(BF16), w_up[i].astype(BF16), w_down[i].astype(BF16)
        wg_b, wple_b = w_gate[i].astype(BF16), w_ple[i].astype(BF16)
        fin_w = (wo_b, row2(norm2_g[i]), wup_b, wdn_b, row2(gate_norm_g[i]), wg_b, wple_b, row2(final_g))
        ln_g, ln_b = row2(ln_v_g[i]), row2(ln_v_b[i])

        wsp_p = _pair_spatial(w_spatial[i])
        bsp_p = _bias_lanes(b_spatial[i])
        q, k, v, kf, vf, gated = _project(hp, row2(norm1_g[i]), w_in_b, rope_p, ln_g, ln_b, wsp_p, bsp_p,
                                          rows=512, tail_rows=keep, emit_vn=False)
        to_cm = lambda a: a.reshape(seq // N_CLASS, N_CLASS, ATTN_WIDTH).transpose(1, 0, 2)
        attn_cm = _prompt_attention(to_cm(q), to_cm(k), to_cm(v))
        attn = attn_cm.transpose(1, 0, 2).reshape(seq, ATTN_WIDTH)
        hp = _finish(hp, attn, gated, p_prompt[i].reshape(seq, PLE_DIM), *fin_w, rows=512, apply_final=last)
        nk_p.append(kf.reshape(nbp, keep, N_HEADS, HEAD_DIM))
        nv_p.append(vf.reshape(nbp, keep, N_HEADS, HEAD_DIM))

        corner = w_spatial[i][:, :t_new, :t_new]
        wsp_s = _pair_spatial(jax.vmap(lambda c: jnp.kron(jnp.eye(n_s // t_new, dtype=F32), c))(corner))
        bsp_s = _bias_lanes(jnp.tile(b_spatial[i][:, :t_new], (1, n_s // t_new)))
        q, k, v, kf, vf, gated, vn = _project(hs, row2(norm1_g[i]), w_in_b, rope_s, ln_g, ln_b, wsp_s, bsp_s,
                                              rows=n_s, tail_rows=n_s, emit_vn=True)
        r3 = lambda a: a.reshape(nbs, t_new, ATTN_WIDTH)
        attn = _sample_attention(r3(q).astype(F32), r3(kf), r3(vf),
                                 cache_k[i].reshape(nbs, lbuf, ATTN_WIDTH), cache_v[i].reshape(nbs, lbuf, ATTN_WIDTH))
        hs = _finish(hs, attn.reshape(n_s, ATTN_WIDTH).astype(BF16), gated, p_sample[i].reshape(n_s, PLE_DIM),
                     *fin_w, rows=n_s, apply_final=last)
        nk_s.append(kf.reshape(nbs, t_new, N_HEADS, HEAD_DIM))
        nv_s.append(vf.reshape(nbs, t_new, N_HEADS, HEAD_DIM))
        nvc_s.append(vn.reshape(nbs, t_new, GMLP_WIDTH))

    return (hp.reshape(nbp, seq, D_MODEL), hs.reshape(nbs, t_new, D_MODEL),
            jnp.stack(nk_p), jnp.stack(nv_p), jnp.stack(nk_s), jnp.stack(nv_s), jnp.stack(nvc_s))
```

```python
import functools

import jax
import jax.numpy as jnp
from jax import lax
from jax.experimental import pallas as pl
from jax.experimental.pallas import tpu as pltpu

D_MODEL = 1024
N_HEADS = 8
HEAD_DIM = 64
ATTN_WIDTH = N_HEADS * HEAD_DIM
GMLP_GROUPS = 8
GMLP_WIDTH = 512
CHUNK = 128
DILATIONS = ((128, 1), (512, 4), (2048, 16))
WINDOW_MAX = 2048
PAST_LEN = 16384
ROT_DIM = HEAD_DIM // 4
ROPE_THETA = 500000.0
D_FF = 4 * D_MODEL
PLE_DIM = 256
EPS = 1e-6
NEG = -1e30

LANES = 128
VMEM_LIMIT_BYTES = 56 * 1024 * 1024

N_CLASS = 16
SPAN = 128
PAIRS = ATTN_WIDTH // LANES
FF_CHUNK = 1024

BF16 = jnp.bfloat16
F32 = jnp.float32


def _rms(x, g):
    ms = jnp.mean(x * x, axis=-1, keepdims=True)
    return x * lax.rsqrt(ms + EPS) * g


def _const_spec(shape):
    nd = len(shape)
    return pl.BlockSpec(shape, lambda *_: (0,) * nd, pipeline_mode=pl.Buffered(1))


def _project_kernel(x_ref, g1_ref, w_ref, rope_ref, lng_ref, lnb_ref, wsp_ref, bsp_ref,
                    q_ref, k_ref, v_ref, kf_ref, vf_ref, gated_ref, *vn_refs,
                    rows, tail_from):
    i = pl.program_id(0)
    xn = _rms(x_ref[...], g1_ref[...]).astype(BF16)

    cosf = rope_ref[:, 0:LANES]
    sina = rope_ref[:, LANES:2 * LANES]
    sinb = rope_ref[:, 2 * LANES:3 * LANES]

    def rope(z):
        return z * cosf + pltpu.roll(z, LANES - ROT_DIM // 2, 1) * sina + pltpu.roll(z, ROT_DIM // 2, 1) * sinb

    zq = jnp.dot(xn, w_ref[:, 0:ATTN_WIDTH], preferred_element_type=F32)
    for s in range(PAIRS):
        sl = slice(s * LANES, (s + 1) * LANES)
        q_ref[:, sl] = (rope(zq[:, sl]) * (HEAD_DIM ** -0.5)).astype(BF16)

    zk = jnp.dot(xn, w_ref[:, ATTN_WIDTH:2 * ATTN_WIDTH], preferred_element_type=F32)
    zv = jnp.dot(xn, w_ref[:, 2 * ATTN_WIDTH:3 * ATTN_WIDTH], preferred_element_type=F32)
    kr = jnp.concatenate([rope(zk[:, s * LANES:(s + 1) * LANES]) for s in range(PAIRS)], axis=-1)
    k_ref[...] = kr.astype(BF16)
    v_ref[...] = zv.astype(BF16)

    @pl.when(i >= tail_from)
    def _():
        kf_ref[...] = kr
        vf_ref[...] = zv

    o_u = 3 * ATTN_WIDTH
    u = jax.nn.gelu(jnp.dot(xn, w_ref[:, o_u:o_u + GMLP_WIDTH], preferred_element_type=F32))
    vc = jax.nn.gelu(jnp.dot(xn, w_ref[:, o_u + GMLP_WIDTH:o_u + 2 * GMLP_WIDTH], preferred_element_type=F32))
    mu = jnp.mean(vc, axis=-1, keepdims=True)
    cen = vc - mu
    var = jnp.mean(cen * cen, axis=-1, keepdims=True)
    vn = cen * lax.rsqrt(var + EPS) * lng_ref[...] + lnb_ref[...]
    if vn_refs:
        vn_refs[0][...] = vn
    vnb = vn.astype(BF16)

    row = lax.broadcasted_iota(jnp.int32, (CHUNK, 2 * CHUNK), 0)
    col = lax.broadcasted_iota(jnp.int32, (CHUNK, 2 * CHUNK), 1)
    tril = (col % CHUNK) <= row
    lane = lax.broadcasted_iota(jnp.int32, (CHUNK, LANES), 1)
    first = lane < HEAD_DIM
    zero = jnp.zeros((CHUNK, LANES), BF16)
    for s in range(GMLP_WIDTH // LANES):
        sl = slice(s * LANES, (s + 1) * LANES)
        wp = jnp.where(tril, wsp_ref[s], jnp.zeros((), BF16))
        bias = bsp_ref[:, sl]
        for c in range(rows // CHUNK):
            rs = slice(c * CHUNK, (c + 1) * CHUNK)
            vblk = vnb[rs, sl]
            rhs = jnp.concatenate([jnp.where(first, vblk, zero), jnp.where(first, zero, vblk)], axis=0)
            mixed = jnp.dot(wp, rhs, preferred_element_type=F32) + bias
            gated_ref[rs, sl] = (u[rs, sl] * mixed).astype(BF16)


def _project(x, g1, w_in_b, rope_tab, ln_g, ln_b, wsp, bsp, *, rows, tail_rows, emit_vn):
    n_rows = x.shape[0]
    grid = n_rows // rows
    tail_from = (n_rows - tail_rows) // rows
    row_spec = lambda width: pl.BlockSpec((rows, width), lambda i: (i, 0))
    tail_spec = pl.BlockSpec((rows, ATTN_WIDTH), lambda i: (jnp.maximum(i - tail_from, 0), 0))
    out_shape = [
        jax.ShapeDtypeStruct((n_rows, ATTN_WIDTH), BF16),
        jax.ShapeDtypeStruct((n_rows, ATTN_WIDTH), BF16),
        jax.ShapeDtypeStruct((n_rows, ATTN_WIDTH), BF16),
        jax.ShapeDtypeStruct((tail_rows, ATTN_WIDTH), F32),
        jax.ShapeDtypeStruct((tail_rows, ATTN_WIDTH), F32),
        jax.ShapeDtypeStruct((n_rows, GMLP_WIDTH), BF16),
    ]
    out_specs = [row_spec(ATTN_WIDTH)] * 3 + [tail_spec] * 2 + [row_spec(GMLP_WIDTH)]
    if emit_vn:
        out_shape.append(jax.ShapeDtypeStruct((n_rows, GMLP_WIDTH), F32))
        out_specs.append(row_spec(GMLP_WIDTH))
    return pl.pallas_call(
        functools.partial(_project_kernel, rows=rows, tail_from=tail_from),
        grid=(grid,),
        in_specs=[
            row_spec(D_MODEL),
            _const_spec((1, D_MODEL)),
            _const_spec(w_in_b.shape),
            row_spec(3 * LANES),
            _const_spec((1, GMLP_WIDTH)),
            _const_spec((1, GMLP_WIDTH)),
            _const_spec(wsp.shape),
            _const_spec(bsp.shape),
        ],
        out_specs=out_specs,
        out_shape=out_shape,
        compiler_params=pltpu.CompilerParams(
            dimension_semantics=("arbitrary",), vmem_limit_bytes=VMEM_LIMIT_BYTES),
        name="project",
    )(x, g1, w_in_b, rope_tab, ln_g, ln_b, wsp, bsp)


def _attend_unit(qs, ks, vs, mask, state, first, last):
    mq = qs[0].shape[0]
    nk = ks[0].shape[0]
    klane = lax.broadcasted_iota(jnp.int32, (nk, LANES), 1) < HEAD_DIM
    olane = lax.broadcasted_iota(jnp.int32, (mq, LANES), 1) < HEAD_DIM
    kzero = jnp.zeros((nk, LANES), BF16)
    out = []
    for pr in range(PAIRS):
        halves = []
        for hh in range(2):
            keep = klane if hh == 0 else jnp.logical_not(klane)
            km = jnp.where(keep, ks[pr], kzero)
            s = lax.dot_general(qs[pr], km, (((1,), (1,)), ((), ())), preferred_element_type=F32)
            s = jnp.where(mask, s, NEG)
            m = jnp.max(s, axis=-1, keepdims=True)
            p = jnp.exp(s - m)
            l = jnp.sum(p, axis=-1, keepdims=True)
            pv = jnp.dot(p.astype(BF16), vs[pr], preferred_element_type=F32)
            halves.append((pv, m, l))
        pv = jnp.where(olane, halves[0][0], halves[1][0])
        m = jnp.where(olane, halves[0][1], halves[1][1])
        l = jnp.where(olane, halves[0][2], halves[1][2])
        if not first:
            acc0, m0, l0 = state[pr]
            m_new = jnp.maximum(m0, m)
            a = jnp.exp(m0 - m_new)
            b = jnp.exp(m - m_new)
            pv = acc0 * a + pv * b
            l = l0 * a + l * b
            m = m_new
        out.append(pv / l if last else (pv, m, l))
    return out


def _prompt_attn_kernel(q_ref, k_ref, v_ref, o_ref, kc_ref, vc_ref, acc_ref, m_ref, l_ref):
    i = pl.program_id(0)
    rows = q_ref.shape[1]

    @pl.when(i == 0)
    def _():
        kc_ref[:, 0:rows, :] = jnp.zeros((N_CLASS, rows, ATTN_WIDTH), BF16)
        vc_ref[:, 0:rows, :] = jnp.zeros((N_CLASS, rows, ATTN_WIDTH), BF16)

    @pl.when(i > 0)
    def _():
        kc_ref[:, 0:rows, :] = kc_ref[:, rows:2 * rows, :]
        vc_ref[:, 0:rows, :] = vc_ref[:, rows:2 * rows, :]

    kc_ref[:, rows:2 * rows, :] = k_ref[...]
    vc_ref[:, rows:2 * rows, :] = v_ref[...]
    has_prev = i > 0

    def lanes(pr):
        return slice(pr * LANES, (pr + 1) * LANES)

    qi = lax.broadcasted_iota(jnp.int32, (rows, 2 * rows), 0)
    kj = lax.broadcasted_iota(jnp.int32, (rows, 2 * rows), 1)
    diff = qi + rows - kj
    mask16 = (diff >= 0) & (diff <= SPAN) & ((kj >= rows) | has_prev)

    def body16(r, carry):
        qs = [q_ref[r, :, lanes(pr)] for pr in range(PAIRS)]
        ks = [kc_ref[r, :, lanes(pr)] for pr in range(PAIRS)]
        vs = [vc_ref[r, :, lanes(pr)] for pr in range(PAIRS)]
        res = _attend_unit(qs, ks, vs, mask16, None, True, False)
        for pr in range(PAIRS):
            acc_ref[r, :, lanes(pr)] = res[pr][0]
            m_ref[r, :, lanes(pr)] = res[pr][1]
            l_ref[r, :, lanes(pr)] = res[pr][2]
        return carry

    lax.fori_loop(0, N_CLASS, body16, 0)

    n4 = N_CLASS // 4
    qb = rows // n4
    qa = lax.broadcasted_iota(jnp.int32, (n4 * qb, n4 * 2 * qb), 0)
    ka = lax.broadcasted_iota(jnp.int32, (n4 * qb, n4 * 2 * qb), 1)
    d4 = 4 * (qa % qb - ka % (2 * qb) + qb) + (qa // qb - ka // (2 * qb))
    band4 = (d4 >= 0) & (d4 <= SPAN)

    def body4(t, carry):
        c = t // n4
        b = t % n4
        q0 = pl.multiple_of(b * qb, qb)
        k0 = pl.multiple_of(rows - qb + b * qb, qb)
        mask = band4 & ((ka % (2 * qb) + b * qb >= qb) | has_prev)
        slabs = [c + 4 * a for a in range(n4)]
        cat = lambda ref, r0, n, pr: jnp.concatenate([ref[sb, pl.ds(r0, n), lanes(pr)] for sb in slabs], axis=0)
        qs = [cat(q_ref, q0, qb, pr) for pr in range(PAIRS)]
        ks = [cat(kc_ref, k0, 2 * qb, pr) for pr in range(PAIRS)]
        vs = [cat(vc_ref, k0, 2 * qb, pr) for pr in range(PAIRS)]
        state = [(cat(acc_ref, q0, qb, pr), cat(m_ref, q0, qb, pr), cat(l_ref, q0, qb, pr)) for pr in range(PAIRS)]
        res = _attend_unit(qs, ks, vs, mask, state, False, False)
        for pr in range(PAIRS):
            for a, sb in enumerate(slabs):
                rs = slice(a * qb, (a + 1) * qb)
                acc_ref[sb, pl.ds(q0, qb), lanes(pr)] = res[pr][0][rs]
                m_ref[sb, pl.ds(q0, qb), lanes(pr)] = res[pr][1][rs]
                l_ref[sb, pl.ds(q0, qb), lanes(pr)] = res[pr][2][rs]
        return carry

    lax.fori_loop(0, N_CLASS, body4, 0)

    q1 = 2 * rows // N_CLASS
    qa1 = lax.broadcasted_iota(jnp.int32, (N_CLASS * q1, N_CLASS * 2 * q1), 0)
    ka1 = lax.broadcasted_iota(jnp.int32, (N_CLASS * q1, N_CLASS * 2 * q1), 1)
    d1 = N_CLASS * (qa1 % q1 - ka1 % (2 * q1) + q1) + (qa1 // q1 - ka1 // (2 * q1))
    band1 = (d1 >= 0) & (d1 <= SPAN)

    def body1(b, carry):
        q0 = pl.multiple_of(b * q1, q1)
        k0 = pl.multiple_of(rows - q1 + b * q1, q1)
        mask = band1 & ((ka1 % (2 * q1) + b * q1 >= q1) | has_prev)
        cat = lambda ref, r0, n, pr: jnp.concatenate(
            [ref[sb, pl.ds(r0, n), lanes(pr)] for sb in range(N_CLASS)], axis=0)
        qs = [cat(q_ref, q0, q1, pr) for pr in range(PAIRS)]
        ks = [cat(kc_ref, k0, 2 * q1, pr) for pr in range(PAIRS)]
        vs = [cat(vc_ref, k0, 2 * q1, pr) for pr in range(PAIRS)]
        state = [(cat(acc_ref, q0, q1, pr), cat(m_ref, q0, q1, pr), cat(l_ref, q0, q1, pr)) for pr in range(PAIRS)]
        res = _attend_unit(qs, ks, vs, mask, state, False, True)
        for pr in range(PAIRS):
            for sb in range(N_CLASS):
                o_ref[sb, pl.ds(q0, q1), lanes(pr)] = res[pr][sb * q1:(sb + 1) * q1].astype(BF16)
        return carry

    lax.fori_loop(0, rows // q1, body1, 0)


def _prompt_attention(q_cm, k_cm, v_cm):
    n_slab_rows = q_cm.shape[1]
    spec = pl.BlockSpec((N_CLASS, SPAN, ATTN_WIDTH), lambda i: (0, i, 0))
    return pl.pallas_call(
        _prompt_attn_kernel,
        grid=(n_slab_rows // SPAN,),
        in_specs=[spec, spec, spec],
        out_specs=spec,
        out_shape=jax.ShapeDtypeStruct(q_cm.shape, BF16),
        scratch_shapes=[
            pltpu.VMEM((N_CLASS, 2 * SPAN, ATTN_WIDTH), BF16),
            pltpu.VMEM((N_CLASS, 2 * SPAN, ATTN_WIDTH), BF16),
            pltpu.VMEM((N_CLASS, SPAN, ATTN_WIDTH), F32),
            pltpu.VMEM((N_CLASS, SPAN, ATTN_WIDTH), F32),
            pltpu.VMEM((N_CLASS, SPAN, ATTN_WIDTH), F32),
        ],
        compiler_params=pltpu.CompilerParams(
            dimension_semantics=("arbitrary",), vmem_limit_bytes=VMEM_LIMIT_BYTES),
        name="prompt_attention",
    )(q_cm, k_cm, v_cm)


def _sample_attn_kernel(q_ref, kn_ref, vn_ref, kt_ref, vt_ref, o_ref, *, t_new, lbuf):
    width = ATTN_WIDTH
    nrow = t_new * N_HEADS
    q = q_ref[...]
    kn = kn_ref[...]
    vnew = vn_ref[...]

    row = lax.broadcasted_iota(jnp.int32, (nrow, width), 0)
    lane = lax.broadcasted_iota(jnp.int32, (nrow, width), 1)
    hmask = (lane // HEAD_DIM) == (row % N_HEADS)
    qtok = jnp.concatenate([jnp.broadcast_to(q[j:j + 1, :], (N_HEADS, width)) for j in range(t_new)], axis=0)
    qrows = jnp.where(hmask, qtok, 0.0)

    kt = kt_ref[...].reshape(width, lbuf).astype(BF16)
    vt = vt_ref[...].reshape(width, lbuf).astype(BF16)
    s = jnp.dot(qrows.astype(BF16), kt, preferred_element_type=F32)

    def reach_count(delta):
        cnt = jnp.zeros(delta.shape, F32)
        for window, dil in DILATIONS:
            hit = (delta >= 0) & (delta % dil == 0) & (delta <= window)
            cnt = cnt + jnp.where(hit, 1.0, 0.0)
        return cnt

    pos = lax.broadcasted_iota(jnp.int32, (nrow, lbuf), 1)
    tok = lax.broadcasted_iota(jnp.int32, (nrow, lbuf), 0) // N_HEADS
    cnt = reach_count(lbuf + tok - pos)
    tok1 = tok[:, 0:1]
    cnew = [reach_count(tok1 - j) for j in range(t_new)]
    snew = [jnp.sum(qrows * kn[j:j + 1, :], axis=-1, keepdims=True) for j in range(t_new)]

    s = jnp.where(cnt > 0.0, s, NEG)
    m = jnp.max(s, axis=-1, keepdims=True)
    for j in range(t_new):
        m = jnp.maximum(m, jnp.where(cnew[j] > 0.0, snew[j], NEG))
    e = jnp.exp(s - m) * cnt
    l = jnp.sum(e, axis=-1, keepdims=True)
    o = lax.dot_general(e.astype(BF16), vt, (((1,), (1,)), ((), ())), preferred_element_type=F32)
    for j in range(t_new):
        ej = jnp.where(cnew[j] > 0.0, jnp.exp(snew[j] - m), 0.0) * cnew[j]
        l = l + ej
        o = o + ej * vnew[j:j + 1, :]
    o = jnp.where(hmask, o / l, 0.0)
    for j in range(t_new):
        o_ref[j:j + 1, :] = jnp.sum(o[j * N_HEADS:(j + 1) * N_HEADS, :], axis=0, keepdims=True)


def _sample_attention(q, k_new, v_new, cache_kt, cache_vt):
    nb, t_new, width = q.shape
    lbuf = cache_kt.shape[-1]
    pad = lambda a: jnp.pad(a, ((0, 0), (0, 8 - t_new), (0, 0)))
    new_spec = pl.BlockSpec((None, 8, width), lambda b: (b, 0, 0))
    cache_spec = pl.BlockSpec((None, N_HEADS, HEAD_DIM, lbuf), lambda b: (b, 0, 0, 0))
    return pl.pallas_call(
        functools.partial(_sample_attn_kernel, t_new=t_new, lbuf=lbuf),
        grid=(nb,),
        in_specs=[new_spec, new_spec, new_spec, cache_spec, cache_spec],
        out_specs=pl.BlockSpec((None, t_new, width), lambda b: (b, 0, 0)),
        out_shape=jax.ShapeDtypeStruct((nb, t_new, width), F32),
        compiler_params=pltpu.CompilerParams(
            dimension_semantics=("arbitrary",), vmem_limit_bytes=VMEM_LIMIT_BYTES),
        name="sample_attention",
    )(pad(q), pad(k_new), pad(v_new), cache_kt, cache_vt)


def _finish_kernel(x_ref, attn_ref, gated_ref, p_ref, wo_ref, g2_ref, wup_ref, wdn_ref,
                   gg_ref, wg_ref, wple_ref, gf_ref, y_ref, *, apply_final):
    mix = jnp.concatenate([attn_ref[...], gated_ref[...]], axis=-1)
    h = x_ref[...] + jnp.dot(mix, wo_ref[...], preferred_element_type=F32)
    n2 = _rms(h, g2_ref[...]).astype(BF16)
    f = jnp.zeros_like(h)
    for c in range(D_FF // FF_CHUNK):
        cs = slice(c * FF_CHUNK, (c + 1) * FF_CHUNK)
        up = jnp.dot(n2, wup_ref[:, cs], preferred_element_type=F32)
        act = jnp.square(jnp.maximum(up, 0.0)).astype(BF16)
        f = f + jnp.dot(act, wdn_ref[cs, :], preferred_element_type=F32)
    h = h + f
    gate = jax.nn.sigmoid(jnp.dot(_rms(h, gg_ref[...]).astype(BF16), wg_ref[...], preferred_element_type=F32))
    ple = jnp.dot(p_ref[...].astype(BF16), wple_ref[...], preferred_element_type=F32)
    h = h + gate * ple
    y_ref[...] = _rms(h, gf_ref[...]) if apply_final else h


def _finish(x, attn, gated, p, wo, g2, wup, wdn, gg, wg, wple, gf, *, rows, apply_final):
    n_rows = x.shape[0]
    row_spec = lambda width: pl.BlockSpec((rows, width), lambda i: (i, 0))
    return pl.pallas_call(
        functools.partial(_finish_kernel, apply_final=apply_final),
        grid=(n_rows // rows,),
        in_specs=[
            row_spec(D_MODEL), row_spec(ATTN_WIDTH), row_spec(GMLP_WIDTH), row_spec(PLE_DIM),
            _const_spec(wo.shape), _const_spec((1, D_MODEL)), _const_spec(wup.shape), _const_spec(wdn.shape),
            _const_spec((1, D_MODEL)), _const_spec(wg.shape), _const_spec(wple.shape), _const_spec((1, D_MODEL)),
        ],
        out_specs=row_spec(D_MODEL),
        out_shape=jax.ShapeDtypeStruct((n_rows, D_MODEL), F32),
        compiler_params=pltpu.CompilerParams(
            dimension_semantics=("arbitrary",), vmem_limit_bytes=VMEM_LIMIT_BYTES),
        name="finish",
    )(x, attn, gated, p, wo, g2, wup, wdn, gg, wg, wple, gf)


def _rope_table(pos):
    half = ROT_DIM // 2
    inv_freq = ROPE_THETA ** (-jnp.arange(0, ROT_DIM, 2, dtype=F32) / ROT_DIM)
    ang = pos.astype(F32)[:, None] * inv_freq[None, :]
    cos, sin = jnp.cos(ang), jnp.sin(ang)
    t = pos.shape[0]
    rest = jnp.zeros((t, HEAD_DIM - ROT_DIM), F32)
    z = jnp.zeros((t, half), F32)
    per_head = [
        jnp.concatenate([cos, cos, rest + 1.0], axis=-1),
        jnp.concatenate([-sin, z, rest], axis=-1),
        jnp.concatenate([z, sin, rest], axis=-1),
    ]
    return jnp.concatenate([jnp.tile(a, (1, LANES // HEAD_DIM)) for a in per_head], axis=-1)


def _pair_spatial(w):
    g, l, _ = w.shape
    return w.reshape(g // 2, 2, l, l).transpose(0, 2, 1, 3).reshape(g // 2, l, 2 * l).astype(BF16)


def _bias_lanes(b):
    return jnp.repeat(b.T, GMLP_WIDTH // GMLP_GROUPS, axis=1)


def kernel(x_prompt, x_sample, cache_k, cache_v, p_prompt, p_sample, norm1_g, w_in, ln_v_g, ln_v_b,
           w_spatial, b_spatial, w_out, norm2_g, w_up, w_down, gate_norm_g, w_gate, w_ple, final_g):
    depth = w_in.shape[0]
    nbp, seq, _ = x_prompt.shape
    nbs, t_new, _ = x_sample.shape
    lbuf = cache_k.shape[2]
    keep = min(WINDOW_MAX, seq)
    assert nbp == 1 and seq % (N_CLASS * SPAN) == 0 and (nbs * t_new) % CHUNK == 0 and t_new <= CHUNK
    n_s = nbs * t_new

    rope_p = _rope_table(jnp.arange(seq, dtype=F32))
    rope_s = jnp.tile(_rope_table(PAST_LEN + jnp.arange(t_new, dtype=F32)), (nbs, 1))
    row2 = lambda a: a.reshape(1, -1)

    hp = x_prompt.reshape(seq, D_MODEL)
    hs = x_sample.reshape(n_s, D_MODEL)
    nk_p, nv_p, nk_s, nv_s, nvc_s = [], [], [], [], []
    for i in range(depth):
        last = i == depth - 1
        w_in_b = w_in[i].astype(BF16)
        wo_b, wup_b, wdn_b = w_out[i].astype(BF16), w_up[i].astype(BF16), w_down[i].astype(BF16)
        wg_b, wple_b = w_gate[i].astype(BF16), w_ple[i].astype(BF16)
        fin_w = (wo_b, row2(norm2_g[i]), wup_b, wdn_b, row2(gate_norm_g[i]), wg_b, wple_b, row2(final_g))
        ln_g, ln_b = row2(ln_v_g[i]), row2(ln_v_b[i])

        wsp_p = _pair_spatial(w_spatial[i])
        bsp_p = _bias_lanes(b_spatial[i])
        q, k, v, kf, vf, gated = _project(hp, row2(norm1_g[i]), w_in_b, rope_p, ln_g, ln_b, wsp_p, bsp_p,
                                          rows=512, tail_rows=keep, emit_vn=False)
        to_cm = lambda a: a.reshape(seq // N_CLASS, N_CLASS, ATTN_WIDTH).transpose(1, 0, 2)
        attn_cm = _prompt_attention(to_cm(q), to_cm(k), to_cm(v))
        attn = attn_cm.transpose(1, 0, 2).reshape(seq, ATTN_WIDTH)
        hp = _finish(hp, attn, gated, p_prompt[i].reshape(seq, PLE_DIM), *fin_w, rows=512, apply_final=last)
        nk_p.append(kf.reshape(nbp, keep, N_HEADS, HEAD_DIM))
        nv_p.append(vf.reshape(nbp, keep, N_HEADS, HEAD_DIM))

        corner = w_spatial[i][:, :t_new, :t_new]
        wsp_s = _pair_spatial(jax.vmap(lambda c: jnp.kron(jnp.eye(n_s // t_new, dtype=F32), c))(corner))
        bsp_s = _bias_lanes(jnp.tile(b_spatial[i][:, :t_new], (1, n_s // t_new)))
        q, k, v, kf, vf, gated, vn = _project(hs, row2(norm1_g[i]), w_in_b, rope_s, ln_g, ln_b, wsp_s, bsp_s,
                                              rows=n_s, tail_rows=n_s, emit_vn=True)
        r3 = lambda a: a.reshape(nbs, t_new, ATTN_WIDTH)
        attn = _sample_attention(r3(q).astype(F32), r3(kf), r3(vf),
                                 cache_k[i].transpose(0, 2, 3, 1), cache_v[i].transpose(0, 2, 3, 1))
        hs = _finish(hs, attn.reshape(n_s, ATTN_WIDTH).astype(BF16), gated, p_sample[i].reshape(n_s, PLE_DIM),
                     *fin_w, rows=n_s, apply_final=last)
        nk_s.append(kf.reshape(nbs, t_new, N_HEADS, HEAD_DIM))
        nv_s.append(vf.reshape(nbs, t_new, N_HEADS, HEAD_DIM))
        nvc_s.append(vn.reshape(nbs, t_new, GMLP_WIDTH))

    return (hp.reshape(nbp, seq, D_MODEL), hs.reshape(nbs, t_new, D_MODEL),
            jnp.stack(nk_p), jnp.stack(nv_p), jnp.stack(nk_s), jnp.stack(nv_s), jnp.stack(nvc_s))
```

```python
import functools

import jax
import jax.numpy as jnp
from jax import lax
from jax.experimental import pallas as pl
from jax.experimental.pallas import tpu as pltpu

D_MODEL = 1024
N_HEADS = 8
HEAD_DIM = 64
ATTN_WIDTH = N_HEADS * HEAD_DIM
GMLP_GROUPS = 8
GMLP_WIDTH = 512
CHUNK = 128
DILATIONS = ((128, 1), (512, 4), (2048, 16))
WINDOW_MAX = 2048
PAST_LEN = 16384
ROT_DIM = HEAD_DIM // 4
ROPE_THETA = 500000.0
D_FF = 4 * D_MODEL
PLE_DIM = 256
EPS = 1e-6
NEG = -1e30

LANES = 128
VMEM_LIMIT_BYTES = 56 * 1024 * 1024

N_CLASS = 16
SPAN = 128
PAIRS = ATTN_WIDTH // LANES
FF_CHUNK = 1024

BF16 = jnp.bfloat16
F32 = jnp.float32


def _rms(x, g):
    ms = jnp.mean(x * x, axis=-1, keepdims=True)
    return x * lax.rsqrt(ms + EPS) * g


def _const_spec(shape):
    nd = len(shape)
    return pl.BlockSpec(shape, lambda *_: (0,) * nd, pipeline_mode=pl.Buffered(1))


def _project_kernel(x_ref, g1_ref, w_ref, invf_ref, lng_ref, lnb_ref, wsp_ref, bsp_ref,
                    q_ref, k_ref, v_ref, kf_ref, vf_ref, gated_ref, *rest,
                    rows, tail_from, pos_base, pos_period, class_major, emit_vn):
    rest = list(rest)
    vn_ref = rest.pop(0) if emit_vn else None
    cos_off_ref, sin_off_ref = rest.pop(0), rest.pop(0)
    zs_ref = rest.pop(0) if class_major else None
    i = pl.program_id(0)

    @pl.when(i == 0)
    def _():
        off = (lax.broadcasted_iota(jnp.int32, (rows, LANES), 0) % pos_period).astype(F32)
        ang = off * invf_ref[...]
        cos_off_ref[...] = jnp.cos(ang)
        sin_off_ref[...] = jnp.sin(ang)

    tile_stride = rows if pos_period == rows else 0
    base = (i * tile_stride).astype(F32) + pos_base
    base_ang = base * invf_ref[...]
    cb, sb = jnp.cos(base_ang), jnp.sin(base_ang)
    co, so = cos_off_ref[...], sin_off_ref[...]
    cosf = cb * co - sb * so
    sint = sb * co + cb * so
    head_lane = lax.broadcasted_iota(jnp.int32, (1, LANES), 1) % HEAD_DIM
    sina = jnp.where(head_lane < ROT_DIM // 2, -sint, 0.0)
    sinb = jnp.where((head_lane >= ROT_DIM // 2) & (head_lane < ROT_DIM), sint, 0.0)

    def rope(z):
        return z * cosf + pltpu.roll(z, LANES - ROT_DIM // 2, 1) * sina + pltpu.roll(z, ROT_DIM // 2, 1) * sinb

    xn = _rms(x_ref[...], g1_ref[...]).astype(BF16)
    zq = jnp.dot(xn, w_ref[:, 0:ATTN_WIDTH], preferred_element_type=F32)
    zk = jnp.dot(xn, w_ref[:, ATTN_WIDTH:2 * ATTN_WIDTH], preferred_element_type=F32)
    zv = jnp.dot(xn, w_ref[:, 2 * ATTN_WIDTH:3 * ATTN_WIDTH], preferred_element_type=F32)
    slabs = [slice(s * LANES, (s + 1) * LANES) for s in range(PAIRS)]
    qr = [rope(zq[:, sl]) * (HEAD_DIM ** -0.5) for sl in slabs]
    kr = [rope(zk[:, sl]) for sl in slabs]

    @pl.when(i >= tail_from)
    def _():
        for s, sl in enumerate(slabs):
            kf_ref[:, sl] = kr[s]
        vf_ref[...] = zv

    if class_major:
        per_class = rows // N_CLASS
        for s, sl in enumerate(slabs):
            zs_ref[s] = qr[s]
            zs_ref[PAIRS + s] = kr[s]
            zs_ref[2 * PAIRS + s] = zv[:, sl]
        for r in range(N_CLASS):
            for s, sl in enumerate(slabs):
                pick = lambda n: zs_ref[n, pl.ds(r, per_class, stride=N_CLASS), :].astype(BF16)
                q_ref[r, :, sl] = pick(s)
                k_ref[r, :, sl] = pick(PAIRS + s)
                v_ref[r, :, sl] = pick(2 * PAIRS + s)
    else:
        for s, sl in enumerate(slabs):
            q_ref[:, sl] = qr[s].astype(BF16)
            k_ref[:, sl] = kr[s].astype(BF16)
        v_ref[...] = zv.astype(BF16)

    o_u = 3 * ATTN_WIDTH
    u = jax.nn.gelu(jnp.dot(xn, w_ref[:, o_u:o_u + GMLP_WIDTH], preferred_element_type=F32))
    vc = jax.nn.gelu(jnp.dot(xn, w_ref[:, o_u + GMLP_WIDTH:o_u + 2 * GMLP_WIDTH], preferred_element_type=F32))
    mu = jnp.mean(vc, axis=-1, keepdims=True)
    cen = vc - mu
    var = jnp.mean(cen * cen, axis=-1, keepdims=True)
    vn = cen * lax.rsqrt(var + EPS) * lng_ref[...] + lnb_ref[...]
    if emit_vn:
        vn_ref[...] = vn
    vnb = vn.astype(BF16)

    row = lax.broadcasted_iota(jnp.int32, (CHUNK, 2 * CHUNK), 0)
    col = lax.broadcasted_iota(jnp.int32, (CHUNK, 2 * CHUNK), 1)
    tril = (col % CHUNK) <= row
    lane = lax.broadcasted_iota(jnp.int32, (CHUNK, LANES), 1)
    first = lane < HEAD_DIM
    zero = jnp.zeros((CHUNK, LANES), BF16)
    for s in range(GMLP_WIDTH // LANES):
        sl = slice(s * LANES, (s + 1) * LANES)
        wp = jnp.where(tril, wsp_ref[s], jnp.zeros((), BF16))
        bias = bsp_ref[:, sl]
        for c in range(rows // CHUNK):
            rs = slice(c * CHUNK, (c + 1) * CHUNK)
            vblk = vnb[rs, sl]
            rhs = jnp.concatenate([jnp.where(first, vblk, zero), jnp.where(first, zero, vblk)], axis=0)
            mixed = jnp.dot(wp, rhs, preferred_element_type=F32) + bias
            gated_ref[rs, sl] = (u[rs, sl] * mixed).astype(BF16)


def _project(x, g1, w_in_b, ln_g, ln_b, wsp, bsp, *, rows, tail_rows, pos_base, pos_period, class_major, emit_vn):
    n_rows = x.shape[0]
    grid = n_rows // rows
    tail_from = (n_rows - tail_rows) // rows
    row_spec = lambda width: pl.BlockSpec((rows, width), lambda i: (i, 0))
    tail_spec = pl.BlockSpec((rows, ATTN_WIDTH), lambda i: (jnp.maximum(i - tail_from, 0), 0))
    if class_major:
        qkv_shape = jax.ShapeDtypeStruct((N_CLASS, n_rows // N_CLASS, ATTN_WIDTH), BF16)
        qkv_spec = pl.BlockSpec((N_CLASS, rows // N_CLASS, ATTN_WIDTH), lambda i: (0, i, 0))
    else:
        qkv_shape = jax.ShapeDtypeStruct((n_rows, ATTN_WIDTH), BF16)
        qkv_spec = row_spec(ATTN_WIDTH)
    out_shape = [
        qkv_shape,
        qkv_shape,
        qkv_shape,
        jax.ShapeDtypeStruct((tail_rows, ATTN_WIDTH), F32),
        jax.ShapeDtypeStruct((tail_rows, ATTN_WIDTH), F32),
        jax.ShapeDtypeStruct((n_rows, GMLP_WIDTH), BF16),
    ]
    out_specs = [qkv_spec] * 3 + [tail_spec] * 2 + [row_spec(GMLP_WIDTH)]
    if emit_vn:
        out_shape.append(jax.ShapeDtypeStruct((n_rows, GMLP_WIDTH), F32))
        out_specs.append(row_spec(GMLP_WIDTH))
    scratch = [pltpu.VMEM((rows, LANES), F32), pltpu.VMEM((rows, LANES), F32)]
    if class_major:
        scratch.append(pltpu.VMEM((3 * PAIRS, rows, LANES), F32))
    inv_freq = ROPE_THETA ** (-jnp.arange(0, ROT_DIM, 2, dtype=F32) / ROT_DIM)
    per_head = jnp.concatenate([inv_freq, inv_freq, jnp.zeros((HEAD_DIM - ROT_DIM,), F32)])
    invf = jnp.tile(per_head, LANES // HEAD_DIM).reshape(1, LANES)
    return pl.pallas_call(
        functools.partial(_project_kernel, rows=rows, tail_from=tail_from, pos_base=float(pos_base),
                          pos_period=pos_period, class_major=class_major, emit_vn=emit_vn),
        grid=(grid,),
        in_specs=[
            row_spec(D_MODEL),
            _const_spec((1, D_MODEL)),
            _const_spec(w_in_b.shape),
            _const_spec((1, LANES)),
            _const_spec((1, GMLP_WIDTH)),
            _const_spec((1, GMLP_WIDTH)),
            _const_spec(wsp.shape),
            _const_spec(bsp.shape),
        ],
        out_specs=out_specs,
        out_shape=out_shape,
        scratch_shapes=scratch,
        compiler_params=pltpu.CompilerParams(
            dimension_semantics=("arbitrary",), vmem_limit_bytes=VMEM_LIMIT_BYTES),
        name="project",
    )(x, g1, w_in_b, invf, ln_g, ln_b, wsp, bsp)


def _attend_unit(qs, ks, vs, mask, state, first, last):
    mq = qs[0].shape[0]
    nk = ks[0].shape[0]
    klane = lax.broadcasted_iota(jnp.int32, (nk, LANES), 1) < HEAD_DIM
    olane = lax.broadcasted_iota(jnp.int32, (mq, LANES), 1) < HEAD_DIM
    kzero = jnp.zeros((nk, LANES), BF16)
    out = []
    for pr in range(PAIRS):
        halves = []
        for hh in range(2):
            keep = klane if hh == 0 else jnp.logical_not(klane)
            km = jnp.where(keep, ks[pr], kzero)
            s = lax.dot_general(qs[pr], km, (((1,), (1,)), ((), ())), preferred_element_type=F32)
            s = jnp.where(mask, s, NEG)
            m = jnp.max(s, axis=-1, keepdims=True)
            p = jnp.exp(s - m)
            l = jnp.sum(p, axis=-1, keepdims=True)
            pv = jnp.dot(p.astype(BF16), vs[pr], preferred_element_type=F32)
            halves.append((pv, m, l))
        pv = jnp.where(olane, halves[0][0], halves[1][0])
        m = jnp.where(olane, halves[0][1], halves[1][1])
        l = jnp.where(olane, halves[0][2], halves[1][2])
        if not first:
            acc0, m0, l0 = state[pr]
            m_new = jnp.maximum(m0, m)
            a = jnp.exp(m0 - m_new)
            b = jnp.exp(m - m_new)
            pv = acc0 * a + pv * b
            l = l0 * a + l * b
            m = m_new
        out.append(pv / l if last else (pv, m, l))
    return out


def _prompt_attn_kernel(q_ref, k_ref, v_ref, o_ref, kc_ref, vc_ref, acc_ref, m_ref, l_ref):
    i = pl.program_id(0)
    rows = q_ref.shape[1]

    @pl.when(i == 0)
    def _():
        kc_ref[:, 0:rows, :] = jnp.zeros((N_CLASS, rows, ATTN_WIDTH), BF16)
        vc_ref[:, 0:rows, :] = jnp.zeros((N_CLASS, rows, ATTN_WIDTH), BF16)

    @pl.when(i > 0)
    def _():
        kc_ref[:, 0:rows, :] = kc_ref[:, rows:2 * rows, :]
        vc_ref[:, 0:rows, :] = vc_ref[:, rows:2 * rows, :]

    kc_ref[:, rows:2 * rows, :] = k_ref[...]
    vc_ref[:, rows:2 * rows, :] = v_ref[...]
    has_prev = i > 0

    def lanes(pr):
        return slice(pr * LANES, (pr + 1) * LANES)

    qi = lax.broadcasted_iota(jnp.int32, (rows, 2 * rows), 0)
    kj = lax.broadcasted_iota(jnp.int32, (rows, 2 * rows), 1)
    diff = qi + rows - kj
    mask16 = (diff >= 0) & (diff <= SPAN) & ((kj >= rows) | has_prev)

    def body16(r, carry):
        qs = [q_ref[r, :, lanes(pr)] for pr in range(PAIRS)]
        ks = [kc_ref[r, :, lanes(pr)] for pr in range(PAIRS)]
        vs = [vc_ref[r, :, lanes(pr)] for pr in range(PAIRS)]
        res = _attend_unit(qs, ks, vs, mask16, None, True, False)
        for pr in range(PAIRS):
            acc_ref[r, :, lanes(pr)] = res[pr][0]
            m_ref[r, :, lanes(pr)] = res[pr][1]
            l_ref[r, :, lanes(pr)] = res[pr][2]
        return carry

    lax.fori_loop(0, N_CLASS, body16, 0)

    n4 = N_CLASS // 4
    qb = rows // n4
    qa = lax.broadcasted_iota(jnp.int32, (n4 * qb, n4 * 2 * qb), 0)
    ka = lax.broadcasted_iota(jnp.int32, (n4 * qb, n4 * 2 * qb), 1)
    d4 = 4 * (qa % qb - ka % (2 * qb) + qb) + (qa // qb - ka // (2 * qb))
    band4 = (d4 >= 0) & (d4 <= SPAN)

    def body4(t, carry):
        c = t // n4
        b = t % n4
        q0 = pl.multiple_of(b * qb, qb)
        k0 = pl.multiple_of(rows - qb + b * qb, qb)
        mask = band4 & ((ka % (2 * qb) + b * qb >= qb) | has_prev)
        slabs = [c + 4 * a for a in range(n4)]
        cat = lambda ref, r0, n, pr: jnp.concatenate([ref[sb, pl.ds(r0, n), lanes(pr)] for sb in slabs], axis=0)
        qs = [cat(q_ref, q0, qb, pr) for pr in range(PAIRS)]
        ks = [cat(kc_ref, k0, 2 * qb, pr) for pr in range(PAIRS)]
        vs = [cat(vc_ref, k0, 2 * qb, pr) for pr in range(PAIRS)]
        state = [(cat(acc_ref, q0, qb, pr), cat(m_ref, q0, qb, pr), cat(l_ref, q0, qb, pr)) for pr in range(PAIRS)]
        res = _attend_unit(qs, ks, vs, mask, state, False, False)
        for pr in range(PAIRS):
            for a, sb in enumerate(slabs):
                rs = slice(a * qb, (a + 1) * qb)
                acc_ref[sb, pl.ds(q0, qb), lanes(pr)] = res[pr][0][rs]
                m_ref[sb, pl.ds(q0, qb), lanes(pr)] = res[pr][1][rs]
                l_ref[sb, pl.ds(q0, qb), lanes(pr)] = res[pr][2][rs]
        return carry

    lax.fori_loop(0, N_CLASS, body4, 0)

    q1 = 2 * rows // N_CLASS
    qa1 = lax.broadcasted_iota(jnp.int32, (N_CLASS * q1, N_CLASS * 2 * q1), 0)
    ka1 = lax.broadcasted_iota(jnp.int32, (N_CLASS * q1, N_CLASS * 2 * q1), 1)
    d1 = N_CLASS * (qa1 % q1 - ka1 % (2 * q1) + q1) + (qa1 // q1 - ka1 // (2 * q1))
    band1 = (d1 >= 0) & (d1 <= SPAN)

    def body1(b, carry):
        q0 = pl.multiple_of(b * q1, q1)
        k0 = pl.multiple_of(rows - q1 + b * q1, q1)
        mask = band1 & ((ka1 % (2 * q1) + b * q1 >= q1) | has_prev)
        cat = lambda ref, r0, n, pr: jnp.concatenate(
            [ref[sb, pl.ds(r0, n), lanes(pr)] for sb in range(N_CLASS)], axis=0)
        qs = [cat(q_ref, q0, q1, pr) for pr in range(PAIRS)]
        ks = [cat(kc_ref, k0, 2 * q1, pr) for pr in range(PAIRS)]
        vs = [cat(vc_ref, k0, 2 * q1, pr) for pr in range(PAIRS)]
        state = [(cat(acc_ref, q0, q1, pr), cat(m_ref, q0, q1, pr), cat(l_ref, q0, q1, pr)) for pr in range(PAIRS)]
        res = _attend_unit(qs, ks, vs, mask, state, False, True)
        for pr in range(PAIRS):
            for sb in range(N_CLASS):
                o_ref[sb, pl.ds(q0, q1), lanes(pr)] = res[pr][sb * q1:(sb + 1) * q1].astype(BF16)
        return carry

    lax.fori_loop(0, rows // q1, body1, 0)


def _prompt_attention(q_cm, k_cm, v_cm):
    n_slab_rows = q_cm.shape[1]
    spec = pl.BlockSpec((N_CLASS, SPAN, ATTN_WIDTH), lambda i: (0, i, 0))
    return pl.pallas_call(
        _prompt_attn_kernel,
        grid=(n_slab_rows // SPAN,),
        in_specs=[spec, spec, spec],
        out_specs=spec,
        out_shape=jax.ShapeDtypeStruct(q_cm.shape, BF16),
        scratch_shapes=[
            pltpu.VMEM((N_CLASS, 2 * SPAN, ATTN_WIDTH), BF16),
            pltpu.VMEM((N_CLASS, 2 * SPAN, ATTN_WIDTH), BF16),
            pltpu.VMEM((N_CLASS, SPAN, ATTN_WIDTH), F32),
            pltpu.VMEM((N_CLASS, SPAN, ATTN_WIDTH), F32),
            pltpu.VMEM((N_CLASS, SPAN, ATTN_WIDTH), F32),
        ],
        compiler_params=pltpu.CompilerParams(
            dimension_semantics=("arbitrary",), vmem_limit_bytes=VMEM_LIMIT_BYTES),
        name="prompt_attention",
    )(q_cm, k_cm, v_cm)


def _sample_attn_kernel(q_ref, kn_ref, vn_ref, kt_ref, vt_ref, o_ref, *, t_new, lbuf):
    width = ATTN_WIDTH
    nrow = t_new * N_HEADS
    q = q_ref[...]
    kn = kn_ref[...]
    vnew = vn_ref[...]

    row = lax.broadcasted_iota(jnp.int32, (nrow, width), 0)
    lane = lax.broadcasted_iota(jnp.int32, (nrow, width), 1)
    hmask = (lane // HEAD_DIM) == (row % N_HEADS)
    qtok = jnp.concatenate([jnp.broadcast_to(q[j:j + 1, :], (N_HEADS, width)) for j in range(t_new)], axis=0)
    qrows = jnp.where(hmask, qtok, 0.0)

    kt = kt_ref[...].reshape(width, lbuf).astype(BF16)
    vt = vt_ref[...].reshape(width, lbuf).astype(BF16)
    s = jnp.dot(qrows.astype(BF16), kt, preferred_element_type=F32)

    def reach_count(delta):
        cnt = jnp.zeros(delta.shape, F32)
        for window, dil in DILATIONS:
            hit = (delta >= 0) & (delta % dil == 0) & (delta <= window)
            cnt = cnt + jnp.where(hit, 1.0, 0.0)
        return cnt

    pos = lax.broadcasted_iota(jnp.int32, (nrow, lbuf), 1)
    tok = lax.broadcasted_iota(jnp.int32, (nrow, lbuf), 0) // N_HEADS
    cnt = reach_count(lbuf + tok - pos)
    tok1 = tok[:, 0:1]
    cnew = [reach_count(tok1 - j) for j in range(t_new)]
    snew = [jnp.sum(qrows * kn[j:j + 1, :], axis=-1, keepdims=True) for j in range(t_new)]

    s = jnp.where(cnt > 0.0, s, NEG)
    m = jnp.max(s, axis=-1, keepdims=True)
    for j in range(t_new):
        m = jnp.maximum(m, jnp.where(cnew[j] > 0.0, snew[j], NEG))
    e = jnp.exp(s - m) * cnt
    l = jnp.sum(e, axis=-1, keepdims=True)
    o = lax.dot_general(e.astype(BF16), vt, (((1,), (1,)), ((), ())), preferred_element_type=F32)
    for j in range(t_new):
        ej = jnp.where(cnew[j] > 0.0, jnp.exp(snew[j] - m), 0.0) * cnew[j]
        l = l + ej
        o = o + ej * vnew[j:j + 1, :]
    o = jnp.where(hmask, o / l, 0.0)
    for j in range(t_new):
        o_ref[j:j + 1, :] = jnp.sum(o[j * N_HEADS:(j + 1) * N_HEADS, :], axis=0, keepdims=True)


def _sample_attention(q, k_new, v_new, cache_kt, cache_vt):
    nb, t_new, width = q.shape
    lbuf = cache_kt.shape[-1]
    pad = lambda a: jnp.pad(a, ((0, 0), (0, 8 - t_new), (0, 0)))
    new_spec = pl.BlockSpec((None, 8, width), lambda b: (b, 0, 0))
    cache_spec = pl.BlockSpec((None, N_HEADS, HEAD_DIM, lbuf), lambda b: (b, 0, 0, 0))
    return pl.pallas_call(
        functools.partial(_sample_attn_kernel, t_new=t_new, lbuf=lbuf),
        grid=(nb,),
        in_specs=[new_spec, new_spec, new_spec, cache_spec, cache_spec],
        out_specs=pl.BlockSpec((None, t_new, width), lambda b: (b, 0, 0)),
        out_shape=jax.ShapeDtypeStruct((nb, t_new, width), F32),
        compiler_params=pltpu.CompilerParams(
            dimension_semantics=("arbitrary",), vmem_limit_bytes=VMEM_LIMIT_BYTES),
        name="sample_attention",
    )(pad(q), pad(k_new), pad(v_new), cache_kt, cache_vt)


def _finish_kernel(x_ref, attn_ref, gated_ref, p_ref, wo_ref, g2_ref, wup_ref, wdn_ref,
                   gg_ref, wg_ref, wple_ref, gf_ref, y_ref, *, apply_final):
    mix = jnp.concatenate([attn_ref[...], gated_ref[...]], axis=-1)
    h = x_ref[...] + jnp.dot(mix, wo_ref[...], preferred_element_type=F32)
    n2 = _rms(h, g2_ref[...]).astype(BF16)
    f = jnp.zeros_like(h)
    for c in range(D_FF // FF_CHUNK):
        cs = slice(c * FF_CHUNK, (c + 1) * FF_CHUNK)
        up = jnp.dot(n2, wup_ref[:, cs], preferred_element_type=F32)
        act = jnp.square(jnp.maximum(up, 0.0)).astype(BF16)
        f = f + jnp.dot(act, wdn_ref[cs, :], preferred_element_type=F32)
    h = h + f
    gate = jax.nn.sigmoid(jnp.dot(_rms(h, gg_ref[...]).astype(BF16), wg_ref[...], preferred_element_type=F32))
    ple = jnp.dot(p_ref[...].astype(BF16), wple_ref[...], preferred_element_type=F32)
    h = h + gate * ple
    y_ref[...] = _rms(h, gf_ref[...]) if apply_final else h


def _finish(x, attn, gated, p, wo, g2, wup, wdn, gg, wg, wple, gf, *, rows, apply_final):
    n_rows = x.shape[0]
    row_spec = lambda width: pl.BlockSpec((rows, width), lambda i: (i, 0))
    return pl.pallas_call(
        functools.partial(_finish_kernel, apply_final=apply_final),
        grid=(n_rows // rows,),
        in_specs=[
            row_spec(D_MODEL), row_spec(ATTN_WIDTH), row_spec(GMLP_WIDTH), row_spec(PLE_DIM),
            _const_spec(wo.shape), _const_spec((1, D_MODEL)), _const_spec(wup.shape), _const_spec(wdn.shape),
            _const_spec((1, D_MODEL)), _const_spec(wg.shape), _const_spec(wple.shape), _const_spec((1, D_MODEL)),
        ],
        out_specs=row_spec(D_MODEL),
        out_shape=jax.ShapeDtypeStruct((n_rows, D_MODEL), F32),
        compiler_params=pltpu.CompilerParams(
            dimension_semantics=("arbitrary",), vmem_limit_bytes=VMEM_LIMIT_BYTES),
        name="finish",
    )(x, attn, gated, p, wo, g2, wup, wdn, gg, wg, wple, gf)


def _pair_spatial(w):
    g, l, _ = w.shape
    return w.reshape(g // 2, 2, l, l).transpose(0, 2, 1, 3).reshape(g // 2, l, 2 * l).astype(BF16)


def _bias_lanes(b):
    return jnp.repeat(b.T, GMLP_WIDTH // GMLP_GROUPS, axis=1)


def kernel(x_prompt, x_sample, cache_k, cache_v, p_prompt, p_sample, norm1_g, w_in, ln_v_g, ln_v_b,
           w_spatial, b_spatial, w_out, norm2_g, w_up, w_down, gate_norm_g, w_gate, w_ple, final_g):
    depth = w_in.shape[0]
    nbp, seq, _ = x_prompt.shape
    nbs, t_new, _ = x_sample.shape
    lbuf = cache_k.shape[2]
    keep = min(WINDOW_MAX, seq)
    assert nbp == 1 and seq % (N_CLASS * SPAN) == 0 and (nbs * t_new) % CHUNK == 0 and t_new <= CHUNK
    n_s = nbs * t_new

    row2 = lambda a: a.reshape(1, -1)

    hp = x_prompt.reshape(seq, D_MODEL)
    hs = x_sample.reshape(n_s, D_MODEL)
    nk_p, nv_p, nk_s, nv_s, nvc_s = [], [], [], [], []
    for i in range(depth):
        last = i == depth - 1
        w_in_b = w_in[i].astype(BF16)
        wo_b, wup_b, wdn_b = w_out[i].astype(BF16), w_up[i].astype(BF16), w_down[i].astype(BF16)
        wg_b, wple_b = w_gate[i].astype(BF16), w_ple[i].astype(BF16)
        fin_w = (wo_b, row2(norm2_g[i]), wup_b, wdn_b, row2(gate_norm_g[i]), wg_b, wple_b, row2(final_g))
        ln_g, ln_b = row2(ln_v_g[i]), row2(ln_v_b[i])

        wsp_p = _pair_spatial(w_spatial[i])
        bsp_p = _bias_lanes(b_spatial[i])
        q, k, v, kf, vf, gated = _project(hp, row2(norm1_g[i]), w_in_b, ln_g, ln_b, wsp_p, bsp_p,
                                          rows=512, tail_rows=keep, pos_base=0, pos_period=512,
                                          class_major=True, emit_vn=False)
        attn_cm = _prompt_attention(q, k, v)
        attn = attn_cm.transpose(1, 0, 2).reshape(seq, ATTN_WIDTH)
        hp = _finish(hp, attn, gated, p_prompt[i].reshape(seq, PLE_DIM), *fin_w, rows=512, apply_final=last)
        nk_p.append(kf.reshape(nbp, keep, N_HEADS, HEAD_DIM))
        nv_p.append(vf.reshape(nbp, keep, N_HEADS, HEAD_DIM))

        seq_of_row = jnp.arange(n_s) // t_new
        same_seq = seq_of_row[:, None] == seq_of_row[None, :]
        corner = jnp.tile(w_spatial[i][:, :t_new, :t_new], (1, nbs, nbs))
        wsp_s = _pair_spatial(jnp.where(same_seq[None], corner, 0.0))
        bsp_s = _bias_lanes(jnp.tile(b_spatial[i][:, :t_new], (1, nbs)))
        q, k, v, kf, vf, gated, vn = _project(hs, row2(norm1_g[i]), w_in_b, ln_g, ln_b, wsp_s, bsp_s,
                                              rows=n_s, tail_rows=n_s, pos_base=PAST_LEN, pos_period=t_new,
                                              class_major=False, emit_vn=True)
        r3 = lambda a: a.reshape(nbs, t_new, ATTN_WIDTH)
        attn = _sample_attention(r3(q).astype(F32), r3(kf), r3(vf),
                                 cache_k[i].transpose(0, 2, 3, 1), cache_v[i].transpose(0, 2, 3, 1))
        hs = _finish(hs, attn.reshape(n_s, ATTN_WIDTH).astype(BF16), gated, p_sample[i].reshape(n_s, PLE_DIM),
                     *fin_w, rows=n_s, apply_final=last)
        nk_s.append(kf.reshape(nbs, t_new, N_HEADS, HEAD_DIM))
        nv_s.append(vf.reshape(nbs, t_new, N_HEADS, HEAD_DIM))
        nvc_s.append(vn.reshape(nbs, t_new, GMLP_WIDTH))

    return (hp.reshape(nbp, seq, D_MODEL), hs.reshape(nbs, t_new, D_MODEL),
            jnp.stack(nk_p), jnp.stack(nv_p), jnp.stack(nk_s), jnp.stack(nv_s), jnp.stack(nvc_s))
```

```python
import functools

import jax
import jax.numpy as jnp
from jax import lax
from jax.experimental import pallas as pl
from jax.experimental.pallas import tpu as pltpu

D_MODEL = 1024
N_HEADS = 8
HEAD_DIM = 64
ATTN_WIDTH = N_HEADS * HEAD_DIM
GMLP_GROUPS = 8
GMLP_WIDTH = 512
CHUNK = 128
DILATIONS = ((128, 1), (512, 4), (2048, 16))
WINDOW_MAX = 2048
PAST_LEN = 16384
ROT_DIM = HEAD_DIM // 4
ROPE_THETA = 500000.0
D_FF = 4 * D_MODEL
PLE_DIM = 256
EPS = 1e-6
NEG = -1e30

LANES = 128
VMEM_LIMIT_BYTES = 56 * 1024 * 1024

N_CLASS = 16
SPAN = 128
PAIRS = ATTN_WIDTH // LANES
ATTN_STEP_LANES = 2 * LANES
FF_CHUNK = 1024

BF16 = jnp.bfloat16
F32 = jnp.float32


def _rms(x, g):
    ms = jnp.mean(x * x, axis=-1, keepdims=True)
    return x * lax.rsqrt(ms + EPS) * g


def _const_spec(shape):
    nd = len(shape)
    return pl.BlockSpec(shape, lambda *_: (0,) * nd, pipeline_mode=pl.Buffered(1))


def _project_kernel(x_ref, g1_ref, w_ref, invf_ref, lng_ref, lnb_ref, wsp_ref, bsp_ref,
                    q_ref, k_ref, v_ref, kf_ref, vf_ref, gated_ref, *rest,
                    rows, tail_from, pos_base, pos_period, class_major, emit_vn):
    rest = list(rest)
    vn_ref = rest.pop(0) if emit_vn else None
    cos_off_ref, sin_off_ref = rest.pop(0), rest.pop(0)
    zs_ref = rest.pop(0) if class_major else None
    i = pl.program_id(0)

    @pl.when(i == 0)
    def _():
        off = (lax.broadcasted_iota(jnp.int32, (rows, LANES), 0) % pos_period).astype(F32)
        ang = off * invf_ref[...]
        cos_off_ref[...] = jnp.cos(ang)
        sin_off_ref[...] = jnp.sin(ang)

    tile_stride = rows if pos_period == rows else 0
    base = (i * tile_stride).astype(F32) + pos_base
    base_ang = base * invf_ref[...]
    cb, sb = jnp.cos(base_ang), jnp.sin(base_ang)
    co, so = cos_off_ref[...], sin_off_ref[...]
    cosf = cb * co - sb * so
    sint = sb * co + cb * so
    head_lane = lax.broadcasted_iota(jnp.int32, (1, LANES), 1) % HEAD_DIM
    sina = jnp.where(head_lane < ROT_DIM // 2, -sint, 0.0)
    sinb = jnp.where((head_lane >= ROT_DIM // 2) & (head_lane < ROT_DIM), sint, 0.0)

    def rope(z):
        return z * cosf + pltpu.roll(z, LANES - ROT_DIM // 2, 1) * sina + pltpu.roll(z, ROT_DIM // 2, 1) * sinb

    xn = _rms(x_ref[...], g1_ref[...]).astype(BF16)
    zq = jnp.dot(xn, w_ref[:, 0:ATTN_WIDTH], preferred_element_type=F32)
    zk = jnp.dot(xn, w_ref[:, ATTN_WIDTH:2 * ATTN_WIDTH], preferred_element_type=F32)
    zv = jnp.dot(xn, w_ref[:, 2 * ATTN_WIDTH:3 * ATTN_WIDTH], preferred_element_type=F32)
    slabs = [slice(s * LANES, (s + 1) * LANES) for s in range(PAIRS)]
    qr = [rope(zq[:, sl]) * (HEAD_DIM ** -0.5) for sl in slabs]
    kr = [rope(zk[:, sl]) for sl in slabs]

    @pl.when(i >= tail_from)
    def _():
        for s, sl in enumerate(slabs):
            kf_ref[:, sl] = kr[s]
        vf_ref[...] = zv

    if class_major:
        per_class = rows // N_CLASS
        for s, sl in enumerate(slabs):
            zs_ref[s] = qr[s]
            zs_ref[PAIRS + s] = kr[s]
            zs_ref[2 * PAIRS + s] = zv[:, sl]
        for r in range(N_CLASS):
            for s, sl in enumerate(slabs):
                pick = lambda n: zs_ref[n, pl.ds(r, per_class, stride=N_CLASS), :].astype(BF16)
                q_ref[r, :, sl] = pick(s)
                k_ref[r, :, sl] = pick(PAIRS + s)
                v_ref[r, :, sl] = pick(2 * PAIRS + s)
    else:
        for s, sl in enumerate(slabs):
            q_ref[:, sl] = qr[s].astype(BF16)
            k_ref[:, sl] = kr[s].astype(BF16)
        v_ref[...] = zv.astype(BF16)

    o_u = 3 * ATTN_WIDTH
    u = jax.nn.gelu(jnp.dot(xn, w_ref[:, o_u:o_u + GMLP_WIDTH], preferred_element_type=F32))
    vc = jax.nn.gelu(jnp.dot(xn, w_ref[:, o_u + GMLP_WIDTH:o_u + 2 * GMLP_WIDTH], preferred_element_type=F32))
    mu = jnp.mean(vc, axis=-1, keepdims=True)
    cen = vc - mu
    var = jnp.mean(cen * cen, axis=-1, keepdims=True)
    vn = cen * lax.rsqrt(var + EPS) * lng_ref[...] + lnb_ref[...]
    if emit_vn:
        vn_ref[...] = vn
    vnb = vn.astype(BF16)

    row = lax.broadcasted_iota(jnp.int32, (CHUNK, 2 * CHUNK), 0)
    col = lax.broadcasted_iota(jnp.int32, (CHUNK, 2 * CHUNK), 1)
    tril = (col % CHUNK) <= row
    lane = lax.broadcasted_iota(jnp.int32, (CHUNK, LANES), 1)
    first = lane < HEAD_DIM
    zero = jnp.zeros((CHUNK, LANES), BF16)
    for s in range(GMLP_WIDTH // LANES):
        sl = slice(s * LANES, (s + 1) * LANES)
        wp = jnp.where(tril, wsp_ref[s], jnp.zeros((), BF16))
        bias = bsp_ref[:, sl]
        for c in range(rows // CHUNK):
            rs = slice(c * CHUNK, (c + 1) * CHUNK)
            vblk = vnb[rs, sl]
            rhs = jnp.concatenate([jnp.where(first, vblk, zero), jnp.where(first, zero, vblk)], axis=0)
            mixed = jnp.dot(wp, rhs, preferred_element_type=F32) + bias
            gated_ref[rs, sl] = (u[rs, sl] * mixed).astype(BF16)


def _project(x, g1, w_in_b, ln_g, ln_b, wsp, bsp, *, rows, tail_rows, pos_base, pos_period, class_major, emit_vn):
    n_rows = x.shape[0]
    grid = n_rows // rows
    tail_from = (n_rows - tail_rows) // rows
    row_spec = lambda width: pl.BlockSpec((rows, width), lambda i: (i, 0))
    tail_spec = pl.BlockSpec((rows, ATTN_WIDTH), lambda i: (jnp.maximum(i - tail_from, 0), 0))
    if class_major:
        qkv_shape = jax.ShapeDtypeStruct((N_CLASS, n_rows // N_CLASS, ATTN_WIDTH), BF16)
        qkv_spec = pl.BlockSpec((N_CLASS, rows // N_CLASS, ATTN_WIDTH), lambda i: (0, i, 0))
    else:
        qkv_shape = jax.ShapeDtypeStruct((n_rows, ATTN_WIDTH), BF16)
        qkv_spec = row_spec(ATTN_WIDTH)
    out_shape = [
        qkv_shape,
        qkv_shape,
        qkv_shape,
        jax.ShapeDtypeStruct((tail_rows, ATTN_WIDTH), F32),
        jax.ShapeDtypeStruct((tail_rows, ATTN_WIDTH), F32),
        jax.ShapeDtypeStruct((n_rows, GMLP_WIDTH), BF16),
    ]
    out_specs = [qkv_spec] * 3 + [tail_spec] * 2 + [row_spec(GMLP_WIDTH)]
    if emit_vn:
        out_shape.append(jax.ShapeDtypeStruct((n_rows, GMLP_WIDTH), F32))
        out_specs.append(row_spec(GMLP_WIDTH))
    scratch = [pltpu.VMEM((rows, LANES), F32), pltpu.VMEM((rows, LANES), F32)]
    if class_major:
        scratch.append(pltpu.VMEM((3 * PAIRS, rows, LANES), F32))
    inv_freq = ROPE_THETA ** (-jnp.arange(0, ROT_DIM, 2, dtype=F32) / ROT_DIM)
    per_head = jnp.concatenate([inv_freq, inv_freq, jnp.zeros((HEAD_DIM - ROT_DIM,), F32)])
    invf = jnp.tile(per_head, LANES // HEAD_DIM).reshape(1, LANES)
    return pl.pallas_call(
        functools.partial(_project_kernel, rows=rows, tail_from=tail_from, pos_base=float(pos_base),
                          pos_period=pos_period, class_major=class_major, emit_vn=emit_vn),
        grid=(grid,),
        in_specs=[
            row_spec(D_MODEL),
            _const_spec((1, D_MODEL)),
            _const_spec(w_in_b.shape),
            _const_spec((1, LANES)),
            _const_spec((1, GMLP_WIDTH)),
            _const_spec((1, GMLP_WIDTH)),
            _const_spec(wsp.shape),
            _const_spec(bsp.shape),
        ],
        out_specs=out_specs,
        out_shape=out_shape,
        scratch_shapes=scratch,
        compiler_params=pltpu.CompilerParams(
            dimension_semantics=("arbitrary",), vmem_limit_bytes=VMEM_LIMIT_BYTES),
        name="project",
    )(x, g1, w_in_b, invf, ln_g, ln_b, wsp, bsp)


def _prompt_attn_kernel(q_ref, k_ref, v_ref, o_ref, kc_ref, vc_ref, acc_ref, m_ref, l_ref,
                        s_ref, p_ref, ms_ref, ls_ref):
    i = pl.program_id(1)
    rows = q_ref.shape[1]
    width = q_ref.shape[2]
    npair = width // LANES

    @pl.when((pl.program_id(0) == 0) & (i == 0))
    def _():
        p_ref[1] = jnp.zeros(p_ref.shape[1:], BF16)
        ms_ref[1] = jnp.zeros(ms_ref.shape[1:], F32)
        ls_ref[1] = jnp.zeros(ls_ref.shape[1:], F32)

    @pl.when(i == 0)
    def _():
        kc_ref[:, 0:rows, :] = jnp.zeros((N_CLASS, rows, width), BF16)
        vc_ref[:, 0:rows, :] = jnp.zeros((N_CLASS, rows, width), BF16)

    @pl.when(i > 0)
    def _():
        kc_ref[:, 0:rows, :] = kc_ref[:, rows:2 * rows, :]
        vc_ref[:, 0:rows, :] = vc_ref[:, rows:2 * rows, :]

    kc_ref[:, rows:2 * rows, :] = k_ref[...]
    vc_ref[:, rows:2 * rows, :] = v_ref[...]
    has_prev = i > 0

    def lanes(pr):
        return slice(pr * LANES, (pr + 1) * LANES)

    def run(n_units, mq, nk, q_of, k_of, v_of, mask_of, state_of, put, first, last):
        klane = lax.broadcasted_iota(jnp.int32, (nk, LANES), 1) < HEAD_DIM
        olane = lax.broadcasted_iota(jnp.int32, (mq, LANES), 1) < HEAD_DIM
        kzero = jnp.zeros((nk, LANES), BF16)
        nt = (((1,), (1,)), ((), ()))

        def scores(u, slot):
            for pr in range(npair):
                q, k = q_of(u, pr), k_of(u, pr)
                for hh in range(2):
                    km = jnp.where(klane if hh == 0 else jnp.logical_not(klane), k, kzero)
                    s_ref[slot, 2 * pr + hh, 0:mq, 0:nk] = lax.dot_general(q, km, nt, preferred_element_type=F32)

        def softmax(u, slot):
            mask = mask_of(u)
            for pr in range(npair):
                stats = []
                for hh in range(2):
                    s = jnp.where(mask, s_ref[slot, 2 * pr + hh, 0:mq, 0:nk], NEG)
                    m = jnp.max(s, axis=-1, keepdims=True)
                    p = jnp.exp(s - m)
                    p_ref[slot, 2 * pr + hh, 0:mq, 0:nk] = p.astype(BF16)
                    stats.append((m, jnp.sum(p, axis=-1, keepdims=True)))
                ms_ref[slot, pr, 0:mq, :] = jnp.where(olane, stats[0][0], stats[1][0])
                ls_ref[slot, pr, 0:mq, :] = jnp.where(olane, stats[0][1], stats[1][1])

        def values(u, slot, valid):
            for pr in range(npair):
                v = v_of(u, pr)
                pv = jnp.where(olane,
                               jnp.dot(p_ref[slot, 2 * pr, 0:mq, 0:nk], v, preferred_element_type=F32),
                               jnp.dot(p_ref[slot, 2 * pr + 1, 0:mq, 0:nk], v, preferred_element_type=F32))
                m = ms_ref[slot, pr, 0:mq, :]
                l = ls_ref[slot, pr, 0:mq, :]
                if not first:
                    acc0, m0, l0 = state_of(u, pr)
                    m_new = jnp.maximum(m0, m)
                    a = jnp.exp(m0 - m_new)
                    b = jnp.exp(m - m_new)
                    pv = acc0 * a + pv * b
                    l = l0 * a + l * b
                    m = m_new
                    if valid is not None and not last:
                        pv, m, l = jnp.where(valid, pv, acc0), jnp.where(valid, m, m0), jnp.where(valid, l, l0)
                put(u, pr, pv / l if last else (pv, m, l))

        scores(jnp.int32(0), 0)

        def trip(u, carry):
            slot = u % 2
            values(jnp.maximum(u - 1, 0), 1 - slot, u > 0)
            softmax(u, slot)
            scores(jnp.minimum(u + 1, n_units - 1), 1 - slot)
            return carry

        lax.fori_loop(0, n_units, trip, 0)
        values(jnp.int32(n_units - 1), (n_units - 1) % 2, None)

    def put_state(slabs_of, r0_of, n):
        def put(u, pr, res):
            for a, sb in enumerate(slabs_of(u)):
                rs = slice(a * n, (a + 1) * n)
                acc_ref[sb, pl.ds(r0_of(u), n), lanes(pr)] = res[0][rs]
                m_ref[sb, pl.ds(r0_of(u), n), lanes(pr)] = res[1][rs]
                l_ref[sb, pl.ds(r0_of(u), n), lanes(pr)] = res[2][rs]
        return put

    def gather(ref, slabs, r0, n, pr):
        return jnp.concatenate([ref[sb, pl.ds(r0, n), lanes(pr)] for sb in slabs], axis=0)

    qi = lax.broadcasted_iota(jnp.int32, (rows, 2 * rows), 0)
    kj = lax.broadcasted_iota(jnp.int32, (rows, 2 * rows), 1)
    diff = qi + rows - kj
    mask16 = (diff >= 0) & (diff <= SPAN) & ((kj >= rows) | has_prev)
    run(N_CLASS, rows, 2 * rows,
        q_of=lambda u, pr: q_ref[u, :, lanes(pr)],
        k_of=lambda u, pr: kc_ref[u, :, lanes(pr)],
        v_of=lambda u, pr: vc_ref[u, :, lanes(pr)],
        mask_of=lambda u: mask16, state_of=None,
        put=put_state(lambda u: [u], lambda u: 0, rows), first=True, last=False)

    n4 = N_CLASS // 4
    qb = rows // n4
    qa = lax.broadcasted_iota(jnp.int32, (n4 * qb, n4 * 2 * qb), 0)
    ka = lax.broadcasted_iota(jnp.int32, (n4 * qb, n4 * 2 * qb), 1)
    d4 = 4 * (qa % qb - ka % (2 * qb) + qb) + (qa // qb - ka // (2 * qb))
    band4 = (d4 >= 0) & (d4 <= SPAN)
    slabs4 = lambda u: [u // n4 + 4 * a for a in range(n4)]
    q0_4 = lambda u: pl.multiple_of((u % n4) * qb, qb)
    k0_4 = lambda u: pl.multiple_of(rows - qb + (u % n4) * qb, qb)
    run(N_CLASS, n4 * qb, n4 * 2 * qb,
        q_of=lambda u, pr: gather(q_ref, slabs4(u), q0_4(u), qb, pr),
        k_of=lambda u, pr: gather(kc_ref, slabs4(u), k0_4(u), 2 * qb, pr),
        v_of=lambda u, pr: gather(vc_ref, slabs4(u), k0_4(u), 2 * qb, pr),
        mask_of=lambda u: band4 & ((ka % (2 * qb) + (u % n4) * qb >= qb) | has_prev),
        state_of=lambda u, pr: tuple(gather(ref, slabs4(u), q0_4(u), qb, pr) for ref in (acc_ref, m_ref, l_ref)),
        put=put_state(slabs4, q0_4, qb), first=False, last=False)

    q1 = 2 * rows // N_CLASS
    qa1 = lax.broadcasted_iota(jnp.int32, (N_CLASS * q1, N_CLASS * 2 * q1), 0)
    ka1 = lax.broadcasted_iota(jnp.int32, (N_CLASS * q1, N_CLASS * 2 * q1), 1)
    d1 = N_CLASS * (qa1 % q1 - ka1 % (2 * q1) + q1) + (qa1 // q1 - ka1 // (2 * q1))
    band1 = (d1 >= 0) & (d1 <= SPAN)
    every = list(range(N_CLASS))
    q0_1 = lambda u: pl.multiple_of(u * q1, q1)
    k0_1 = lambda u: pl.multiple_of(rows - q1 + u * q1, q1)

    def put_out(u, pr, res):
        for sb in every:
            o_ref[sb, pl.ds(q0_1(u), q1), lanes(pr)] = res[sb * q1:(sb + 1) * q1].astype(BF16)

    run(rows // q1, N_CLASS * q1, N_CLASS * 2 * q1,
        q_of=lambda u, pr: gather(q_ref, every, q0_1(u), q1, pr),
        k_of=lambda u, pr: gather(kc_ref, every, k0_1(u), 2 * q1, pr),
        v_of=lambda u, pr: gather(vc_ref, every, k0_1(u), 2 * q1, pr),
        mask_of=lambda u: band1 & ((ka1 % (2 * q1) + u * q1 >= q1) | has_prev),
        state_of=lambda u, pr: tuple(gather(ref, every, q0_1(u), q1, pr) for ref in (acc_ref, m_ref, l_ref)),
        put=put_out, first=False, last=True)


def _prompt_attention(q_cm, k_cm, v_cm):
    n_slab_rows = q_cm.shape[1]
    npair = ATTN_STEP_LANES // LANES
    spec = pl.BlockSpec((N_CLASS, SPAN, ATTN_STEP_LANES), lambda h, i: (0, i, h))
    mq_max, nk_max = 2 * SPAN, 4 * SPAN
    return pl.pallas_call(
        _prompt_attn_kernel,
        grid=(ATTN_WIDTH // ATTN_STEP_LANES, n_slab_rows // SPAN),
        in_specs=[spec, spec, spec],
        out_specs=spec,
        out_shape=jax.ShapeDtypeStruct(q_cm.shape, BF16),
        scratch_shapes=[
            pltpu.VMEM((N_CLASS, 2 * SPAN, ATTN_STEP_LANES), BF16),
            pltpu.VMEM((N_CLASS, 2 * SPAN, ATTN_STEP_LANES), BF16),
            pltpu.VMEM((N_CLASS, SPAN, ATTN_STEP_LANES), F32),
            pltpu.VMEM((N_CLASS, SPAN, ATTN_STEP_LANES), F32),
            pltpu.VMEM((N_CLASS, SPAN, ATTN_STEP_LANES), F32),
            pltpu.VMEM((2, 2 * npair, mq_max, nk_max), F32),
            pltpu.VMEM((2, 2 * npair, mq_max, nk_max), BF16),
            pltpu.VMEM((2, npair, mq_max, LANES), F32),
            pltpu.VMEM((2, npair, mq_max, LANES), F32),
        ],
        compiler_params=pltpu.CompilerParams(
            dimension_semantics=("arbitrary", "arbitrary"), vmem_limit_bytes=VMEM_LIMIT_BYTES),
        name="prompt_attention",
    )(q_cm, k_cm, v_cm)


def _sample_attn_kernel(q_ref, kn_ref, vn_ref, kt_ref, vt_ref, o_ref, *, t_new, lbuf):
    width = ATTN_WIDTH
    nrow = t_new * N_HEADS
    q = q_ref[...]
    kn = kn_ref[...]
    vnew = vn_ref[...]

    row = lax.broadcasted_iota(jnp.int32, (nrow, width), 0)
    lane = lax.broadcasted_iota(jnp.int32, (nrow, width), 1)
    hmask = (lane // HEAD_DIM) == (row % N_HEADS)
    qtok = jnp.concatenate([jnp.broadcast_to(q[j:j + 1, :], (N_HEADS, width)) for j in range(t_new)], axis=0)
    qrows = jnp.where(hmask, qtok, 0.0)

    kt = kt_ref[...].reshape(width, lbuf).astype(BF16)
    vt = vt_ref[...].reshape(width, lbuf).astype(BF16)
    s = jnp.dot(qrows.astype(BF16), kt, preferred_element_type=F32)

    def reach_count(delta):
        cnt = jnp.zeros(delta.shape, F32)
        for window, dil in DILATIONS:
            hit = (delta >= 0) & (delta % dil == 0) & (delta <= window)
            cnt = cnt + jnp.where(hit, 1.0, 0.0)
        return cnt

    pos = lax.broadcasted_iota(jnp.int32, (nrow, lbuf), 1)
    tok = lax.broadcasted_iota(jnp.int32, (nrow, lbuf), 0) // N_HEADS
    cnt = reach_count(lbuf + tok - pos)
    tok1 = tok[:, 0:1]
    cnew = [reach_count(tok1 - j) for j in range(t_new)]
    snew = [jnp.sum(qrows * kn[j:j + 1, :], axis=-1, keepdims=True) for j in range(t_new)]

    s = jnp.where(cnt > 0.0, s, NEG)
    m = jnp.max(s, axis=-1, keepdims=True)
    for j in range(t_new):
        m = jnp.maximum(m, jnp.where(cnew[j] > 0.0, snew[j], NEG))
    e = jnp.exp(s - m) * cnt
    l = jnp.sum(e, axis=-1, keepdims=True)
    o = lax.dot_general(e.astype(BF16), vt, (((1,), (1,)), ((), ())), preferred_element_type=F32)
    for j in range(t_new):
        ej = jnp.where(cnew[j] > 0.0, jnp.exp(snew[j] - m), 0.0) * cnew[j]
        l = l + ej
        o = o + ej * vnew[j:j + 1, :]
    o = jnp.where(hmask, o / l, 0.0)
    for j in range(t_new):
        o_ref[j:j + 1, :] = jnp.sum(o[j * N_HEADS:(j + 1) * N_HEADS, :], axis=0, keepdims=True)


def _sample_attention(q, k_new, v_new, cache_kt, cache_vt):
    nb, t_new, width = q.shape
    lbuf = cache_kt.shape[-1]
    pad = lambda a: jnp.pad(a, ((0, 0), (0, 8 - t_new), (0, 0)))
    new_spec = pl.BlockSpec((None, 8, width), lambda b: (b, 0, 0))
    cache_spec = pl.BlockSpec((None, N_HEADS, HEAD_DIM, lbuf), lambda b: (b, 0, 0, 0))
    return pl.pallas_call(
        functools.partial(_sample_attn_kernel, t_new=t_new, lbuf=lbuf),
        grid=(nb,),
        in_specs=[new_spec, new_spec, new_spec, cache_spec, cache_spec],
        out_specs=pl.BlockSpec((None, t_new, width), lambda b: (b, 0, 0)),
        out_shape=jax.ShapeDtypeStruct((nb, t_new, width), F32),
        compiler_params=pltpu.CompilerParams(
            dimension_semantics=("arbitrary",), vmem_limit_bytes=VMEM_LIMIT_BYTES),
        name="sample_attention",
    )(pad(q), pad(k_new), pad(v_new), cache_kt, cache_vt)


def _finish_kernel(x_ref, attn_ref, gated_ref, p_ref, wo_ref, g2_ref, wup_ref, wdn_ref,
                   gg_ref, wg_ref, wple_ref, gf_ref, y_ref, *, apply_final):
    mix = jnp.concatenate([attn_ref[...], gated_ref[...]], axis=-1)
    h = x_ref[...] + jnp.dot(mix, wo_ref[...], preferred_element_type=F32)
    n2 = _rms(h, g2_ref[...]).astype(BF16)
    f = jnp.zeros_like(h)
    for c in range(D_FF // FF_CHUNK):
        cs = slice(c * FF_CHUNK, (c + 1) * FF_CHUNK)
        up = jnp.dot(n2, wup_ref[:, cs], preferred_element_type=F32)
        act = jnp.square(jnp.maximum(up, 0.0)).astype(BF16)
        f = f + jnp.dot(act, wdn_ref[cs, :], preferred_element_type=F32)
    h = h + f
    gate = jax.nn.sigmoid(jnp.dot(_rms(h, gg_ref[...]).astype(BF16), wg_ref[...], preferred_element_type=F32))
    ple = jnp.dot(p_ref[...].astype(BF16), wple_ref[...], preferred_element_type=F32)
    h = h + gate * ple
    y_ref[...] = _rms(h, gf_ref[...]) if apply_final else h


def _finish(x, attn, gated, p, wo, g2, wup, wdn, gg, wg, wple, gf, *, rows, apply_final):
    n_rows = x.shape[0]
    row_spec = lambda width: pl.BlockSpec((rows, width), lambda i: (i, 0))
    return pl.pallas_call(
        functools.partial(_finish_kernel, apply_final=apply_final),
        grid=(n_rows // rows,),
        in_specs=[
            row_spec(D_MODEL), row_spec(ATTN_WIDTH), row_spec(GMLP_WIDTH), row_spec(PLE_DIM),
            _const_spec(wo.shape), _const_spec((1, D_MODEL)), _const_spec(wup.shape), _const_spec(wdn.shape),
            _const_spec((1, D_MODEL)), _const_spec(wg.shape), _const_spec(wple.shape), _const_spec((1, D_MODEL)),
        ],
        out_specs=row_spec(D_MODEL),
        out_shape=jax.ShapeDtypeStruct((n_rows, D_MODEL), F32),
        compiler_params=pltpu.CompilerParams(
            dimension_semantics=("arbitrary",), vmem_limit_bytes=VMEM_LIMIT_BYTES),
        name="finish",
    )(x, attn, gated, p, wo, g2, wup, wdn, gg, wg, wple, gf)


def _pair_spatial(w):
    g, l, _ = w.shape
    return w.reshape(g // 2, 2, l, l).transpose(0, 2, 1, 3).reshape(g // 2, l, 2 * l).astype(BF16)


def _bias_lanes(b):
    return jnp.repeat(b.T, GMLP_WIDTH // GMLP_GROUPS, axis=1)


def kernel(x_prompt, x_sample, cache_k, cache_v, p_prompt, p_sample, norm1_g, w_in, ln_v_g, ln_v_b,
           w_spatial, b_spatial, w_out, norm2_g, w_up, w_down, gate_norm_g, w_gate, w_ple, final_g):
    depth = w_in.shape[0]
    nbp, seq, _ = x_prompt.shape
    nbs, t_new, _ = x_sample.shape
    lbuf = cache_k.shape[2]
    keep = min(WINDOW_MAX, seq)
    assert nbp == 1 and seq % (N_CLASS * SPAN) == 0 and (nbs * t_new) % CHUNK == 0 and t_new <= CHUNK
    n_s = nbs * t_new

    row2 = lambda a: a.reshape(1, -1)

    hp = x_prompt.reshape(seq, D_MODEL)
    hs = x_sample.reshape(n_s, D_MODEL)
    nk_p, nv_p, nk_s, nv_s, nvc_s = [], [], [], [], []
    for i in range(depth):
        last = i == depth - 1
        w_in_b = w_in[i].astype(BF16)
        wo_b, wup_b, wdn_b = w_out[i].astype(BF16), w_up[i].astype(BF16), w_down[i].astype(BF16)
        wg_b, wple_b = w_gate[i].astype(BF16), w_ple[i].astype(BF16)
        fin_w = (wo_b, row2(norm2_g[i]), wup_b, wdn_b, row2(gate_norm_g[i]), wg_b, wple_b, row2(final_g))
        ln_g, ln_b = row2(ln_v_g[i]), row2(ln_v_b[i])

        wsp_p = _pair_spatial(w_spatial[i])
        bsp_p = _bias_lanes(b_spatial[i])
        q, k, v, kf, vf, gated = _project(hp, row2(norm1_g[i]), w_in_b, ln_g, ln_b, wsp_p, bsp_p,
                                          rows=512, tail_rows=keep, pos_base=0, pos_period=512,
                                          class_major=True, emit_vn=False)
        attn_cm = _prompt_attention(q, k, v)
        attn = attn_cm.transpose(1, 0, 2).reshape(seq, ATTN_WIDTH)
        hp = _finish(hp, attn, gated, p_prompt[i].reshape(seq, PLE_DIM), *fin_w, rows=512, apply_final=last)
        nk_p.append(kf.reshape(nbp, keep, N_HEADS, HEAD_DIM))
        nv_p.append(vf.reshape(nbp, keep, N_HEADS, HEAD_DIM))

        seq_of_row = jnp.arange(n_s) // t_new
        same_seq = seq_of_row[:, None] == seq_of_row[None, :]
        corner = jnp.tile(w_spatial[i][:, :t_new, :t_new], (1, nbs, nbs))
        wsp_s = _pair_spatial(jnp.where(same_seq[None], corner, 0.0))
        bsp_s = _bias_lanes(jnp.tile(b_spatial[i][:, :t_new], (1, nbs)))
        q, k, v, kf, vf, gated, vn = _project(hs, row2(norm1_g[i]), w_in_b, ln_g, ln_b, wsp_s, bsp_s,
                                              rows=n_s, tail_rows=n_s, pos_base=PAST_LEN, pos_period=t_new,
                                              class_major=False, emit_vn=True)
        r3 = lambda a: a.reshape(nbs, t_new, ATTN_WIDTH)
        attn = _sample_attention(r3(q).astype(F32), r3(kf), r3(vf),
                                 cache_k[i].transpose(0, 2, 3, 1), cache_v[i].transpose(0, 2, 3, 1))
        hs = _finish(hs, attn.reshape(n_s, ATTN_WIDTH).astype(BF16), gated, p_sample[i].reshape(n_s, PLE_DIM),
                     *fin_w, rows=n_s, apply_final=last)
        nk_s.append(kf.reshape(nbs, t_new, N_HEADS, HEAD_DIM))
        nv_s.append(vf.reshape(nbs, t_new, N_HEADS, HEAD_DIM))
        nvc_s.append(vn.reshape(nbs, t_new, GMLP_WIDTH))

    return (hp.reshape(nbp, seq, D_MODEL), hs.reshape(nbs, t_new, D_MODEL),
            jnp.stack(nk_p), jnp.stack(nv_p), jnp.stack(nk_s), jnp.stack(nv_s), jnp.stack(nvc_s))
```

```python
import functools

import jax
import jax.numpy as jnp
from jax import lax
from jax.experimental import pallas as pl
from jax.experimental.pallas import tpu as pltpu

D_MODEL = 1024
N_HEADS = 8
HEAD_DIM = 64
ATTN_WIDTH = N_HEADS * HEAD_DIM
GMLP_GROUPS = 8
GMLP_WIDTH = 512
CHUNK = 128
DILATIONS = ((128, 1), (512, 4), (2048, 16))
WINDOW_MAX = 2048
PAST_LEN = 16384
ROT_DIM = HEAD_DIM // 4
ROPE_THETA = 500000.0
D_FF = 4 * D_MODEL
PLE_DIM = 256
EPS = 1e-6
NEG = -1e30

LANES = 128
VMEM_LIMIT_BYTES = 56 * 1024 * 1024

N_CLASS = 16
SPAN = 128
PAIRS = ATTN_WIDTH // LANES
ATTN_STEP_LANES = 2 * LANES
ATTN_GROUP = 2
SUBLANES = 8
PACKED_ROWS = 16
FF_CHUNK = 1024

BF16 = jnp.bfloat16
F32 = jnp.float32


def _rms(x, g):
    ms = jnp.mean(x * x, axis=-1, keepdims=True)
    return x * lax.rsqrt(ms + EPS) * g


def _const_spec(shape):
    nd = len(shape)
    return pl.BlockSpec(shape, lambda *_: (0,) * nd, pipeline_mode=pl.Buffered(1))


def _project_kernel(x_ref, g1_ref, w_ref, invf_ref, lng_ref, lnb_ref, wsp_ref, bsp_ref,
                    q_ref, k_ref, v_ref, kf_ref, vf_ref, gated_ref, *rest,
                    rows, tail_from, pos_base, pos_period, class_major, emit_vn):
    rest = list(rest)
    vn_ref = rest.pop(0) if emit_vn else None
    cos_off_ref, sin_off_ref = rest.pop(0), rest.pop(0)
    zs_ref = rest.pop(0) if class_major else None
    i = pl.program_id(0)

    @pl.when(i == 0)
    def _():
        off = (lax.broadcasted_iota(jnp.int32, (rows, LANES), 0) % pos_period).astype(F32)
        ang = off * invf_ref[...]
        cos_off_ref[...] = jnp.cos(ang)
        sin_off_ref[...] = jnp.sin(ang)

    tile_stride = rows if pos_period == rows else 0
    base = (i * tile_stride).astype(F32) + pos_base
    base_ang = base * invf_ref[...]
    cb, sb = jnp.cos(base_ang), jnp.sin(base_ang)
    co, so = cos_off_ref[...], sin_off_ref[...]
    cosf = cb * co - sb * so
    sint = sb * co + cb * so
    head_lane = lax.broadcasted_iota(jnp.int32, (1, LANES), 1) % HEAD_DIM
    sina = jnp.where(head_lane < ROT_DIM // 2, -sint, 0.0)
    sinb = jnp.where((head_lane >= ROT_DIM // 2) & (head_lane < ROT_DIM), sint, 0.0)

    def rope(z):
        return z * cosf + pltpu.roll(z, LANES - ROT_DIM // 2, 1) * sina + pltpu.roll(z, ROT_DIM // 2, 1) * sinb

    xn = _rms(x_ref[...], g1_ref[...]).astype(BF16)
    zq = jnp.dot(xn, w_ref[:, 0:ATTN_WIDTH], preferred_element_type=F32)
    zk = jnp.dot(xn, w_ref[:, ATTN_WIDTH:2 * ATTN_WIDTH], preferred_element_type=F32)
    zv = jnp.dot(xn, w_ref[:, 2 * ATTN_WIDTH:3 * ATTN_WIDTH], preferred_element_type=F32)
    slabs = [slice(s * LANES, (s + 1) * LANES) for s in range(PAIRS)]
    qr = [rope(zq[:, sl]) * (HEAD_DIM ** -0.5) for sl in slabs]
    kr = [rope(zk[:, sl]) for sl in slabs]

    @pl.when(i >= tail_from)
    def _():
        for s, sl in enumerate(slabs):
            kf_ref[:, sl] = kr[s]
        vf_ref[...] = zv

    if class_major:
        per_class = rows // N_CLASS
        for s, sl in enumerate(slabs):
            zs_ref[s] = qr[s]
            zs_ref[PAIRS + s] = kr[s]
            zs_ref[2 * PAIRS + s] = zv[:, sl]
        for r in range(N_CLASS):
            for s, sl in enumerate(slabs):
                pick = lambda n: zs_ref[n, pl.ds(r, per_class, stride=N_CLASS), :].astype(BF16)
                q_ref[r, :, sl] = pick(s)
                k_ref[r, :, sl] = pick(PAIRS + s)
                v_ref[r, :, sl] = pick(2 * PAIRS + s)
    else:
        for s, sl in enumerate(slabs):
            q_ref[:, sl] = qr[s].astype(BF16)
            k_ref[:, sl] = kr[s].astype(BF16)
        v_ref[...] = zv.astype(BF16)

    o_u = 3 * ATTN_WIDTH
    u = jax.nn.gelu(jnp.dot(xn, w_ref[:, o_u:o_u + GMLP_WIDTH], preferred_element_type=F32))
    vc = jax.nn.gelu(jnp.dot(xn, w_ref[:, o_u + GMLP_WIDTH:o_u + 2 * GMLP_WIDTH], preferred_element_type=F32))
    mu = jnp.mean(vc, axis=-1, keepdims=True)
    cen = vc - mu
    var = jnp.mean(cen * cen, axis=-1, keepdims=True)
    vn = cen * lax.rsqrt(var + EPS) * lng_ref[...] + lnb_ref[...]
    if emit_vn:
        vn_ref[...] = vn
    vnb = vn.astype(BF16)

    row = lax.broadcasted_iota(jnp.int32, (CHUNK, 2 * CHUNK), 0)
    col = lax.broadcasted_iota(jnp.int32, (CHUNK, 2 * CHUNK), 1)
    tril = (col % CHUNK) <= row
    lane = lax.broadcasted_iota(jnp.int32, (CHUNK, LANES), 1)
    first = lane < HEAD_DIM
    zero = jnp.zeros((CHUNK, LANES), BF16)
    for s in range(GMLP_WIDTH // LANES):
        sl = slice(s * LANES, (s + 1) * LANES)
        wp = jnp.where(tril, wsp_ref[s], jnp.zeros((), BF16))
        bias = bsp_ref[:, sl]
        for c in range(rows // CHUNK):
            rs = slice(c * CHUNK, (c + 1) * CHUNK)
            vblk = vnb[rs, sl]
            rhs = jnp.concatenate([jnp.where(first, vblk, zero), jnp.where(first, zero, vblk)], axis=0)
            mixed = jnp.dot(wp, rhs, preferred_element_type=F32) + bias
            gated_ref[rs, sl] = (u[rs, sl] * mixed).astype(BF16)


def _project(x, g1, w_in_b, ln_g, ln_b, wsp, bsp, *, rows, tail_rows, pos_base, pos_period, class_major, emit_vn):
    n_rows = x.shape[0]
    grid = n_rows // rows
    tail_from = (n_rows - tail_rows) // rows
    row_spec = lambda width: pl.BlockSpec((rows, width), lambda i: (i, 0))
    tail_spec = pl.BlockSpec((rows, ATTN_WIDTH), lambda i: (jnp.maximum(i - tail_from, 0), 0))
    if class_major:
        qkv_shape = jax.ShapeDtypeStruct((N_CLASS, n_rows // N_CLASS, ATTN_WIDTH), BF16)
        qkv_spec = pl.BlockSpec((N_CLASS, rows // N_CLASS, ATTN_WIDTH), lambda i: (0, i, 0))
    else:
        qkv_shape = jax.ShapeDtypeStruct((n_rows, ATTN_WIDTH), BF16)
        qkv_spec = row_spec(ATTN_WIDTH)
    out_shape = [
        qkv_shape,
        qkv_shape,
        qkv_shape,
        jax.ShapeDtypeStruct((tail_rows, ATTN_WIDTH), F32),
        jax.ShapeDtypeStruct((tail_rows, ATTN_WIDTH), F32),
        jax.ShapeDtypeStruct((n_rows, GMLP_WIDTH), BF16),
    ]
    out_specs = [qkv_spec] * 3 + [tail_spec] * 2 + [row_spec(GMLP_WIDTH)]
    if emit_vn:
        out_shape.append(jax.ShapeDtypeStruct((n_rows, GMLP_WIDTH), F32))
        out_specs.append(row_spec(GMLP_WIDTH))
    scratch = [pltpu.VMEM((rows, LANES), F32), pltpu.VMEM((rows, LANES), F32)]
    if class_major:
        scratch.append(pltpu.VMEM((3 * PAIRS, rows, LANES), F32))
    inv_freq = ROPE_THETA ** (-jnp.arange(0, ROT_DIM, 2, dtype=F32) / ROT_DIM)
    per_head = jnp.concatenate([inv_freq, inv_freq, jnp.zeros((HEAD_DIM - ROT_DIM,), F32)])
    invf = jnp.tile(per_head, LANES // HEAD_DIM).reshape(1, LANES)
    return pl.pallas_call(
        functools.partial(_project_kernel, rows=rows, tail_from=tail_from, pos_base=float(pos_base),
                          pos_period=pos_period, class_major=class_major, emit_vn=emit_vn),
        grid=(grid,),
        in_specs=[
            row_spec(D_MODEL),
            _const_spec((1, D_MODEL)),
            _const_spec(w_in_b.shape),
            _const_spec((1, LANES)),
            _const_spec((1, GMLP_WIDTH)),
            _const_spec((1, GMLP_WIDTH)),
            _const_spec(wsp.shape),
            _const_spec(bsp.shape),
        ],
        out_specs=out_specs,
        out_shape=out_shape,
        scratch_shapes=scratch,
        compiler_params=pltpu.CompilerParams(
            dimension_semantics=("arbitrary",), vmem_limit_bytes=VMEM_LIMIT_BYTES),
        name="project",
    )(x, g1, w_in_b, invf, ln_g, ln_b, wsp, bsp)


def _prompt_attn_kernel(q_ref, k_ref, v_ref, o_ref, kc_ref, vc_ref, q32_ref, acc_ref, m_ref, l_ref,
                        s_ref, p_ref, ms_ref):
    i = pl.program_id(1)
    rows = q_ref.shape[1]
    width = q_ref.shape[2]
    npair = width // LANES
    half = SUBLANES

    @pl.when((pl.program_id(0) == 0) & (i == 0))
    def _():
        p_ref[1] = jnp.zeros(p_ref.shape[1:], BF16)
        ms_ref[1] = jnp.zeros(ms_ref.shape[1:], F32)

    @pl.when(i == 0)
    def _():
        kc_ref[0, :, 0:rows, :] = jnp.zeros((N_CLASS, rows, width), BF16)
        vc_ref[0, :, 0:rows, :] = jnp.zeros((N_CLASS, rows, width), BF16)

    @pl.when(i > 0)
    def _():
        kc_ref[0, :, 0:rows, :] = kc_ref[0, :, rows:2 * rows, :]
        vc_ref[0, :, 0:rows, :] = vc_ref[0, :, rows:2 * rows, :]

    kc_ref[0, :, rows:2 * rows, :] = k_ref[...]
    vc_ref[0, :, rows:2 * rows, :] = v_ref[...]
    lo = rows - PACKED_ROWS
    for ref in (kc_ref, vc_ref):
        tail = ref[0, :, lo:2 * rows, :].astype(F32)
        ref[1, :, lo:2 * rows - PACKED_ROWS, :] = tail[:, half:half + rows, :].astype(BF16)
    q32_ref[...] = q_ref[...].astype(F32)
    has_prev = i > 0

    def lanes(pr):
        return slice(pr * LANES, (pr + 1) * LANES)

    mq, nk = rows, 2 * rows
    olane = lax.broadcasted_iota(jnp.int32, (mq, LANES), 1) < HEAD_DIM
    qzero = jnp.zeros((mq, LANES), BF16)
    ones_v = jnp.ones((nk, LANES), BF16)
    nt = (((1,), (1,)), ((), ()))

    def run(n_units, q_of, k_of, v_of, mask_of, state_of, put, first, last):
        units = lambda t: [t * ATTN_GROUP + j for j in range(ATTN_GROUP)]

        def scores(t, slot):
            for j, u in enumerate(units(t)):
                for pr in range(npair):
                    q = q_of(u, pr)
                    qq = jnp.concatenate([jnp.where(olane, q, qzero), jnp.where(olane, qzero, q)], axis=0)
                    s_ref[slot, j, pr] = lax.dot_general(qq, k_of(u, pr), nt, preferred_element_type=F32)

        def softmax(t, slot):
            for j, u in enumerate(units(t)):
                mask = mask_of(u)
                for pr in range(npair):
                    ms = []
                    for hh in range(2):
                        hs = slice(hh * mq, (hh + 1) * mq)
                        s = jnp.where(mask, s_ref[slot, j, pr, hs, :], NEG)
                        m = jnp.max(s, axis=-1, keepdims=True)
                        p_ref[slot, j, pr, hs, :] = jnp.exp(s - m).astype(BF16)
                        ms.append(m)
                    ms_ref[slot, j, pr] = jnp.where(olane, ms[0], ms[1])

        def values(t, slot, valid):
            for j, u in enumerate(units(t)):
                for pr in range(npair):
                    vv = jnp.concatenate([v_of(u, pr), ones_v], axis=1)
                    r = jnp.dot(p_ref[slot, j, pr], vv, preferred_element_type=F32)
                    pv = jnp.where(olane, r[0:mq, 0:LANES], r[mq:2 * mq, 0:LANES])
                    l = jnp.where(olane, r[0:mq, LANES:2 * LANES], r[mq:2 * mq, LANES:2 * LANES])
                    m = ms_ref[slot, j, pr]
                    if not first:
                        acc0, m0, l0 = state_of(u, pr)
                        m_new = jnp.maximum(m0, m)
                        a = jnp.exp(m0 - m_new)
                        b = jnp.exp(m - m_new)
                        pv = acc0 * a + pv * b
                        l = l0 * a + l * b
                        m = m_new
                        if last:
                            pv = pv / l
                        if valid is not None:
                            pv, m, l = jnp.where(valid, pv, acc0), jnp.where(valid, m, m0), jnp.where(valid, l, l0)
                    put(u, pr, pv if last else (pv, m, l))

        n_trips = n_units // ATTN_GROUP
        scores(jnp.int32(0), 0)

        def trip(t, carry):
            slot = t % 2
            values(jnp.maximum(t - 1, 0), 1 - slot, t > 0)
            softmax(t, slot)
            scores(jnp.minimum(t + 1, n_trips - 1), 1 - slot)
            return carry

        lax.fori_loop(0, n_trips, trip, 0)
        values(jnp.int32(n_trips - 1), (n_trips - 1) % 2, None)

    def put_state(slabs_of, r0_of, n):
        def put(u, pr, res):
            for a, sb in enumerate(slabs_of(u)):
                rs = slice(a * n, (a + 1) * n)
                acc_ref[sb, pl.ds(r0_of(u), n), lanes(pr)] = res[0][rs]
                m_ref[sb, pl.ds(r0_of(u), n), lanes(pr)] = res[1][rs]
                l_ref[sb, pl.ds(r0_of(u), n), lanes(pr)] = res[2][rs]
        return put

    def gather(ref, slabs, r0, n, pr):
        return jnp.concatenate([ref[sb, pl.ds(r0, n), lanes(pr)] for sb in slabs], axis=0)

    state_refs = (acc_ref, m_ref, l_ref)
    kplain, vplain = kc_ref.at[0], vc_ref.at[0]

    qi = lax.broadcasted_iota(jnp.int32, (mq, nk), 0)
    kj = lax.broadcasted_iota(jnp.int32, (mq, nk), 1)
    diff = qi + rows - kj
    mask16 = (diff >= 0) & (diff <= SPAN) & ((kj >= rows) | has_prev)
    run(N_CLASS,
        q_of=lambda u, pr: q_ref[u, :, lanes(pr)],
        k_of=lambda u, pr: kplain[u, :, lanes(pr)],
        v_of=lambda u, pr: vplain[u, :, lanes(pr)],
        mask_of=lambda u: mask16, state_of=None,
        put=put_state(lambda u: [u], lambda u: 0, rows), first=True, last=False)

    n4 = N_CLASS // 4
    qb = rows // n4
    d4 = 4 * (qi % qb - kj % (2 * qb) + qb) + (qi // qb - kj // (2 * qb))
    band4 = (d4 >= 0) & (d4 <= SPAN)
    slabs4 = lambda u: [u // n4 + 4 * a for a in range(n4)]
    q0_4 = lambda u: pl.multiple_of((u % n4) * qb, qb)
    k0_4 = lambda u: pl.multiple_of(rows - qb + (u % n4) * qb, qb)
    run(N_CLASS,
        q_of=lambda u, pr: gather(q_ref, slabs4(u), q0_4(u), qb, pr),
        k_of=lambda u, pr: gather(kplain, slabs4(u), k0_4(u), 2 * qb, pr),
        v_of=lambda u, pr: gather(vplain, slabs4(u), k0_4(u), 2 * qb, pr),
        mask_of=lambda u: band4 & ((kj % (2 * qb) + (u % n4) * qb >= qb) | has_prev),
        state_of=lambda u, pr: tuple(gather(ref, slabs4(u), q0_4(u), qb, pr) for ref in state_refs),
        put=put_state(slabs4, q0_4, qb), first=False, last=False)

    d1 = N_CLASS * (qi % half - kj % (2 * half) + half) + (qi // half - kj // (2 * half))
    band1 = (d1 >= 0) & (d1 <= SPAN)
    every = list(range(N_CLASS))
    q0_1 = lambda u: pl.multiple_of(u * half, half)
    kcopy = lambda u: 1 - u % 2
    k0_1 = lambda u: pl.multiple_of(((rows - half + u * half) // PACKED_ROWS) * PACKED_ROWS, PACKED_ROWS)

    def put_out(u, pr, res):
        for sb in every:
            acc_ref[sb, pl.ds(q0_1(u), half), lanes(pr)] = res[sb * half:(sb + 1) * half]

    run(rows // half,
        q_of=lambda u, pr: gather(q32_ref, every, q0_1(u), half, pr).astype(BF16),
        k_of=lambda u, pr: gather(kc_ref.at[kcopy(u)], every, k0_1(u), 2 * half, pr),
        v_of=lambda u, pr: gather(vc_ref.at[kcopy(u)], every, k0_1(u), 2 * half, pr),
        mask_of=lambda u: band1 & ((kj % (2 * half) + u * half >= half) | has_prev),
        state_of=lambda u, pr: tuple(gather(ref, every, q0_1(u), half, pr) for ref in state_refs),
        put=put_out, first=False, last=True)
    o_ref[...] = acc_ref[...].astype(BF16)


def _prompt_attention(q_cm, k_cm, v_cm):
    n_slab_rows = q_cm.shape[1]
    npair = ATTN_STEP_LANES // LANES
    spec = pl.BlockSpec((N_CLASS, SPAN, ATTN_STEP_LANES), lambda h, i: (0, i, h))
    return pl.pallas_call(
        _prompt_attn_kernel,
        grid=(ATTN_WIDTH // ATTN_STEP_LANES, n_slab_rows // SPAN),
        in_specs=[spec, spec, spec],
        out_specs=spec,
        out_shape=jax.ShapeDtypeStruct(q_cm.shape, BF16),
        scratch_shapes=[
            pltpu.VMEM((2, N_CLASS, 2 * SPAN, ATTN_STEP_LANES), BF16),
            pltpu.VMEM((2, N_CLASS, 2 * SPAN, ATTN_STEP_LANES), BF16),
            pltpu.VMEM((N_CLASS, SPAN, ATTN_STEP_LANES), F32),
            pltpu.VMEM((N_CLASS, SPAN, ATTN_STEP_LANES), F32),
            pltpu.VMEM((N_CLASS, SPAN, ATTN_STEP_LANES), F32),
            pltpu.VMEM((N_CLASS, SPAN, ATTN_STEP_LANES), F32),
            pltpu.VMEM((2, ATTN_GROUP, npair, 2 * SPAN, 2 * SPAN), F32),
            pltpu.VMEM((2, ATTN_GROUP, npair, 2 * SPAN, 2 * SPAN), BF16),
            pltpu.VMEM((2, ATTN_GROUP, npair, SPAN, LANES), F32),
        ],
        compiler_params=pltpu.CompilerParams(
            dimension_semantics=("arbitrary", "arbitrary"), vmem_limit_bytes=VMEM_LIMIT_BYTES),
        name="prompt_attention",
    )(q_cm, k_cm, v_cm)


def _sample_attn_kernel(q_ref, kn_ref, vn_ref, kt_ref, vt_ref, o_ref, *, t_new, lbuf):
    width = ATTN_WIDTH
    nrow = t_new * N_HEADS
    q = q_ref[...]
    kn = kn_ref[...]
    vnew = vn_ref[...]

    row = lax.broadcasted_iota(jnp.int32, (nrow, width), 0)
    lane = lax.broadcasted_iota(jnp.int32, (nrow, width), 1)
    hmask = (lane // HEAD_DIM) == (row % N_HEADS)
    qtok = jnp.concatenate([jnp.broadcast_to(q[j:j + 1, :], (N_HEADS, width)) for j in range(t_new)], axis=0)
    qrows = jnp.where(hmask, qtok, 0.0)

    kt = kt_ref[...].reshape(width, lbuf).astype(BF16)
    vt = vt_ref[...].reshape(width, lbuf).astype(BF16)
    s = jnp.dot(qrows.astype(BF16), kt, preferred_element_type=F32)

    def reach_count(delta):
        cnt = jnp.zeros(delta.shape, F32)
        for window, dil in DILATIONS:
            hit = (delta >= 0) & (delta % dil == 0) & (delta <= window)
            cnt = cnt + jnp.where(hit, 1.0, 0.0)
        return cnt

    pos = lax.broadcasted_iota(jnp.int32, (nrow, lbuf), 1)
    tok = lax.broadcasted_iota(jnp.int32, (nrow, lbuf), 0) // N_HEADS
    cnt = reach_count(lbuf + tok - pos)
    tok1 = tok[:, 0:1]
    cnew = [reach_count(tok1 - j) for j in range(t_new)]
    snew = [jnp.sum(qrows * kn[j:j + 1, :], axis=-1, keepdims=True) for j in range(t_new)]

    s = jnp.where(cnt > 0.0, s, NEG)
    m = jnp.max(s, axis=-1, keepdims=True)
    for j in range(t_new):
        m = jnp.maximum(m, jnp.where(cnew[j] > 0.0, snew[j], NEG))
    e = jnp.exp(s - m) * cnt
    l = jnp.sum(e, axis=-1, keepdims=True)
    o = lax.dot_general(e.astype(BF16), vt, (((1,), (1,)), ((), ())), preferred_element_type=F32)
    for j in range(t_new):
        ej = jnp.where(cnew[j] > 0.0, jnp.exp(snew[j] - m), 0.0) * cnew[j]
        l = l + ej
        o = o + ej * vnew[j:j + 1, :]
    o = jnp.where(hmask, o / l, 0.0)
    for j in range(t_new):
        o_ref[j:j + 1, :] = jnp.sum(o[j * N_HEADS:(j + 1) * N_HEADS, :], axis=0, keepdims=True)


def _sample_attention(q, k_new, v_new, cache_kt, cache_vt):
    nb, t_new, width = q.shape
    lbuf = cache_kt.shape[-1]
    pad = lambda a: jnp.pad(a, ((0, 0), (0, 8 - t_new), (0, 0)))
    new_spec = pl.BlockSpec((None, 8, width), lambda b: (b, 0, 0))
    cache_spec = pl.BlockSpec((None, N_HEADS, HEAD_DIM, lbuf), lambda b: (b, 0, 0, 0))
    return pl.pallas_call(
        functools.partial(_sample_attn_kernel, t_new=t_new, lbuf=lbuf),
        grid=(nb,),
        in_specs=[new_spec, new_spec, new_spec, cache_spec, cache_spec],
        out_specs=pl.BlockSpec((None, t_new, width), lambda b: (b, 0, 0)),
        out_shape=jax.ShapeDtypeStruct((nb, t_new, width), F32),
        compiler_params=pltpu.CompilerParams(
            dimension_semantics=("arbitrary",), vmem_limit_bytes=VMEM_LIMIT_BYTES),
        name="sample_attention",
    )(pad(q), pad(k_new), pad(v_new), cache_kt, cache_vt)


def _finish_kernel(x_ref, attn_ref, gated_ref, p_ref, wo_ref, g2_ref, wup_ref, wdn_ref,
                   gg_ref, wg_ref, wple_ref, gf_ref, y_ref, *, apply_final):
    mix = jnp.concatenate([attn_ref[...], gated_ref[...]], axis=-1)
    h = x_ref[...] + jnp.dot(mix, wo_ref[...], preferred_element_type=F32)
    n2 = _rms(h, g2_ref[...]).astype(BF16)
    f = jnp.zeros_like(h)
    for c in range(D_FF // FF_CHUNK):
        cs = slice(c * FF_CHUNK, (c + 1) * FF_CHUNK)
        up = jnp.dot(n2, wup_ref[:, cs], preferred_element_type=F32)
        act = jnp.square(jnp.maximum(up, 0.0)).astype(BF16)
        f = f + jnp.dot(act, wdn_ref[cs, :], preferred_element_type=F32)
    h = h + f
    gate = jax.nn.sigmoid(jnp.dot(_rms(h, gg_ref[...]).astype(BF16), wg_ref[...], preferred_element_type=F32))
    ple = jnp.dot(p_ref[...].astype(BF16), wple_ref[...], preferred_element_type=F32)
    h = h + gate * ple
    y_ref[...] = _rms(h, gf_ref[...]) if apply_final else h


def _finish(x, attn, gated, p, wo, g2, wup, wdn, gg, wg, wple, gf, *, rows, apply_final):
    n_rows = x.shape[0]
    row_spec = lambda width: pl.BlockSpec((rows, width), lambda i: (i, 0))
    return pl.pallas_call(
        functools.partial(_finish_kernel, apply_final=apply_final),
        grid=(n_rows // rows,),
        in_specs=[
            row_spec(D_MODEL), row_spec(ATTN_WIDTH), row_spec(GMLP_WIDTH), row_spec(PLE_DIM),
            _const_spec(wo.shape), _const_spec((1, D_MODEL)), _const_spec(wup.shape), _const_spec(wdn.shape),
            _const_spec((1, D_MODEL)), _const_spec(wg.shape), _const_spec(wple.shape), _const_spec((1, D_MODEL)),
        ],
        out_specs=row_spec(D_MODEL),
        out_shape=jax.ShapeDtypeStruct((n_rows, D_MODEL), F32),
        compiler_params=pltpu.CompilerParams(
            dimension_semantics=("arbitrary",), vmem_limit_bytes=VMEM_LIMIT_BYTES),
        name="finish",
    )(x, attn, gated, p, wo, g2, wup, wdn, gg, wg, wple, gf)


def _pair_spatial(w):
    g, l, _ = w.shape
    return w.reshape(g // 2, 2, l, l).transpose(0, 2, 1, 3).reshape(g // 2, l, 2 * l).astype(BF16)


def _bias_lanes(b):
    return jnp.repeat(b.T, GMLP_WIDTH // GMLP_GROUPS, axis=1)


def kernel(x_prompt, x_sample, cache_k, cache_v, p_prompt, p_sample, norm1_g, w_in, ln_v_g, ln_v_b,
           w_spatial, b_spatial, w_out, norm2_g, w_up, w_down, gate_norm_g, w_gate, w_ple, final_g):
    depth = w_in.shape[0]
    nbp, seq, _ = x_prompt.shape
    nbs, t_new, _ = x_sample.shape
    lbuf = cache_k.shape[2]
    keep = min(WINDOW_MAX, seq)
    assert nbp == 1 and seq % (N_CLASS * SPAN) == 0 and (nbs * t_new) % CHUNK == 0 and t_new <= CHUNK
    n_s = nbs * t_new

    row2 = lambda a: a.reshape(1, -1)

    hp = x_prompt.reshape(seq, D_MODEL)
    hs = x_sample.reshape(n_s, D_MODEL)
    nk_p, nv_p, nk_s, nv_s, nvc_s = [], [], [], [], []
    for i in range(depth):
        last = i == depth - 1
        w_in_b = w_in[i].astype(BF16)
        wo_b, wup_b, wdn_b = w_out[i].astype(BF16), w_up[i].astype(BF16), w_down[i].astype(BF16)
        wg_b, wple_b = w_gate[i].astype(BF16), w_ple[i].astype(BF16)
        fin_w = (wo_b, row2(norm2_g[i]), wup_b, wdn_b, row2(gate_norm_g[i]), wg_b, wple_b, row2(final_g))
        ln_g, ln_b = row2(ln_v_g[i]), row2(ln_v_b[i])

        wsp_p = _pair_spatial(w_spatial[i])
        bsp_p = _bias_lanes(b_spatial[i])
        q, k, v, kf, vf, gated = _project(hp, row2(norm1_g[i]), w_in_b, ln_g, ln_b, wsp_p, bsp_p,
                                          rows=512, tail_rows=keep, pos_base=0, pos_period=512,
                                          class_major=True, emit_vn=False)
        attn_cm = _prompt_attention(q, k, v)
        attn = attn_cm.transpose(1, 0, 2).reshape(seq, ATTN_WIDTH)
        hp = _finish(hp, attn, gated, p_prompt[i].reshape(seq, PLE_DIM), *fin_w, rows=512, apply_final=last)
        nk_p.append(kf.reshape(nbp, keep, N_HEADS, HEAD_DIM))
        nv_p.append(vf.reshape(nbp, keep, N_HEADS, HEAD_DIM))

        seq_of_row = jnp.arange(n_s) // t_new
        same_seq = seq_of_row[:, None] == seq_of_row[None, :]
        corner = jnp.tile(w_spatial[i][:, :t_new, :t_new], (1, nbs, nbs))
        wsp_s = _pair_spatial(jnp.where(same_seq[None], corner, 0.0))
        bsp_s = _bias_lanes(jnp.tile(b_spatial[i][:, :t_new], (1, nbs)))
        q, k, v, kf, vf, gated, vn = _project(hs, row2(norm1_g[i]), w_in_b, ln_g, ln_b, wsp_s, bsp_s,
                                              rows=n_s, tail_rows=n_s, pos_base=PAST_LEN, pos_period=t_new,
                                              class_major=False, emit_vn=True)
        r3 = lambda a: a.reshape(nbs, t_new, ATTN_WIDTH)
        attn = _sample_attention(r3(q).astype(F32), r3(kf), r3(vf),
                                 cache_k[i].transpose(0, 2, 3, 1), cache_v[i].transpose(0, 2, 3, 1))
        hs = _finish(hs, attn.reshape(n_s, ATTN_WIDTH).astype(BF16), gated, p_sample[i].reshape(n_s, PLE_DIM),
                     *fin_w, rows=n_s, apply_final=last)
        nk_s.append(kf.reshape(nbs, t_new, N_HEADS, HEAD_DIM))
        nv_s.append(vf.reshape(nbs, t_new, N_HEADS, HEAD_DIM))
        nvc_s.append(vn.reshape(nbs, t_new, GMLP_WIDTH))

    return (hp.reshape(nbp, seq, D_MODEL), hs.reshape(nbs, t_new, D_MODEL),
            jnp.stack(nk_p), jnp.stack(nv_p), jnp.stack(nk_s), jnp.stack(nv_s), jnp.stack(nvc_s))
```

```python
import functools

import jax
import jax.numpy as jnp
from jax import lax
from jax.experimental import pallas as pl
from jax.experimental.pallas import tpu as pltpu

D_MODEL = 1024
N_HEADS = 8
HEAD_DIM = 64
ATTN_WIDTH = N_HEADS * HEAD_DIM
GMLP_GROUPS = 8
GMLP_WIDTH = 512
CHUNK = 128
DILATIONS = ((128, 1), (512, 4), (2048, 16))
WINDOW_MAX = 2048
PAST_LEN = 16384
ROT_DIM = HEAD_DIM // 4
ROPE_THETA = 500000.0
D_FF = 4 * D_MODEL
PLE_DIM = 256
EPS = 1e-6
NEG = -1e30
Q_SCALE = HEAD_DIM ** -0.5 * 1.4426950408889634

LANES = 128
VMEM_LIMIT_BYTES = 56 * 1024 * 1024

N_CLASS = 16
SPAN = 128
PAIRS = ATTN_WIDTH // LANES
ATTN_STEP_LANES = 2 * LANES
ATTN_GROUP = 2
SUBLANES = 8
PACKED_ROWS = 16
FF_CHUNK = 1024

BF16 = jnp.bfloat16
F32 = jnp.float32


def _rms(x, g):
    ms = jnp.mean(x * x, axis=-1, keepdims=True)
    return x * lax.rsqrt(ms + EPS) * g


def _const_spec(shape):
    nd = len(shape)
    return pl.BlockSpec(shape, lambda *_: (0,) * nd, pipeline_mode=pl.Buffered(1))


def _project_kernel(x_ref, g1_ref, w_ref, invf_ref, lng_ref, lnb_ref, wsp_ref, bsp_ref,
                    q_ref, k_ref, v_ref, kf_ref, vf_ref, gated_ref, *rest,
                    rows, tail_from, pos_base, pos_period, class_major, emit_vn):
    rest = list(rest)
    vn_ref = rest.pop(0) if emit_vn else None
    cos_off_ref, sin_off_ref = rest.pop(0), rest.pop(0)
    zs_ref = rest.pop(0) if class_major else None
    i = pl.program_id(0)

    @pl.when(i == 0)
    def _():
        off = (lax.broadcasted_iota(jnp.int32, (rows, LANES), 0) % pos_period).astype(F32)
        ang = off * invf_ref[...]
        cos_off_ref[...] = jnp.cos(ang)
        sin_off_ref[...] = jnp.sin(ang)

    tile_stride = rows if pos_period == rows else 0
    base = (i * tile_stride).astype(F32) + pos_base
    base_ang = base * invf_ref[...]
    cb, sb = jnp.cos(base_ang), jnp.sin(base_ang)
    co, so = cos_off_ref[...], sin_off_ref[...]
    cosf = cb * co - sb * so
    sint = sb * co + cb * so
    head_lane = lax.broadcasted_iota(jnp.int32, (1, LANES), 1) % HEAD_DIM
    sina = jnp.where(head_lane < ROT_DIM // 2, -sint, 0.0)
    sinb = jnp.where((head_lane >= ROT_DIM // 2) & (head_lane < ROT_DIM), sint, 0.0)

    def rope(z):
        return z * cosf + pltpu.roll(z, LANES - ROT_DIM // 2, 1) * sina + pltpu.roll(z, ROT_DIM // 2, 1) * sinb

    xn = _rms(x_ref[...], g1_ref[...]).astype(BF16)
    zq = jnp.dot(xn, w_ref[:, 0:ATTN_WIDTH], preferred_element_type=F32)
    zk = jnp.dot(xn, w_ref[:, ATTN_WIDTH:2 * ATTN_WIDTH], preferred_element_type=F32)
    zv = jnp.dot(xn, w_ref[:, 2 * ATTN_WIDTH:3 * ATTN_WIDTH], preferred_element_type=F32)
    slabs = [slice(s * LANES, (s + 1) * LANES) for s in range(PAIRS)]
    qr = [rope(zq[:, sl]) * Q_SCALE for sl in slabs]
    kr = [rope(zk[:, sl]) for sl in slabs]

    @pl.when(i >= tail_from)
    def _():
        for s, sl in enumerate(slabs):
            kf_ref[:, sl] = kr[s]
        vf_ref[...] = zv

    if class_major:
        per_class = rows // N_CLASS
        for s, sl in enumerate(slabs):
            zs_ref[s] = qr[s]
            zs_ref[PAIRS + s] = kr[s]
            zs_ref[2 * PAIRS + s] = zv[:, sl]
        for r in range(N_CLASS):
            for s, sl in enumerate(slabs):
                pick = lambda n: zs_ref[n, pl.ds(r, per_class, stride=N_CLASS), :].astype(BF16)
                q_ref[r, :, sl] = pick(s)
                k_ref[r, :, sl] = pick(PAIRS + s)
                v_ref[r, :, sl] = pick(2 * PAIRS + s)
    else:
        for s, sl in enumerate(slabs):
            q_ref[:, sl] = qr[s].astype(BF16)
            k_ref[:, sl] = kr[s].astype(BF16)
        v_ref[...] = zv.astype(BF16)

    o_u = 3 * ATTN_WIDTH
    u = jax.nn.gelu(jnp.dot(xn, w_ref[:, o_u:o_u + GMLP_WIDTH], preferred_element_type=F32))
    vc = jax.nn.gelu(jnp.dot(xn, w_ref[:, o_u + GMLP_WIDTH:o_u + 2 * GMLP_WIDTH], preferred_element_type=F32))
    mu = jnp.mean(vc, axis=-1, keepdims=True)
    cen = vc - mu
    var = jnp.mean(cen * cen, axis=-1, keepdims=True)
    vn = cen * lax.rsqrt(var + EPS) * lng_ref[...] + lnb_ref[...]
    if emit_vn:
        vn_ref[...] = vn
    vnb = vn.astype(BF16)

    row = lax.broadcasted_iota(jnp.int32, (CHUNK, 2 * CHUNK), 0)
    col = lax.broadcasted_iota(jnp.int32, (CHUNK, 2 * CHUNK), 1)
    tril = (col % CHUNK) <= row
    lane = lax.broadcasted_iota(jnp.int32, (CHUNK, LANES), 1)
    first = lane < HEAD_DIM
    zero = jnp.zeros((CHUNK, LANES), BF16)
    for s in range(GMLP_WIDTH // LANES):
        sl = slice(s * LANES, (s + 1) * LANES)
        wp = jnp.where(tril, wsp_ref[s], jnp.zeros((), BF16))
        bias = bsp_ref[:, sl]
        for c in range(rows // CHUNK):
            rs = slice(c * CHUNK, (c + 1) * CHUNK)
            vblk = vnb[rs, sl]
            rhs = jnp.concatenate([jnp.where(first, vblk, zero), jnp.where(first, zero, vblk)], axis=0)
            mixed = jnp.dot(wp, rhs, preferred_element_type=F32) + bias
            gated_ref[rs, sl] = (u[rs, sl] * mixed).astype(BF16)


def _project(x, g1, w_in_b, ln_g, ln_b, wsp, bsp, *, rows, tail_rows, pos_base, pos_period, class_major, emit_vn):
    n_rows = x.shape[0]
    grid = n_rows // rows
    tail_from = (n_rows - tail_rows) // rows
    row_spec = lambda width: pl.BlockSpec((rows, width), lambda i: (i, 0))
    tail_spec = pl.BlockSpec((rows, ATTN_WIDTH), lambda i: (jnp.maximum(i - tail_from, 0), 0))
    if class_major:
        qkv_shape = jax.ShapeDtypeStruct((N_CLASS, n_rows // N_CLASS, ATTN_WIDTH), BF16)
        qkv_spec = pl.BlockSpec((N_CLASS, rows // N_CLASS, ATTN_WIDTH), lambda i: (0, i, 0))
    else:
        qkv_shape = jax.ShapeDtypeStruct((n_rows, ATTN_WIDTH), BF16)
        qkv_spec = row_spec(ATTN_WIDTH)
    out_shape = [
        qkv_shape,
        qkv_shape,
        qkv_shape,
        jax.ShapeDtypeStruct((tail_rows, ATTN_WIDTH), F32),
        jax.ShapeDtypeStruct((tail_rows, ATTN_WIDTH), F32),
        jax.ShapeDtypeStruct((n_rows, GMLP_WIDTH), BF16),
    ]
    out_specs = [qkv_spec] * 3 + [tail_spec] * 2 + [row_spec(GMLP_WIDTH)]
    if emit_vn:
        out_shape.append(jax.ShapeDtypeStruct((n_rows, GMLP_WIDTH), F32))
        out_specs.append(row_spec(GMLP_WIDTH))
    scratch = [pltpu.VMEM((rows, LANES), F32), pltpu.VMEM((rows, LANES), F32)]
    if class_major:
        scratch.append(pltpu.VMEM((3 * PAIRS, rows, LANES), F32))
    inv_freq = ROPE_THETA ** (-jnp.arange(0, ROT_DIM, 2, dtype=F32) / ROT_DIM)
    per_head = jnp.concatenate([inv_freq, inv_freq, jnp.zeros((HEAD_DIM - ROT_DIM,), F32)])
    invf = jnp.tile(per_head, LANES // HEAD_DIM).reshape(1, LANES)
    return pl.pallas_call(
        functools.partial(_project_kernel, rows=rows, tail_from=tail_from, pos_base=float(pos_base),
                          pos_period=pos_period, class_major=class_major, emit_vn=emit_vn),
        grid=(grid,),
        in_specs=[
            row_spec(D_MODEL),
            _const_spec((1, D_MODEL)),
            _const_spec(w_in_b.shape),
            _const_spec((1, LANES)),
            _const_spec((1, GMLP_WIDTH)),
            _const_spec((1, GMLP_WIDTH)),
            _const_spec(wsp.shape),
            _const_spec(bsp.shape),
        ],
        out_specs=out_specs,
        out_shape=out_shape,
        scratch_shapes=scratch,
        compiler_params=pltpu.CompilerParams(
            dimension_semantics=("arbitrary",), vmem_limit_bytes=VMEM_LIMIT_BYTES),
        name="project",
    )(x, g1, w_in_b, invf, ln_g, ln_b, wsp, bsp)


def _prompt_attn_kernel(q_ref, k_ref, v_ref, o_ref, kc_ref, vc_ref, q32_ref, acc_ref, m_ref, l_ref,
                        s_ref, p_ref, ms_ref):
    i = pl.program_id(1)
    rows = q_ref.shape[1]
    width = q_ref.shape[2]
    npair = width // LANES
    half = SUBLANES

    @pl.when((pl.program_id(0) == 0) & (i == 0))
    def _():
        p_ref[1] = jnp.zeros(p_ref.shape[1:], BF16)
        ms_ref[1] = jnp.zeros(ms_ref.shape[1:], F32)

    @pl.when(i == 0)
    def _():
        kc_ref[0, :, 0:rows, :] = jnp.zeros((N_CLASS, rows, width), BF16)
        vc_ref[0, :, 0:rows, :] = jnp.zeros((N_CLASS, rows, width), BF16)

    @pl.when(i > 0)
    def _():
        kc_ref[0, :, 0:rows, :] = kc_ref[0, :, rows:2 * rows, :]
        vc_ref[0, :, 0:rows, :] = vc_ref[0, :, rows:2 * rows, :]

    kc_ref[0, :, rows:2 * rows, :] = k_ref[...]
    vc_ref[0, :, rows:2 * rows, :] = v_ref[...]
    lo = rows - PACKED_ROWS
    for ref in (kc_ref, vc_ref):
        tail = ref[0, :, lo:2 * rows, :].astype(F32)
        ref[1, :, lo:2 * rows - PACKED_ROWS, :] = tail[:, half:half + rows, :].astype(BF16)
    q32_ref[...] = q_ref[...].astype(F32)
    has_prev = i > 0

    def lanes(pr):
        return slice(pr * LANES, (pr + 1) * LANES)

    mq, nk = rows, 2 * rows
    olane = lax.broadcasted_iota(jnp.int32, (mq, LANES), 1) < HEAD_DIM
    qzero = jnp.zeros((mq, LANES), BF16)
    ones_v = jnp.ones((nk, LANES), BF16)
    nt = (((1,), (1,)), ((), ()))

    def run(n_units, q_of, k_of, v_of, mask_of, state_of, put, first, last):
        units = lambda t: [t * ATTN_GROUP + j for j in range(ATTN_GROUP)]

        def scores(t, slot):
            for j, u in enumerate(units(t)):
                for pr in range(npair):
                    q = q_of(u, pr)
                    qq = jnp.concatenate([jnp.where(olane, q, qzero), jnp.where(olane, qzero, q)], axis=0)
                    s_ref[slot, j, pr] = lax.dot_general(qq, k_of(u, pr), nt, preferred_element_type=F32)

        def softmax(t, slot):
            for j, u in enumerate(units(t)):
                mask = mask_of(u)
                for pr in range(npair):
                    ms = []
                    for hh in range(2):
                        hs = slice(hh * mq, (hh + 1) * mq)
                        s = jnp.where(mask, s_ref[slot, j, pr, hs, :], NEG)
                        m = jnp.max(s, axis=-1, keepdims=True)
                        p_ref[slot, j, pr, hs, :] = jnp.exp2(s - m).astype(BF16)
                        ms.append(m)
                    ms_ref[slot, j, pr] = jnp.where(olane, ms[0], ms[1])

        def values(t, slot, valid):
            for j, u in enumerate(units(t)):
                for pr in range(npair):
                    vv = jnp.concatenate([v_of(u, pr), ones_v], axis=1)
                    r = jnp.dot(p_ref[slot, j, pr], vv, preferred_element_type=F32)
                    pv = jnp.where(olane, r[0:mq, 0:LANES], r[mq:2 * mq, 0:LANES])
                    l = jnp.where(olane, r[0:mq, LANES:2 * LANES], r[mq:2 * mq, LANES:2 * LANES])
                    m = ms_ref[slot, j, pr]
                    if not first:
                        acc0, m0, l0 = state_of(u, pr)
                        m_new = jnp.maximum(m0, m)
                        a = jnp.exp2(m0 - m_new)
                        b = jnp.exp2(m - m_new)
                        pv = acc0 * a + pv * b
                        l = l0 * a + l * b
                        m = m_new
                        if last:
                            pv = pv / l
                        if valid is not None:
                            pv, m, l = jnp.where(valid, pv, acc0), jnp.where(valid, m, m0), jnp.where(valid, l, l0)
                    put(u, pr, pv if last else (pv, m, l))

        n_trips = n_units // ATTN_GROUP
        scores(jnp.int32(0), 0)

        def trip(t, carry):
            slot = t % 2
            values(jnp.maximum(t - 1, 0), 1 - slot, t > 0)
            softmax(t, slot)
            scores(jnp.minimum(t + 1, n_trips - 1), 1 - slot)
            return carry

        lax.fori_loop(0, n_trips, trip, 0)
        values(jnp.int32(n_trips - 1), (n_trips - 1) % 2, None)

    def put_state(slabs_of, r0_of, n):
        def put(u, pr, res):
            for a, sb in enumerate(slabs_of(u)):
                rs = slice(a * n, (a + 1) * n)
                acc_ref[sb, pl.ds(r0_of(u), n), lanes(pr)] = res[0][rs]
                m_ref[sb, pl.ds(r0_of(u), n), lanes(pr)] = res[1][rs]
                l_ref[sb, pl.ds(r0_of(u), n), lanes(pr)] = res[2][rs]
        return put

    def gather(ref, slabs, r0, n, pr):
        return jnp.concatenate([ref[sb, pl.ds(r0, n), lanes(pr)] for sb in slabs], axis=0)

    state_refs = (acc_ref, m_ref, l_ref)
    kplain, vplain = kc_ref.at[0], vc_ref.at[0]

    qi = lax.broadcasted_iota(jnp.int32, (mq, nk), 0)
    kj = lax.broadcasted_iota(jnp.int32, (mq, nk), 1)
    diff = qi + rows - kj
    mask16 = (diff >= 0) & (diff <= SPAN) & ((kj >= rows) | has_prev)
    run(N_CLASS,
        q_of=lambda u, pr: q_ref[u, :, lanes(pr)],
        k_of=lambda u, pr: kplain[u, :, lanes(pr)],
        v_of=lambda u, pr: vplain[u, :, lanes(pr)],
        mask_of=lambda u: mask16, state_of=None,
        put=put_state(lambda u: [u], lambda u: 0, rows), first=True, last=False)

    n4 = N_CLASS // 4
    qb = rows // n4
    d4 = 4 * (qi % qb - kj % (2 * qb) + qb) + (qi // qb - kj // (2 * qb))
    band4 = (d4 >= 0) & (d4 <= SPAN)
    cur4 = kj % (2 * qb) >= qb
    slabs4 = lambda u: [u // n4 + 4 * a for a in range(n4)]
    q0_4 = lambda u: pl.multiple_of((u % n4) * qb, qb)
    k0_4 = lambda u: pl.multiple_of(rows - qb + (u % n4) * qb, qb)
    run(N_CLASS,
        q_of=lambda u, pr: gather(q_ref, slabs4(u), q0_4(u), qb, pr),
        k_of=lambda u, pr: gather(kplain, slabs4(u), k0_4(u), 2 * qb, pr),
        v_of=lambda u, pr: gather(vplain, slabs4(u), k0_4(u), 2 * qb, pr),
        mask_of=lambda u: band4 & (cur4 | has_prev | (u % n4 > 0)),
        state_of=lambda u, pr: tuple(gather(ref, slabs4(u), q0_4(u), qb, pr) for ref in state_refs),
        put=put_state(slabs4, q0_4, qb), first=False, last=False)

    d1 = N_CLASS * (qi % half - kj % (2 * half) + half) + (qi // half - kj // (2 * half))
    band1 = (d1 >= 0) & (d1 <= SPAN)
    cur1 = kj % (2 * half) >= half
    every = list(range(N_CLASS))
    q0_1 = lambda u: pl.multiple_of(u * half, half)
    kcopy = lambda u: 1 - u % 2
    k0_1 = lambda u: pl.multiple_of(((rows - half + u * half) // PACKED_ROWS) * PACKED_ROWS, PACKED_ROWS)

    def put_out(u, pr, res):
        for sb in every:
            acc_ref[sb, pl.ds(q0_1(u), half), lanes(pr)] = res[sb * half:(sb + 1) * half]

    run(rows // half,
        q_of=lambda u, pr: gather(q32_ref, every, q0_1(u), half, pr).astype(BF16),
        k_of=lambda u, pr: gather(kc_ref.at[kcopy(u)], every, k0_1(u), 2 * half, pr),
        v_of=lambda u, pr: gather(vc_ref.at[kcopy(u)], every, k0_1(u), 2 * half, pr),
        mask_of=lambda u: band1 & (cur1 | has_prev | (u > 0)),
        state_of=lambda u, pr: tuple(gather(ref, every, q0_1(u), half, pr) for ref in state_refs),
        put=put_out, first=False, last=True)
    o_ref[...] = acc_ref[...].astype(BF16)


def _prompt_attention(q_cm, k_cm, v_cm):
    n_slab_rows = q_cm.shape[1]
    npair = ATTN_STEP_LANES // LANES
    spec = pl.BlockSpec((N_CLASS, SPAN, ATTN_STEP_LANES), lambda h, i: (0, i, h))
    return pl.pallas_call(
        _prompt_attn_kernel,
        grid=(ATTN_WIDTH // ATTN_STEP_LANES, n_slab_rows // SPAN),
        in_specs=[spec, spec, spec],
        out_specs=spec,
        out_shape=jax.ShapeDtypeStruct(q_cm.shape, BF16),
        scratch_shapes=[
            pltpu.VMEM((2, N_CLASS, 2 * SPAN, ATTN_STEP_LANES), BF16),
            pltpu.VMEM((2, N_CLASS, 2 * SPAN, ATTN_STEP_LANES), BF16),
            pltpu.VMEM((N_CLASS, SPAN, ATTN_STEP_LANES), F32),
            pltpu.VMEM((N_CLASS, SPAN, ATTN_STEP_LANES), F32),
            pltpu.VMEM((N_CLASS, SPAN, ATTN_STEP_LANES), F32),
            pltpu.VMEM((N_CLASS, SPAN, ATTN_STEP_LANES), F32),
            pltpu.VMEM((2, ATTN_GROUP, npair, 2 * SPAN, 2 * SPAN), F32),
            pltpu.VMEM((2, ATTN_GROUP, npair, 2 * SPAN, 2 * SPAN), BF16),
            pltpu.VMEM((2, ATTN_GROUP, npair, SPAN, LANES), F32),
        ],
        compiler_params=pltpu.CompilerParams(
            dimension_semantics=("arbitrary", "arbitrary"), vmem_limit_bytes=VMEM_LIMIT_BYTES),
        name="prompt_attention",
    )(q_cm, k_cm, v_cm)


def _sample_attn_kernel(q_ref, kn_ref, vn_ref, kt_ref, vt_ref, o_ref, *, t_new, lbuf):
    width = ATTN_WIDTH
    nrow = t_new * N_HEADS
    q = q_ref[...]
    kn = kn_ref[...]
    vnew = vn_ref[...]

    row = lax.broadcasted_iota(jnp.int32, (nrow, width), 0)
    lane = lax.broadcasted_iota(jnp.int32, (nrow, width), 1)
    hmask = (lane // HEAD_DIM) == (row % N_HEADS)
    qtok = jnp.concatenate([jnp.broadcast_to(q[j:j + 1, :], (N_HEADS, width)) for j in range(t_new)], axis=0)
    qrows = jnp.where(hmask, qtok, 0.0)

    kt = kt_ref[...].reshape(width, lbuf).astype(BF16)
    vt = vt_ref[...].reshape(width, lbuf).astype(BF16)
    s = jnp.dot(qrows.astype(BF16), kt, preferred_element_type=F32)

    def reach_count(delta):
        cnt = jnp.zeros(delta.shape, F32)
        for window, dil in DILATIONS:
            hit = (delta >= 0) & (delta % dil == 0) & (delta <= window)
            cnt = cnt + jnp.where(hit, 1.0, 0.0)
        return cnt

    pos = lax.broadcasted_iota(jnp.int32, (nrow, lbuf), 1)
    tok = lax.broadcasted_iota(jnp.int32, (nrow, lbuf), 0) // N_HEADS
    cnt = reach_count(lbuf + tok - pos)
    tok1 = tok[:, 0:1]
    cnew = [reach_count(tok1 - j) for j in range(t_new)]
    snew = [jnp.sum(qrows * kn[j:j + 1, :], axis=-1, keepdims=True) for j in range(t_new)]

    s = jnp.where(cnt > 0.0, s, NEG)
    m = jnp.max(s, axis=-1, keepdims=True)
    for j in range(t_new):
        m = jnp.maximum(m, jnp.where(cnew[j] > 0.0, snew[j], NEG))
    e = jnp.exp2(s - m) * cnt
    l = jnp.sum(e, axis=-1, keepdims=True)
    o = lax.dot_general(e.astype(BF16), vt, (((1,), (1,)), ((), ())), preferred_element_type=F32)
    for j in range(t_new):
        ej = jnp.where(cnew[j] > 0.0, jnp.exp2(snew[j] - m), 0.0) * cnew[j]
        l = l + ej
        o = o + ej * vnew[j:j + 1, :]
    o = jnp.where(hmask, o / l, 0.0)
    for j in range(t_new):
        o_ref[j:j + 1, :] = jnp.sum(o[j * N_HEADS:(j + 1) * N_HEADS, :], axis=0, keepdims=True)


def _sample_attention(q, k_new, v_new, cache_kt, cache_vt):
    nb, t_new, width = q.shape
    lbuf = cache_kt.shape[-1]
    pad = lambda a: jnp.pad(a, ((0, 0), (0, 8 - t_new), (0, 0)))
    new_spec = pl.BlockSpec((None, 8, width), lambda b: (b, 0, 0))
    cache_spec = pl.BlockSpec((None, N_HEADS, HEAD_DIM, lbuf), lambda b: (b, 0, 0, 0))
    return pl.pallas_call(
        functools.partial(_sample_attn_kernel, t_new=t_new, lbuf=lbuf),
        grid=(nb,),
        in_specs=[new_spec, new_spec, new_spec, cache_spec, cache_spec],
        out_specs=pl.BlockSpec((None, t_new, width), lambda b: (b, 0, 0)),
        out_shape=jax.ShapeDtypeStruct((nb, t_new, width), F32),
        compiler_params=pltpu.CompilerParams(
            dimension_semantics=("arbitrary",), vmem_limit_bytes=VMEM_LIMIT_BYTES),
        name="sample_attention",
    )(pad(q), pad(k_new), pad(v_new), cache_kt, cache_vt)


def _finish_kernel(x_ref, attn_ref, gated_ref, p_ref, wo_ref, g2_ref, wup_ref, wdn_ref,
                   gg_ref, wg_ref, wple_ref, gf_ref, y_ref, *nat_refs, apply_final):
    if nat_refs:
        nat_ref, per_class = nat_refs[0], attn_ref.shape[1]
        for r in range(N_CLASS):
            for s in range(PAIRS):
                nat_ref[s, pl.ds(r, per_class, stride=N_CLASS), :] = attn_ref[r, :, s * LANES:(s + 1) * LANES].astype(F32)
        attn = jnp.concatenate([nat_ref[s] for s in range(PAIRS)], axis=-1).astype(BF16)
    else:
        attn = attn_ref[...]
    mix = jnp.concatenate([attn, gated_ref[...]], axis=-1)
    h = x_ref[...] + jnp.dot(mix, wo_ref[...], preferred_element_type=F32)
    n2 = _rms(h, g2_ref[...]).astype(BF16)
    f = jnp.zeros_like(h)
    for c in range(D_FF // FF_CHUNK):
        cs = slice(c * FF_CHUNK, (c + 1) * FF_CHUNK)
        up = jnp.dot(n2, wup_ref[:, cs], preferred_element_type=F32)
        act = jnp.square(jnp.maximum(up, 0.0)).astype(BF16)
        f = f + jnp.dot(act, wdn_ref[cs, :], preferred_element_type=F32)
    h = h + f
    gate = jax.nn.sigmoid(jnp.dot(_rms(h, gg_ref[...]).astype(BF16), wg_ref[...], preferred_element_type=F32))
    ple = jnp.dot(p_ref[...].astype(BF16), wple_ref[...], preferred_element_type=F32)
    h = h + gate * ple
    y_ref[...] = _rms(h, gf_ref[...]) if apply_final else h


def _finish(x, attn, gated, p, wo, g2, wup, wdn, gg, wg, wple, gf, *, rows, apply_final):
    n_rows = x.shape[0]
    row_spec = lambda width: pl.BlockSpec((rows, width), lambda i: (i, 0))
    if attn.ndim == 3:
        attn_spec = pl.BlockSpec((N_CLASS, rows // N_CLASS, ATTN_WIDTH), lambda i: (0, i, 0))
        scratch = [pltpu.VMEM((PAIRS, rows, LANES), F32)]
    else:
        attn_spec, scratch = row_spec(ATTN_WIDTH), []
    return pl.pallas_call(
        functools.partial(_finish_kernel, apply_final=apply_final),
        grid=(n_rows // rows,),
        in_specs=[
            row_spec(D_MODEL), attn_spec, row_spec(GMLP_WIDTH), row_spec(PLE_DIM),
            _const_spec(wo.shape), _const_spec((1, D_MODEL)), _const_spec(wup.shape), _const_spec(wdn.shape),
            _const_spec((1, D_MODEL)), _const_spec(wg.shape), _const_spec(wple.shape), _const_spec((1, D_MODEL)),
        ],
        out_specs=row_spec(D_MODEL),
        out_shape=jax.ShapeDtypeStruct((n_rows, D_MODEL), F32),
        scratch_shapes=scratch,
        compiler_params=pltpu.CompilerParams(
            dimension_semantics=("arbitrary",), vmem_limit_bytes=VMEM_LIMIT_BYTES),
        name="finish",
    )(x, attn, gated, p, wo, g2, wup, wdn, gg, wg, wple, gf)


def _pair_spatial(w):
    g, l, _ = w.shape
    return w.reshape(g // 2, 2, l, l).transpose(0, 2, 1, 3).reshape(g // 2, l, 2 * l).astype(BF16)


def _bias_lanes(b):
    return jnp.repeat(b.T, GMLP_WIDTH // GMLP_GROUPS, axis=1)


def kernel(x_prompt, x_sample, cache_k, cache_v, p_prompt, p_sample, norm1_g, w_in, ln_v_g, ln_v_b,
           w_spatial, b_spatial, w_out, norm2_g, w_up, w_down, gate_norm_g, w_gate, w_ple, final_g):
    depth = w_in.shape[0]
    nbp, seq, _ = x_prompt.shape
    nbs, t_new, _ = x_sample.shape
    lbuf = cache_k.shape[2]
    keep = min(WINDOW_MAX, seq)
    assert nbp == 1 and seq % (N_CLASS * SPAN) == 0 and (nbs * t_new) % CHUNK == 0 and t_new <= CHUNK
    n_s = nbs * t_new

    row2 = lambda a: a.reshape(1, -1)

    hp = x_prompt.reshape(seq, D_MODEL)
    hs = x_sample.reshape(n_s, D_MODEL)
    nk_p, nv_p, nk_s, nv_s, nvc_s = [], [], [], [], []
    for i in range(depth):
        last = i == depth - 1
        w_in_b = w_in[i].astype(BF16)
        wo_b, wup_b, wdn_b = w_out[i].astype(BF16), w_up[i].astype(BF16), w_down[i].astype(BF16)
        wg_b, wple_b = w_gate[i].astype(BF16), w_ple[i].astype(BF16)
        fin_w = (wo_b, row2(norm2_g[i]), wup_b, wdn_b, row2(gate_norm_g[i]), wg_b, wple_b, row2(final_g))
        ln_g, ln_b = row2(ln_v_g[i]), row2(ln_v_b[i])

        wsp_p = _pair_spatial(w_spatial[i])
        bsp_p = _bias_lanes(b_spatial[i])
        q, k, v, kf, vf, gated = _project(hp, row2(norm1_g[i]), w_in_b, ln_g, ln_b, wsp_p, bsp_p,
                                          rows=512, tail_rows=keep, pos_base=0, pos_period=512,
                                          class_major=True, emit_vn=False)
        attn_cm = _prompt_attention(q, k, v)
        hp = _finish(hp, attn_cm, gated, p_prompt[i].reshape(seq, PLE_DIM), *fin_w, rows=512, apply_final=last)
        nk_p.append(kf.reshape(nbp, keep, N_HEADS, HEAD_DIM))
        nv_p.append(vf.reshape(nbp, keep, N_HEADS, HEAD_DIM))

        seq_of_row = jnp.arange(n_s) // t_new
        same_seq = seq_of_row[:, None] == seq_of_row[None, :]
        corner = jnp.tile(w_spatial[i][:, :t_new, :t_new], (1, nbs, nbs))
        wsp_s = _pair_spatial(jnp.where(same_seq[None], corner, 0.0))
        bsp_s = _bias_lanes(jnp.tile(b_spatial[i][:, :t_new], (1, nbs)))
        q, k, v, kf, vf, gated, vn = _project(hs, row2(norm1_g[i]), w_in_b, ln_g, ln_b, wsp_s, bsp_s,
                                              rows=n_s, tail_rows=n_s, pos_base=PAST_LEN, pos_period=t_new,
                                              class_major=False, emit_vn=True)
        r3 = lambda a: a.reshape(nbs, t_new, ATTN_WIDTH)
        attn = _sample_attention(r3(q).astype(F32), r3(kf), r3(vf),
                                 cache_k[i].transpose(0, 2, 3, 1), cache_v[i].transpose(0, 2, 3, 1))
        hs = _finish(hs, attn.reshape(n_s, ATTN_WIDTH).astype(BF16), gated, p_sample[i].reshape(n_s, PLE_DIM),
                     *fin_w, rows=n_s, apply_final=last)
        nk_s.append(kf.reshape(nbs, t_new, N_HEADS, HEAD_DIM))
        nv_s.append(vf.reshape(nbs, t_new, N_HEADS, HEAD_DIM))
        nvc_s.append(vn.reshape(nbs, t_new, GMLP_WIDTH))

    return (hp.reshape(nbp, seq, D_MODEL), hs.reshape(nbs, t_new, D_MODEL),
            jnp.stack(nk_p), jnp.stack(nv_p), jnp.stack(nk_s), jnp.stack(nv_s), jnp.stack(nvc_s))
```

```python
import functools

import jax
import jax.numpy as jnp
from jax import lax
from jax.experimental import pallas as pl
from jax.experimental.pallas import tpu as pltpu

D_MODEL = 1024
N_HEADS = 8
HEAD_DIM = 64
ATTN_WIDTH = N_HEADS * HEAD_DIM
GMLP_GROUPS = 8
GMLP_WIDTH = 512
CHUNK = 128
DILATIONS = ((128, 1), (512, 4), (2048, 16))
WINDOW_MAX = 2048
PAST_LEN = 16384
ROT_DIM = HEAD_DIM // 4
ROPE_THETA = 500000.0
D_FF = 4 * D_MODEL
PLE_DIM = 256
EPS = 1e-6
NEG = -1e30
Q_SCALE = HEAD_DIM ** -0.5 * 1.4426950408889634

LANES = 128
VMEM_LIMIT_BYTES = 56 * 1024 * 1024

N_CLASS = 16
SPAN = 128
PAIRS = ATTN_WIDTH // LANES
ATTN_STEP_LANES = 2 * LANES
ATTN_GROUP = 2
SUBLANES = 8
PACKED_ROWS = 16
FF_CHUNK = 1024

BF16 = jnp.bfloat16
F32 = jnp.float32


def _rms(x, g):
    ms = jnp.mean(x * x, axis=-1, keepdims=True)
    return x * lax.rsqrt(ms + EPS) * g


def _const_spec(shape):
    nd = len(shape)
    return pl.BlockSpec(shape, lambda *_: (0,) * nd, pipeline_mode=pl.Buffered(1))


def _project_kernel(x_ref, g1_ref, w_ref, invf_ref, lng_ref, lnb_ref, wsp_ref, bsp_ref,
                    q_ref, k_ref, v_ref, kf_ref, vf_ref, gated_ref, *rest,
                    rows, tail_from, tail_transposed, pos_base, pos_period, class_major, emit_vn, seq_rows):
    rest = list(rest)
    vn_ref = rest.pop(0) if emit_vn else None
    cos_off_ref, sin_off_ref = rest.pop(0), rest.pop(0)
    zs_ref = rest.pop(0) if class_major else None
    mix_ref = rest.pop(0) if seq_rows else None
    i = pl.program_id(0)

    @pl.when(i == 0)
    def _():
        off = (lax.broadcasted_iota(jnp.int32, (rows, LANES), 0) % pos_period).astype(F32)
        ang = off * invf_ref[...]
        cos_off_ref[...] = jnp.cos(ang)
        sin_off_ref[...] = jnp.sin(ang)

    tile_stride = rows if pos_period == rows else 0
    base = (i * tile_stride).astype(F32) + pos_base
    base_ang = base * invf_ref[...]
    cb, sb = jnp.cos(base_ang), jnp.sin(base_ang)
    co, so = cos_off_ref[...], sin_off_ref[...]
    cosf = cb * co - sb * so
    sint = sb * co + cb * so
    head_lane = lax.broadcasted_iota(jnp.int32, (1, LANES), 1) % HEAD_DIM
    sina = jnp.where(head_lane < ROT_DIM // 2, -sint, 0.0)
    sinb = jnp.where((head_lane >= ROT_DIM // 2) & (head_lane < ROT_DIM), sint, 0.0)

    def rope(z):
        return z * cosf + pltpu.roll(z, LANES - ROT_DIM // 2, 1) * sina + pltpu.roll(z, ROT_DIM // 2, 1) * sinb

    xn = _rms(x_ref[...], g1_ref[...]).astype(BF16)
    zq = jnp.dot(xn, w_ref[:, 0:ATTN_WIDTH], preferred_element_type=F32)
    zk = jnp.dot(xn, w_ref[:, ATTN_WIDTH:2 * ATTN_WIDTH], preferred_element_type=F32)
    zv = jnp.dot(xn, w_ref[:, 2 * ATTN_WIDTH:3 * ATTN_WIDTH], preferred_element_type=F32)
    slabs = [slice(s * LANES, (s + 1) * LANES) for s in range(PAIRS)]
    qr = [rope(zq[:, sl]) * Q_SCALE for sl in slabs]
    kr = [rope(zk[:, sl]) for sl in slabs]

    @pl.when(i >= tail_from)
    def _():
        if tail_transposed:
            kf_ref[...] = jnp.concatenate(kr, axis=-1).T
            vf_ref[...] = zv.T
        else:
            for s, sl in enumerate(slabs):
                kf_ref[:, sl] = kr[s]
            vf_ref[...] = zv

    if class_major:
        per_class = rows // N_CLASS
        for s, sl in enumerate(slabs):
            zs_ref[s] = qr[s]
            zs_ref[PAIRS + s] = kr[s]
            zs_ref[2 * PAIRS + s] = zv[:, sl]
        for r in range(N_CLASS):
            for s, sl in enumerate(slabs):
                pick = lambda n: zs_ref[n, pl.ds(r, per_class, stride=N_CLASS), :].astype(BF16)
                q_ref[r, :, sl] = pick(s)
                k_ref[r, :, sl] = pick(PAIRS + s)
                v_ref[r, :, sl] = pick(2 * PAIRS + s)
    else:
        for s, sl in enumerate(slabs):
            q_ref[:, sl] = qr[s].astype(BF16)
            k_ref[:, sl] = kr[s].astype(BF16)
        v_ref[...] = zv.astype(BF16)

    o_u = 3 * ATTN_WIDTH
    u = jax.nn.gelu(jnp.dot(xn, w_ref[:, o_u:o_u + GMLP_WIDTH], preferred_element_type=F32))
    vc = jax.nn.gelu(jnp.dot(xn, w_ref[:, o_u + GMLP_WIDTH:o_u + 2 * GMLP_WIDTH], preferred_element_type=F32))
    mu = jnp.mean(vc, axis=-1, keepdims=True)
    cen = vc - mu
    var = jnp.mean(cen * cen, axis=-1, keepdims=True)
    vn = cen * lax.rsqrt(var + EPS) * lng_ref[...] + lnb_ref[...]
    if emit_vn:
        vn_ref[...] = vn
    if seq_rows:
        nseq = rows // seq_rows
        in_first = lax.broadcasted_iota(jnp.int32, (1, LANES), 1) < HEAD_DIM
        for s in range(GMLP_WIDTH // LANES):
            ga, gb = 2 * s, 2 * s + 1
            mix_ref[0, s] = vn[:, s * LANES:(s + 1) * LANES]
            xs = [mix_ref[0, s, pl.ds(j, nseq, stride=seq_rows), :] for j in range(seq_rows)]
            for r in range(seq_rows):
                acc = jnp.broadcast_to(
                    jnp.where(in_first, bsp_ref[ga * seq_rows + r], bsp_ref[gb * seq_rows + r]), (nseq, LANES))
                for j in range(r + 1):
                    w = jnp.where(in_first, wsp_ref[(ga * seq_rows + r) * seq_rows + j],
                                  wsp_ref[(gb * seq_rows + r) * seq_rows + j])
                    acc = acc + w * xs[j]
                mix_ref[1, s, pl.ds(r, nseq, stride=seq_rows), :] = acc
        mixed = jnp.concatenate([mix_ref[1, s] for s in range(GMLP_WIDTH // LANES)], axis=-1)
        gated_ref[...] = (u * mixed).astype(BF16)
        return
    vnb = vn.astype(BF16)

    row = lax.broadcasted_iota(jnp.int32, (CHUNK, 2 * CHUNK), 0)
    col = lax.broadcasted_iota(jnp.int32, (CHUNK, 2 * CHUNK), 1)
    tril = (col % CHUNK) <= row
    lane = lax.broadcasted_iota(jnp.int32, (CHUNK, LANES), 1)
    first = lane < HEAD_DIM
    zero = jnp.zeros((CHUNK, LANES), BF16)
    for s in range(GMLP_WIDTH // LANES):
        sl = slice(s * LANES, (s + 1) * LANES)
        wp = jnp.where(tril, wsp_ref[s], jnp.zeros((), BF16))
        bias = bsp_ref[:, sl]
        for c in range(rows // CHUNK):
            rs = slice(c * CHUNK, (c + 1) * CHUNK)
            vblk = vnb[rs, sl]
            rhs = jnp.concatenate([jnp.where(first, vblk, zero), jnp.where(first, zero, vblk)], axis=0)
            mixed = jnp.dot(wp, rhs, preferred_element_type=F32) + bias
            gated_ref[rs, sl] = (u[rs, sl] * mixed).astype(BF16)


def _project(x, g1, w_in_b, ln_g, ln_b, wsp, bsp, *, rows, tail_rows, pos_base, pos_period, class_major, emit_vn,
             seq_rows=None, tail_transposed=False):
    n_rows = x.shape[0]
    grid = n_rows // rows
    tail_from = (n_rows - tail_rows) // rows
    row_spec = lambda width: pl.BlockSpec((rows, width), lambda i: (i, 0))
    if tail_transposed:
        tail_shape = jax.ShapeDtypeStruct((ATTN_WIDTH, tail_rows), F32)
        tail_spec = pl.BlockSpec((ATTN_WIDTH, rows), lambda i: (0, jnp.maximum(i - tail_from, 0)))
    else:
        tail_shape = jax.ShapeDtypeStruct((tail_rows, ATTN_WIDTH), F32)
        tail_spec = pl.BlockSpec((rows, ATTN_WIDTH), lambda i: (jnp.maximum(i - tail_from, 0), 0))
    if class_major:
        qkv_shape = jax.ShapeDtypeStruct((N_CLASS, n_rows // N_CLASS, ATTN_WIDTH), BF16)
        qkv_spec = pl.BlockSpec((N_CLASS, rows // N_CLASS, ATTN_WIDTH), lambda i: (0, i, 0))
    else:
        qkv_shape = jax.ShapeDtypeStruct((n_rows, ATTN_WIDTH), BF16)
        qkv_spec = row_spec(ATTN_WIDTH)
    out_shape = [
        qkv_shape,
        qkv_shape,
        qkv_shape,
        tail_shape,
        tail_shape,
        jax.ShapeDtypeStruct((n_rows, GMLP_WIDTH), BF16),
    ]
    out_specs = [qkv_spec] * 3 + [tail_spec] * 2 + [row_spec(GMLP_WIDTH)]
    if emit_vn:
        out_shape.append(jax.ShapeDtypeStruct((n_rows, GMLP_WIDTH), F32))
        out_specs.append(row_spec(GMLP_WIDTH))
    scratch = [pltpu.VMEM((rows, LANES), F32), pltpu.VMEM((rows, LANES), F32)]
    if class_major:
        scratch.append(pltpu.VMEM((3 * PAIRS, rows, LANES), F32))
    if seq_rows:
        scratch.append(pltpu.VMEM((2, GMLP_WIDTH // LANES, rows, LANES), F32))
        spatial_specs = [pl.BlockSpec(memory_space=pltpu.SMEM)] * 2
    else:
        spatial_specs = [_const_spec(wsp.shape), _const_spec(bsp.shape)]
    inv_freq = ROPE_THETA ** (-jnp.arange(0, ROT_DIM, 2, dtype=F32) / ROT_DIM)
    per_head = jnp.concatenate([inv_freq, inv_freq, jnp.zeros((HEAD_DIM - ROT_DIM,), F32)])
    invf = jnp.tile(per_head, LANES // HEAD_DIM).reshape(1, LANES)
    return pl.pallas_call(
        functools.partial(_project_kernel, rows=rows, tail_from=tail_from, tail_transposed=tail_transposed,
                          pos_base=float(pos_base),
                          pos_period=pos_period, class_major=class_major, emit_vn=emit_vn, seq_rows=seq_rows),
        grid=(grid,),
        in_specs=[
            row_spec(D_MODEL),
            _const_spec((1, D_MODEL)),
            _const_spec(w_in_b.shape),
            _const_spec((1, LANES)),
            _const_spec((1, GMLP_WIDTH)),
            _const_spec((1, GMLP_WIDTH)),
            *spatial_specs,
        ],
        out_specs=out_specs,
        out_shape=out_shape,
        scratch_shapes=scratch,
        compiler_params=pltpu.CompilerParams(
            dimension_semantics=("arbitrary",), vmem_limit_bytes=VMEM_LIMIT_BYTES),
        name="project",
    )(x, g1, w_in_b, invf, ln_g, ln_b, wsp, bsp)


def _prompt_attn_kernel(q_ref, k_ref, v_ref, o_ref, kc_ref, vc_ref, q32_ref, acc_ref, m_ref, l_ref,
                        s_ref, p_ref, ms_ref):
    i = pl.program_id(1)
    rows = q_ref.shape[1]
    width = q_ref.shape[2]
    npair = width // LANES
    half = SUBLANES

    @pl.when((pl.program_id(0) == 0) & (i == 0))
    def _():
        p_ref[1] = jnp.zeros(p_ref.shape[1:], BF16)
        ms_ref[1] = jnp.zeros(ms_ref.shape[1:], F32)

    @pl.when(i == 0)
    def _():
        kc_ref[0, :, 0:rows, :] = jnp.zeros((N_CLASS, rows, width), BF16)
        vc_ref[0, :, 0:rows, :] = jnp.zeros((N_CLASS, rows, width), BF16)

    @pl.when(i > 0)
    def _():
        kc_ref[0, :, 0:rows, :] = kc_ref[0, :, rows:2 * rows, :]
        vc_ref[0, :, 0:rows, :] = vc_ref[0, :, rows:2 * rows, :]

    kc_ref[0, :, rows:2 * rows, :] = k_ref[...]
    vc_ref[0, :, rows:2 * rows, :] = v_ref[...]
    lo = rows - PACKED_ROWS
    for ref in (kc_ref, vc_ref):
        tail = ref[0, :, lo:2 * rows, :].astype(F32)
        ref[1, :, lo:2 * rows - PACKED_ROWS, :] = tail[:, half:half + rows, :].astype(BF16)
    q32_ref[...] = q_ref[...].astype(F32)
    has_prev = i > 0

    def lanes(pr):
        return slice(pr * LANES, (pr + 1) * LANES)

    mq, nk = rows, 2 * rows
    olane = lax.broadcasted_iota(jnp.int32, (mq, LANES), 1) < HEAD_DIM
    qzero = jnp.zeros((mq, LANES), BF16)
    ones_v = jnp.ones((nk, LANES), BF16)
    nt = (((1,), (1,)), ((), ()))

    def run(n_units, q_of, k_of, v_of, mask_of, state_of, put, first, last):
        units = lambda t: [t * ATTN_GROUP + j for j in range(ATTN_GROUP)]

        def scores(t, slot):
            for j, u in enumerate(units(t)):
                for pr in range(npair):
                    q = q_of(u, pr)
                    qq = jnp.concatenate([jnp.where(olane, q, qzero), jnp.where(olane, qzero, q)], axis=0)
                    s_ref[slot, j, pr] = lax.dot_general(qq, k_of(u, pr), nt, preferred_element_type=F32)

        def softmax(t, slot):
            for j, u in enumerate(units(t)):
                mask = mask_of(u)
                for pr in range(npair):
                    ms = []
                    for hh in range(2):
                        hs = slice(hh * mq, (hh + 1) * mq)
                        s = jnp.where(mask, s_ref[slot, j, pr, hs, :], NEG)
                        m = jnp.max(s, axis=-1, keepdims=True)
                        p_ref[slot, j, pr, hs, :] = jnp.exp2(s - m).astype(BF16)
                        ms.append(m)
                    ms_ref[slot, j, pr] = jnp.where(olane, ms[0], ms[1])

        def values(t, slot, valid):
            for j, u in enumerate(units(t)):
                for pr in range(npair):
                    vv = jnp.concatenate([v_of(u, pr), ones_v], axis=1)
                    r = jnp.dot(p_ref[slot, j, pr], vv, preferred_element_type=F32)
                    pv = jnp.where(olane, r[0:mq, 0:LANES], r[mq:2 * mq, 0:LANES])
                    l = jnp.where(olane, r[0:mq, LANES:2 * LANES], r[mq:2 * mq, LANES:2 * LANES])
                    m = ms_ref[slot, j, pr]
                    if not first:
                        acc0, m0, l0 = state_of(u, pr)
                        m_new = jnp.maximum(m0, m)
                        a = jnp.exp2(m0 - m_new)
                        b = jnp.exp2(m - m_new)
                        pv = acc0 * a + pv * b
                        l = l0 * a + l * b
                        m = m_new
                        if last:
                            pv = pv / l
                        if valid is not None:
                            pv, m, l = jnp.where(valid, pv, acc0), jnp.where(valid, m, m0), jnp.where(valid, l, l0)
                    put(u, pr, pv if last else (pv, m, l))

        n_trips = n_units // ATTN_GROUP
        scores(jnp.int32(0), 0)

        def trip(t, carry):
            slot = t % 2
            values(jnp.maximum(t - 1, 0), 1 - slot, t > 0)
            softmax(t, slot)
            scores(jnp.minimum(t + 1, n_trips - 1), 1 - slot)
            return carry

        lax.fori_loop(0, n_trips, trip, 0)
        values(jnp.int32(n_trips - 1), (n_trips - 1) % 2, None)

    def put_state(slabs_of, r0_of, n):
        def put(u, pr, res):
            for a, sb in enumerate(slabs_of(u)):
                rs = slice(a * n, (a + 1) * n)
                acc_ref[sb, pl.ds(r0_of(u), n), lanes(pr)] = res[0][rs]
                m_ref[sb, pl.ds(r0_of(u), n), lanes(pr)] = res[1][rs]
                l_ref[sb, pl.ds(r0_of(u), n), lanes(pr)] = res[2][rs]
        return put

    def gather(ref, slabs, r0, n, pr):
        return jnp.concatenate([ref[sb, pl.ds(r0, n), lanes(pr)] for sb in slabs], axis=0)

    state_refs = (acc_ref, m_ref, l_ref)
    kplain, vplain = kc_ref.at[0], vc_ref.at[0]

    qi = lax.broadcasted_iota(jnp.int32, (mq, nk), 0)
    kj = lax.broadcasted_iota(jnp.int32, (mq, nk), 1)
    diff = qi + rows - kj
    mask16 = (diff >= 0) & (diff <= SPAN) & ((kj >= rows) | has_prev)
    run(N_CLASS,
        q_of=lambda u, pr: q_ref[u, :, lanes(pr)],
        k_of=lambda u, pr: kplain[u, :, lanes(pr)],
        v_of=lambda u, pr: vplain[u, :, lanes(pr)],
        mask_of=lambda u: mask16, state_of=None,
        put=put_state(lambda u: [u], lambda u: 0, rows), first=True, last=False)

    n4 = N_CLASS // 4
    qb = rows // n4
    d4 = 4 * (qi % qb - kj % (2 * qb) + qb) + (qi // qb - kj // (2 * qb))
    band4 = (d4 >= 0) & (d4 <= SPAN)
    cur4 = kj % (2 * qb) >= qb
    slabs4 = lambda u: [u // n4 + 4 * a for a in range(n4)]
    q0_4 = lambda u: pl.multiple_of((u % n4) * qb, qb)
    k0_4 = lambda u: pl.multiple_of(rows - qb + (u % n4) * qb, qb)
    run(N_CLASS,
        q_of=lambda u, pr: gather(q_ref, slabs4(u), q0_4(u), qb, pr),
        k_of=lambda u, pr: gather(kplain, slabs4(u), k0_4(u), 2 * qb, pr),
        v_of=lambda u, pr: gather(vplain, slabs4(u), k0_4(u), 2 * qb, pr),
        mask_of=lambda u: band4 & (cur4 | has_prev | (u % n4 > 0)),
        state_of=lambda u, pr: tuple(gather(ref, slabs4(u), q0_4(u), qb, pr) for ref in state_refs),
        put=put_state(slabs4, q0_4, qb), first=False, last=False)

    d1 = N_CLASS * (qi % half - kj % (2 * half) + half) + (qi // half - kj // (2 * half))
    band1 = (d1 >= 0) & (d1 <= SPAN)
    cur1 = kj % (2 * half) >= half
    every = list(range(N_CLASS))
    q0_1 = lambda u: pl.multiple_of(u * half, half)
    kcopy = lambda u: 1 - u % 2
    k0_1 = lambda u: pl.multiple_of(((rows - half + u * half) // PACKED_ROWS) * PACKED_ROWS, PACKED_ROWS)

    def put_out(u, pr, res):
        for sb in every:
            acc_ref[sb, pl.ds(q0_1(u), half), lanes(pr)] = res[sb * half:(sb + 1) * half]

    run(rows // half,
        q_of=lambda u, pr: gather(q32_ref, every, q0_1(u), half, pr).astype(BF16),
        k_of=lambda u, pr: gather(kc_ref.at[kcopy(u)], every, k0_1(u), 2 * half, pr),
        v_of=lambda u, pr: gather(vc_ref.at[kcopy(u)], every, k0_1(u), 2 * half, pr),
        mask_of=lambda u: band1 & (cur1 | has_prev | (u > 0)),
        state_of=lambda u, pr: tuple(gather(ref, every, q0_1(u), half, pr) for ref in state_refs),
        put=put_out, first=False, last=True)
    o_ref[...] = acc_ref[...].astype(BF16)


def _prompt_attention(q_cm, k_cm, v_cm):
    n_slab_rows = q_cm.shape[1]
    npair = ATTN_STEP_LANES // LANES
    spec = pl.BlockSpec((N_CLASS, SPAN, ATTN_STEP_LANES), lambda h, i: (0, i, h))
    return pl.pallas_call(
        _prompt_attn_kernel,
        grid=(ATTN_WIDTH // ATTN_STEP_LANES, n_slab_rows // SPAN),
        in_specs=[spec, spec, spec],
        out_specs=spec,
        out_shape=jax.ShapeDtypeStruct(q_cm.shape, BF16),
        scratch_shapes=[
            pltpu.VMEM((2, N_CLASS, 2 * SPAN, ATTN_STEP_LANES), BF16),
            pltpu.VMEM((2, N_CLASS, 2 * SPAN, ATTN_STEP_LANES), BF16),
            pltpu.VMEM((N_CLASS, SPAN, ATTN_STEP_LANES), F32),
            pltpu.VMEM((N_CLASS, SPAN, ATTN_STEP_LANES), F32),
            pltpu.VMEM((N_CLASS, SPAN, ATTN_STEP_LANES), F32),
            pltpu.VMEM((N_CLASS, SPAN, ATTN_STEP_LANES), F32),
            pltpu.VMEM((2, ATTN_GROUP, npair, 2 * SPAN, 2 * SPAN), F32),
            pltpu.VMEM((2, ATTN_GROUP, npair, 2 * SPAN, 2 * SPAN), BF16),
            pltpu.VMEM((2, ATTN_GROUP, npair, SPAN, LANES), F32),
        ],
        compiler_params=pltpu.CompilerParams(
            dimension_semantics=("arbitrary", "arbitrary"), vmem_limit_bytes=VMEM_LIMIT_BYTES),
        name="prompt_attention",
    )(q_cm, k_cm, v_cm)


def _sample_attn_kernel(q_ref, kn_ref, vn_ref, kt_ref, vt_ref, o_ref, *, t_new, lbuf):
    width = ATTN_WIDTH
    nrow = t_new * N_HEADS
    q = q_ref[...]
    kn = kn_ref[...]
    vnew = vn_ref[...]

    row = lax.broadcasted_iota(jnp.int32, (nrow, width), 0)
    lane = lax.broadcasted_iota(jnp.int32, (nrow, width), 1)
    hmask = (lane // HEAD_DIM) == (row % N_HEADS)
    qtok = jnp.concatenate([jnp.broadcast_to(q[j:j + 1, :], (N_HEADS, width)) for j in range(t_new)], axis=0)
    qrows = jnp.where(hmask, qtok, 0.0)

    kt = kt_ref[...].reshape(width, lbuf).astype(BF16)
    vt = vt_ref[...].reshape(width, lbuf).astype(BF16)
    s = jnp.dot(qrows.astype(BF16), kt, preferred_element_type=F32)

    def reach_count(delta):
        cnt = jnp.zeros(delta.shape, F32)
        for window, dil in DILATIONS:
            hit = (delta >= 0) & (delta % dil == 0) & (delta <= window)
            cnt = cnt + jnp.where(hit, 1.0, 0.0)
        return cnt

    pos = lax.broadcasted_iota(jnp.int32, (nrow, lbuf), 1)
    tok = lax.broadcasted_iota(jnp.int32, (nrow, lbuf), 0) // N_HEADS
    cnt = reach_count(lbuf + tok - pos)
    tok1 = tok[:, 0:1]
    cnew = [reach_count(tok1 - j) for j in range(t_new)]
    snew = [jnp.sum(qrows * kn[j:j + 1, :], axis=-1, keepdims=True) for j in range(t_new)]

    s = jnp.where(cnt > 0.0, s, NEG)
    m = jnp.max(s, axis=-1, keepdims=True)
    for j in range(t_new):
        m = jnp.maximum(m, jnp.where(cnew[j] > 0.0, snew[j], NEG))
    e = jnp.exp2(s - m) * cnt
    l = jnp.sum(e, axis=-1, keepdims=True)
    o = lax.dot_general(e.astype(BF16), vt, (((1,), (1,)), ((), ())), preferred_element_type=F32)
    for j in range(t_new):
        ej = jnp.where(cnew[j] > 0.0, jnp.exp2(snew[j] - m), 0.0) * cnew[j]
        l = l + ej
        o = o + ej * vnew[j:j + 1, :]
    o = jnp.where(hmask, o / l, 0.0)
    for j in range(t_new):
        o_ref[j:j + 1, :] = jnp.sum(o[j * N_HEADS:(j + 1) * N_HEADS, :], axis=0, keepdims=True)


def _sample_attention(q, k_new, v_new, cache_kt, cache_vt):
    nb, t_new, width = q.shape
    lbuf = cache_kt.shape[-1]
    pad = lambda a: jnp.pad(a, ((0, 0), (0, 8 - t_new), (0, 0)))
    new_spec = pl.BlockSpec((None, 8, width), lambda b: (b, 0, 0))
    cache_spec = pl.BlockSpec((None, N_HEADS, HEAD_DIM, lbuf), lambda b: (b, 0, 0, 0))
    return pl.pallas_call(
        functools.partial(_sample_attn_kernel, t_new=t_new, lbuf=lbuf),
        grid=(nb,),
        in_specs=[new_spec, new_spec, new_spec, cache_spec, cache_spec],
        out_specs=pl.BlockSpec((None, t_new, width), lambda b: (b, 0, 0)),
        out_shape=jax.ShapeDtypeStruct((nb, t_new, width), F32),
        compiler_params=pltpu.CompilerParams(
            dimension_semantics=("arbitrary",), vmem_limit_bytes=VMEM_LIMIT_BYTES),
        name="sample_attention",
    )(pad(q), pad(k_new), pad(v_new), cache_kt, cache_vt)


def _finish_kernel(x_ref, attn_ref, gated_ref, p_ref, wo_ref, g2_ref, wup_ref, wdn_ref,
                   gg_ref, wg_ref, wple_ref, gf_ref, y_ref, *nat_refs, apply_final):
    if nat_refs:
        nat_ref, per_class = nat_refs[0], attn_ref.shape[1]
        for r in range(N_CLASS):
            for s in range(PAIRS):
                nat_ref[s, pl.ds(r, per_class, stride=N_CLASS), :] = attn_ref[r, :, s * LANES:(s + 1) * LANES].astype(F32)
        attn = jnp.concatenate([nat_ref[s] for s in range(PAIRS)], axis=-1).astype(BF16)
    else:
        attn = attn_ref[...]
    mix = jnp.concatenate([attn, gated_ref[...]], axis=-1)
    h = x_ref[...] + jnp.dot(mix, wo_ref[...], preferred_element_type=F32)
    n2 = _rms(h, g2_ref[...]).astype(BF16)
    f = jnp.zeros_like(h)
    for c in range(D_FF // FF_CHUNK):
        cs = slice(c * FF_CHUNK, (c + 1) * FF_CHUNK)
        up = jnp.dot(n2, wup_ref[:, cs], preferred_element_type=F32)
        act = jnp.square(jnp.maximum(up, 0.0)).astype(BF16)
        f = f + jnp.dot(act, wdn_ref[cs, :], preferred_element_type=F32)
    h = h + f
    gate = jax.nn.sigmoid(jnp.dot(_rms(h, gg_ref[...]).astype(BF16), wg_ref[...], preferred_element_type=F32))
    ple = jnp.dot(p_ref[...].astype(BF16), wple_ref[...], preferred_element_type=F32)
    h = h + gate * ple
    y_ref[...] = _rms(h, gf_ref[...]) if apply_final else h


def _finish(x, attn, gated, p, wo, g2, wup, wdn, gg, wg, wple, gf, *, rows, apply_final):
    n_rows = x.shape[0]
    row_spec = lambda width: pl.BlockSpec((rows, width), lambda i: (i, 0))
    if attn.ndim == 3:
        attn_spec = pl.BlockSpec((N_CLASS, rows // N_CLASS, ATTN_WIDTH), lambda i: (0, i, 0))
        scratch = [pltpu.VMEM((PAIRS, rows, LANES), F32)]
    else:
        attn_spec, scratch = row_spec(ATTN_WIDTH), []
    return pl.pallas_call(
        functools.partial(_finish_kernel, apply_final=apply_final),
        grid=(n_rows // rows,),
        in_specs=[
            row_spec(D_MODEL), attn_spec, row_spec(GMLP_WIDTH), row_spec(PLE_DIM),
            _const_spec(wo.shape), _const_spec((1, D_MODEL)), _const_spec(wup.shape), _const_spec(wdn.shape),
            _const_spec((1, D_MODEL)), _const_spec(wg.shape), _const_spec(wple.shape), _const_spec((1, D_MODEL)),
        ],
        out_specs=row_spec(D_MODEL),
        out_shape=jax.ShapeDtypeStruct((n_rows, D_MODEL), F32),
        scratch_shapes=scratch,
        compiler_params=pltpu.CompilerParams(
            dimension_semantics=("arbitrary",), vmem_limit_bytes=VMEM_LIMIT_BYTES),
        name="finish",
    )(x, attn, gated, p, wo, g2, wup, wdn, gg, wg, wple, gf)


def _pair_spatial(w):
    g, l, _ = w.shape
    return w.reshape(g // 2, 2, l, l).transpose(0, 2, 1, 3).reshape(g // 2, l, 2 * l).astype(BF16)


def _bias_lanes(b):
    return jnp.repeat(b.T, GMLP_WIDTH // GMLP_GROUPS, axis=1)


def kernel(x_prompt, x_sample, cache_k, cache_v, p_prompt, p_sample, norm1_g, w_in, ln_v_g, ln_v_b,
           w_spatial, b_spatial, w_out, norm2_g, w_up, w_down, gate_norm_g, w_gate, w_ple, final_g):
    depth = w_in.shape[0]
    nbp, seq, _ = x_prompt.shape
    nbs, t_new, _ = x_sample.shape
    lbuf = cache_k.shape[2]
    keep = min(WINDOW_MAX, seq)
    assert nbp == 1 and seq % (N_CLASS * SPAN) == 0 and (nbs * t_new) % CHUNK == 0 and t_new <= CHUNK
    n_s = nbs * t_new

    row2 = lambda a: a.reshape(1, -1)

    hp = x_prompt.reshape(seq, D_MODEL)
    hs = x_sample.reshape(n_s, D_MODEL)
    nk_p, nv_p, nk_s, nv_s, nvc_s = [], [], [], [], []
    for i in range(depth):
        last = i == depth - 1
        w_in_b = w_in[i].astype(BF16)
        wo_b, wup_b, wdn_b = w_out[i].astype(BF16), w_up[i].astype(BF16), w_down[i].astype(BF16)
        wg_b, wple_b = w_gate[i].astype(BF16), w_ple[i].astype(BF16)
        fin_w = (wo_b, row2(norm2_g[i]), wup_b, wdn_b, row2(gate_norm_g[i]), wg_b, wple_b, row2(final_g))
        ln_g, ln_b = row2(ln_v_g[i]), row2(ln_v_b[i])

        wsp_p = _pair_spatial(w_spatial[i])
        bsp_p = _bias_lanes(b_spatial[i])
        q, k, v, kf, vf, gated = _project(hp, row2(norm1_g[i]), w_in_b, ln_g, ln_b, wsp_p, bsp_p,
                                          rows=512, tail_rows=keep, pos_base=0, pos_period=512,
                                          class_major=True, emit_vn=False, tail_transposed=True)
        attn_cm = _prompt_attention(q, k, v)
        hp = _finish(hp, attn_cm, gated, p_prompt[i].reshape(seq, PLE_DIM), *fin_w, rows=512, apply_final=last)
        from_cm = lambda a: a.reshape(N_HEADS, HEAD_DIM, keep).transpose(2, 0, 1).reshape(nbp, keep, N_HEADS, HEAD_DIM)
        nk_p.append(from_cm(kf))
        nv_p.append(from_cm(vf))

        wsp_s = w_spatial[i][:, :t_new, :t_new].reshape(-1)
        bsp_s = b_spatial[i][:, :t_new].reshape(-1)
        q, k, v, kf, vf, gated, vn = _project(hs, row2(norm1_g[i]), w_in_b, ln_g, ln_b, wsp_s, bsp_s,
                                              rows=n_s, tail_rows=n_s, pos_base=PAST_LEN, pos_period=t_new,
                                              class_major=False, emit_vn=True, seq_rows=t_new)
        r3 = lambda a: a.reshape(nbs, t_new, ATTN_WIDTH)
        attn = _sample_attention(r3(q).astype(F32), r3(kf), r3(vf),
                                 cache_k[i].transpose(0, 2, 3, 1), cache_v[i].transpose(0, 2, 3, 1))
        hs = _finish(hs, attn.reshape(n_s, ATTN_WIDTH).astype(BF16), gated, p_sample[i].reshape(n_s, PLE_DIM),
                     *fin_w, rows=n_s, apply_final=last)
        nk_s.append(kf.reshape(nbs, t_new, N_HEADS, HEAD_DIM))
        nv_s.append(vf.reshape(nbs, t_new, N_HEADS, HEAD_DIM))
        nvc_s.append(vn.reshape(nbs, t_new, GMLP_WIDTH))

    return (hp.reshape(nbp, seq, D_MODEL), hs.reshape(nbs, t_new, D_MODEL),
            jnp.stack(nk_p), jnp.stack(nv_p), jnp.stack(nk_s), jnp.stack(nv_s), jnp.stack(nvc_s))
```

```python
import functools

import jax
import jax.numpy as jnp
from jax import lax
from jax.experimental import pallas as pl
from jax.experimental.pallas import tpu as pltpu

D_MODEL = 1024
N_HEADS = 8
HEAD_DIM = 64
ATTN_WIDTH = N_HEADS * HEAD_DIM
GMLP_GROUPS = 8
GMLP_WIDTH = 512
CHUNK = 128
DILATIONS = ((128, 1), (512, 4), (2048, 16))
WINDOW_MAX = 2048
PAST_LEN = 16384
ROT_DIM = HEAD_DIM // 4
ROPE_THETA = 500000.0
D_FF = 4 * D_MODEL
PLE_DIM = 256
EPS = 1e-6
NEG = -1e30
Q_SCALE = HEAD_DIM ** -0.5 * 1.4426950408889634

LANES = 128
VMEM_LIMIT_BYTES = 56 * 1024 * 1024

N_CLASS = 16
SPAN = 128
PAIRS = ATTN_WIDTH // LANES
ATTN_STEP_LANES = 2 * LANES
ATTN_GROUP = 2
SUBLANES = 8
PACKED_ROWS = 16
FF_CHUNK = 1024
PROMPT_ROWS = 512

BF16 = jnp.bfloat16
F32 = jnp.float32


def _rms(x, g):
    ms = jnp.mean(x * x, axis=-1, keepdims=True)
    return x * lax.rsqrt(ms + EPS) * g


def _const_spec(shape):
    nd = len(shape)
    return pl.BlockSpec(shape, lambda *_: (0,) * nd, pipeline_mode=pl.Buffered(1))


def _project_kernel(x_ref, g1_ref, w_ref, invf_ref, lng_ref, lnb_ref, wsp_ref, bsp_ref, *rest,
                    rows, tail_from, tail_transposed, pos_base, pos_period, class_major, emit_vn, seq_rows, rider):
    rest = list(rest)
    rider_in = [rest.pop(0) for _ in range(5)] if rider else None
    q_ref, k_ref, v_ref, kf_ref, vf_ref, gated_ref = (rest.pop(0) for _ in range(6))
    vn_ref = rest.pop(0) if emit_vn else None
    if rider:
        _sample_attn_kernel(*rider_in, rest.pop(0), t_new=rider[0], lbuf=rider[1])
    cos_off_ref, sin_off_ref = rest.pop(0), rest.pop(0)
    zs_ref = rest.pop(0) if class_major else None
    mix_ref = rest.pop(0) if seq_rows else None
    i = pl.program_id(0)

    @pl.when(i == 0)
    def _():
        off = (lax.broadcasted_iota(jnp.int32, (rows, LANES), 0) % pos_period).astype(F32)
        ang = off * invf_ref[...]
        cos_off_ref[...] = jnp.cos(ang)
        sin_off_ref[...] = jnp.sin(ang)

    tile_stride = rows if pos_period == rows else 0
    base = (i * tile_stride).astype(F32) + pos_base
    base_ang = base * invf_ref[...]
    cb, sb = jnp.cos(base_ang), jnp.sin(base_ang)
    co, so = cos_off_ref[...], sin_off_ref[...]
    cosf = cb * co - sb * so
    sint = sb * co + cb * so
    head_lane = lax.broadcasted_iota(jnp.int32, (1, LANES), 1) % HEAD_DIM
    sina = jnp.where(head_lane < ROT_DIM // 2, -sint, 0.0)
    sinb = jnp.where((head_lane >= ROT_DIM // 2) & (head_lane < ROT_DIM), sint, 0.0)

    def rope(z):
        return z * cosf + pltpu.roll(z, LANES - ROT_DIM // 2, 1) * sina + pltpu.roll(z, ROT_DIM // 2, 1) * sinb

    xn = _rms(x_ref[...], g1_ref[...]).astype(BF16)
    zq = jnp.dot(xn, w_ref[:, 0:ATTN_WIDTH], preferred_element_type=F32)
    zk = jnp.dot(xn, w_ref[:, ATTN_WIDTH:2 * ATTN_WIDTH], preferred_element_type=F32)
    zv = jnp.dot(xn, w_ref[:, 2 * ATTN_WIDTH:3 * ATTN_WIDTH], preferred_element_type=F32)
    slabs = [slice(s * LANES, (s + 1) * LANES) for s in range(PAIRS)]
    qr = [rope(zq[:, sl]) * Q_SCALE for sl in slabs]
    kr = [rope(zk[:, sl]) for sl in slabs]

    @pl.when(i >= tail_from)
    def _():
        if tail_transposed:
            kf_ref[...] = jnp.concatenate(kr, axis=-1).T
            vf_ref[...] = zv.T
        else:
            for s, sl in enumerate(slabs):
                kf_ref[:, sl] = kr[s]
            vf_ref[...] = zv

    if class_major:
        per_class = rows // N_CLASS
        for s, sl in enumerate(slabs):
            zs_ref[s] = qr[s]
            zs_ref[PAIRS + s] = kr[s]
            zs_ref[2 * PAIRS + s] = zv[:, sl]
        for r in range(N_CLASS):
            for s, sl in enumerate(slabs):
                pick = lambda n: zs_ref[n, pl.ds(r, per_class, stride=N_CLASS), :].astype(BF16)
                q_ref[r, :, sl] = pick(s)
                k_ref[r, :, sl] = pick(PAIRS + s)
                v_ref[r, :, sl] = pick(2 * PAIRS + s)
    else:
        for s, sl in enumerate(slabs):
            q_ref[:, sl] = qr[s].astype(BF16)
            k_ref[:, sl] = kr[s].astype(BF16)
        v_ref[...] = zv.astype(BF16)

    o_u = 3 * ATTN_WIDTH
    u = jax.nn.gelu(jnp.dot(xn, w_ref[:, o_u:o_u + GMLP_WIDTH], preferred_element_type=F32))
    vc = jax.nn.gelu(jnp.dot(xn, w_ref[:, o_u + GMLP_WIDTH:o_u + 2 * GMLP_WIDTH], preferred_element_type=F32))
    mu = jnp.mean(vc, axis=-1, keepdims=True)
    cen = vc - mu
    var = jnp.mean(cen * cen, axis=-1, keepdims=True)
    vn = cen * lax.rsqrt(var + EPS) * lng_ref[...] + lnb_ref[...]
    if emit_vn:
        vn_ref[...] = vn
    if seq_rows:
        nseq = rows // seq_rows
        in_first = lax.broadcasted_iota(jnp.int32, (1, LANES), 1) < HEAD_DIM
        for s in range(GMLP_WIDTH // LANES):
            ga, gb = 2 * s, 2 * s + 1
            mix_ref[0, s] = vn[:, s * LANES:(s + 1) * LANES]
            xs = [mix_ref[0, s, pl.ds(j, nseq, stride=seq_rows), :] for j in range(seq_rows)]
            for r in range(seq_rows):
                acc = jnp.broadcast_to(
                    jnp.where(in_first, bsp_ref[ga * seq_rows + r], bsp_ref[gb * seq_rows + r]), (nseq, LANES))
                for j in range(r + 1):
                    w = jnp.where(in_first, wsp_ref[(ga * seq_rows + r) * seq_rows + j],
                                  wsp_ref[(gb * seq_rows + r) * seq_rows + j])
                    acc = acc + w * xs[j]
                mix_ref[1, s, pl.ds(r, nseq, stride=seq_rows), :] = acc
        mixed = jnp.concatenate([mix_ref[1, s] for s in range(GMLP_WIDTH // LANES)], axis=-1)
        gated_ref[...] = (u * mixed).astype(BF16)
        return
    vnb = vn.astype(BF16)

    row = lax.broadcasted_iota(jnp.int32, (CHUNK, 2 * CHUNK), 0)
    col = lax.broadcasted_iota(jnp.int32, (CHUNK, 2 * CHUNK), 1)
    tril = (col % CHUNK) <= row
    lane = lax.broadcasted_iota(jnp.int32, (CHUNK, LANES), 1)
    first = lane < HEAD_DIM
    zero = jnp.zeros((CHUNK, LANES), BF16)
    for s in range(GMLP_WIDTH // LANES):
        sl = slice(s * LANES, (s + 1) * LANES)
        wp = jnp.where(tril, wsp_ref[s], jnp.zeros((), BF16))
        bias = bsp_ref[:, sl]
        for c in range(rows // CHUNK):
            rs = slice(c * CHUNK, (c + 1) * CHUNK)
            vblk = vnb[rs, sl]
            rhs = jnp.concatenate([jnp.where(first, vblk, zero), jnp.where(first, zero, vblk)], axis=0)
            mixed = jnp.dot(wp, rhs, preferred_element_type=F32) + bias
            gated_ref[rs, sl] = (u[rs, sl] * mixed).astype(BF16)


def _project(x, g1, w_in_b, ln_g, ln_b, wsp, bsp, *, rows, tail_rows, pos_base, pos_period, class_major, emit_vn,
             seq_rows=None, tail_transposed=False, sample_attn=None):
    n_rows = x.shape[0]
    grid = n_rows // rows
    tail_from = (n_rows - tail_rows) // rows
    row_spec = lambda width: pl.BlockSpec((rows, width), lambda i: (i, 0))
    if tail_transposed:
        tail_shape = jax.ShapeDtypeStruct((ATTN_WIDTH, tail_rows), F32)
        tail_spec = pl.BlockSpec((ATTN_WIDTH, rows), lambda i: (0, jnp.maximum(i - tail_from, 0)))
    else:
        tail_shape = jax.ShapeDtypeStruct((tail_rows, ATTN_WIDTH), F32)
        tail_spec = pl.BlockSpec((rows, ATTN_WIDTH), lambda i: (jnp.maximum(i - tail_from, 0), 0))
    if class_major:
        qkv_shape = jax.ShapeDtypeStruct((N_CLASS, n_rows // N_CLASS, ATTN_WIDTH), BF16)
        qkv_spec = pl.BlockSpec((N_CLASS, rows // N_CLASS, ATTN_WIDTH), lambda i: (0, i, 0))
    else:
        qkv_shape = jax.ShapeDtypeStruct((n_rows, ATTN_WIDTH), BF16)
        qkv_spec = row_spec(ATTN_WIDTH)
    out_shape = [
        qkv_shape,
        qkv_shape,
        qkv_shape,
        tail_shape,
        tail_shape,
        jax.ShapeDtypeStruct((n_rows, GMLP_WIDTH), BF16),
    ]
    out_specs = [qkv_spec] * 3 + [tail_spec] * 2 + [row_spec(GMLP_WIDTH)]
    if emit_vn:
        out_shape.append(jax.ShapeDtypeStruct((n_rows, GMLP_WIDTH), F32))
        out_specs.append(row_spec(GMLP_WIDTH))
    scratch = [pltpu.VMEM((rows, LANES), F32), pltpu.VMEM((rows, LANES), F32)]
    if class_major:
        scratch.append(pltpu.VMEM((3 * PAIRS, rows, LANES), F32))
    if seq_rows:
        scratch.append(pltpu.VMEM((2, GMLP_WIDTH // LANES, rows, LANES), F32))
        spatial_specs = [pl.BlockSpec(memory_space=pltpu.SMEM)] * 2
    else:
        spatial_specs = [_const_spec(wsp.shape), _const_spec(bsp.shape)]
    rider, rider_specs, rider_args = None, [], []
    if sample_attn is not None:
        rider_args, rider_specs, rider_out_shape, rider_out_spec, rider = _sample_attention_specs(*sample_attn)
        assert rider_args[0].shape[0] == grid
        out_shape.append(rider_out_shape)
        out_specs.append(rider_out_spec)
    inv_freq = ROPE_THETA ** (-jnp.arange(0, ROT_DIM, 2, dtype=F32) / ROT_DIM)
    per_head = jnp.concatenate([inv_freq, inv_freq, jnp.zeros((HEAD_DIM - ROT_DIM,), F32)])
    invf = jnp.tile(per_head, LANES // HEAD_DIM).reshape(1, LANES)
    return pl.pallas_call(
        functools.partial(_project_kernel, rows=rows, tail_from=tail_from, tail_transposed=tail_transposed,
                          pos_base=float(pos_base),
                          pos_period=pos_period, class_major=class_major, emit_vn=emit_vn, seq_rows=seq_rows,
                          rider=rider),
        grid=(grid,),
        in_specs=[
            row_spec(D_MODEL),
            _const_spec((1, D_MODEL)),
            _const_spec(w_in_b.shape),
            _const_spec((1, LANES)),
            _const_spec((1, GMLP_WIDTH)),
            _const_spec((1, GMLP_WIDTH)),
            *spatial_specs,
            *rider_specs,
        ],
        out_specs=out_specs,
        out_shape=out_shape,
        scratch_shapes=scratch,
        compiler_params=pltpu.CompilerParams(
            dimension_semantics=("arbitrary",), vmem_limit_bytes=VMEM_LIMIT_BYTES),
        name="project",
    )(x, g1, w_in_b, invf, ln_g, ln_b, wsp, bsp, *rider_args)


def _prompt_attn_kernel(q_ref, k_ref, v_ref, o_ref, kc_ref, vc_ref, q32_ref, acc_ref, m_ref, l_ref,
                        s_ref, p_ref, ms_ref):
    i = pl.program_id(1)
    rows = q_ref.shape[1]
    width = q_ref.shape[2]
    npair = width // LANES
    half = SUBLANES

    @pl.when((pl.program_id(0) == 0) & (i == 0))
    def _():
        p_ref[1] = jnp.zeros(p_ref.shape[1:], BF16)
        ms_ref[1] = jnp.zeros(ms_ref.shape[1:], F32)

    @pl.when(i == 0)
    def _():
        kc_ref[0, :, 0:rows, :] = jnp.zeros((N_CLASS, rows, width), BF16)
        vc_ref[0, :, 0:rows, :] = jnp.zeros((N_CLASS, rows, width), BF16)

    @pl.when(i > 0)
    def _():
        kc_ref[0, :, 0:rows, :] = kc_ref[0, :, rows:2 * rows, :]
        vc_ref[0, :, 0:rows, :] = vc_ref[0, :, rows:2 * rows, :]

    kc_ref[0, :, rows:2 * rows, :] = k_ref[...]
    vc_ref[0, :, rows:2 * rows, :] = v_ref[...]
    lo = rows - PACKED_ROWS
    for ref in (kc_ref, vc_ref):
        tail = ref[0, :, lo:2 * rows, :].astype(F32)
        ref[1, :, lo:2 * rows - PACKED_ROWS, :] = tail[:, half:half + rows, :].astype(BF16)
    q32_ref[...] = q_ref[...].astype(F32)
    has_prev = i > 0

    def lanes(pr):
        return slice(pr * LANES, (pr + 1) * LANES)

    mq, nk = rows, 2 * rows
    olane = lax.broadcasted_iota(jnp.int32, (mq, LANES), 1) < HEAD_DIM
    qzero = jnp.zeros((mq, LANES), BF16)
    ones_v = jnp.ones((nk, LANES), BF16)
    nt = (((1,), (1,)), ((), ()))

    def run(n_units, q_of, k_of, v_of, mask_of, state_of, put, first, last):
        units = lambda t: [t * ATTN_GROUP + j for j in range(ATTN_GROUP)]

        def scores(t, slot):
            for j, u in enumerate(units(t)):
                for pr in range(npair):
                    q = q_of(u, pr)
                    qq = jnp.concatenate([jnp.where(olane, q, qzero), jnp.where(olane, qzero, q)], axis=0)
                    s_ref[slot, j, pr] = lax.dot_general(qq, k_of(u, pr), nt, preferred_element_type=F32)

        def softmax(t, slot):
            for j, u in enumerate(units(t)):
                mask = mask_of(u)
                for pr in range(npair):
                    ms = []
                    for hh in range(2):
                        hs = slice(hh * mq, (hh + 1) * mq)
                        s = jnp.where(mask, s_ref[slot, j, pr, hs, :], NEG)
                        m = jnp.max(s, axis=-1, keepdims=True)
                        p_ref[slot, j, pr, hs, :] = jnp.exp2(s - m).astype(BF16)
                        ms.append(m)
                    ms_ref[slot, j, pr] = jnp.where(olane, ms[0], ms[1])

        def values(t, slot, valid):
            for j, u in enumerate(units(t)):
                for pr in range(npair):
                    vv = jnp.concatenate([v_of(u, pr), ones_v], axis=1)
                    r = jnp.dot(p_ref[slot, j, pr], vv, preferred_element_type=F32)
                    pv = jnp.where(olane, r[0:mq, 0:LANES], r[mq:2 * mq, 0:LANES])
                    l = jnp.where(olane, r[0:mq, LANES:2 * LANES], r[mq:2 * mq, LANES:2 * LANES])
                    m = ms_ref[slot, j, pr]
                    if not first:
                        acc0, m0, l0 = state_of(u, pr)
                        m_new = jnp.maximum(m0, m)
                        a = jnp.exp2(m0 - m_new)
                        b = jnp.exp2(m - m_new)
                        pv = acc0 * a + pv * b
                        l = l0 * a + l * b
                        m = m_new
                        if last:
                            pv = pv / l
                        if valid is not None:
                            pv, m, l = jnp.where(valid, pv, acc0), jnp.where(valid, m, m0), jnp.where(valid, l, l0)
                    put(u, pr, pv if last else (pv, m, l))

        n_trips = n_units // ATTN_GROUP
        scores(jnp.int32(0), 0)

        def trip(t, carry):
            slot = t % 2
            values(jnp.maximum(t - 1, 0), 1 - slot, t > 0)
            softmax(t, slot)
            scores(jnp.minimum(t + 1, n_trips - 1), 1 - slot)
            return carry

        lax.fori_loop(0, n_trips, trip, 0)
        values(jnp.int32(n_trips - 1), (n_trips - 1) % 2, None)

    def put_state(slabs_of, r0_of, n):
        def put(u, pr, res):
            for a, sb in enumerate(slabs_of(u)):
                rs = slice(a * n, (a + 1) * n)
                acc_ref[sb, pl.ds(r0_of(u), n), lanes(pr)] = res[0][rs]
                m_ref[sb, pl.ds(r0_of(u), n), lanes(pr)] = res[1][rs]
                l_ref[sb, pl.ds(r0_of(u), n), lanes(pr)] = res[2][rs]
        return put

    def gather(ref, slabs, r0, n, pr):
        return jnp.concatenate([ref[sb, pl.ds(r0, n), lanes(pr)] for sb in slabs], axis=0)

    state_refs = (acc_ref, m_ref, l_ref)
    kplain, vplain = kc_ref.at[0], vc_ref.at[0]

    qi = lax.broadcasted_iota(jnp.int32, (mq, nk), 0)
    kj = lax.broadcasted_iota(jnp.int32, (mq, nk), 1)
    diff = qi + rows - kj
    mask16 = (diff >= 0) & (diff <= SPAN) & ((kj >= rows) | has_prev)
    run(N_CLASS,
        q_of=lambda u, pr: q_ref[u, :, lanes(pr)],
        k_of=lambda u, pr: kplain[u, :, lanes(pr)],
        v_of=lambda u, pr: vplain[u, :, lanes(pr)],
        mask_of=lambda u: mask16, state_of=None,
        put=put_state(lambda u: [u], lambda u: 0, rows), first=True, last=False)

    n4 = N_CLASS // 4
    qb = rows // n4
    d4 = 4 * (qi % qb - kj % (2 * qb) + qb) + (qi // qb - kj // (2 * qb))
    band4 = (d4 >= 0) & (d4 <= SPAN)
    cur4 = kj % (2 * qb) >= qb
    slabs4 = lambda u: [u // n4 + 4 * a for a in range(n4)]
    q0_4 = lambda u: pl.multiple_of((u % n4) * qb, qb)
    k0_4 = lambda u: pl.multiple_of(rows - qb + (u % n4) * qb, qb)
    run(N_CLASS,
        q_of=lambda u, pr: gather(q_ref, slabs4(u), q0_4(u), qb, pr),
        k_of=lambda u, pr: gather(kplain, slabs4(u), k0_4(u), 2 * qb, pr),
        v_of=lambda u, pr: gather(vplain, slabs4(u), k0_4(u), 2 * qb, pr),
        mask_of=lambda u: band4 & (cur4 | has_prev | (u % n4 > 0)),
        state_of=lambda u, pr: tuple(gather(ref, slabs4(u), q0_4(u), qb, pr) for ref in state_refs),
        put=put_state(slabs4, q0_4, qb), first=False, last=False)

    d1 = N_CLASS * (qi % half - kj % (2 * half) + half) + (qi // half - kj // (2 * half))
    band1 = (d1 >= 0) & (d1 <= SPAN)
    cur1 = kj % (2 * half) >= half
    every = list(range(N_CLASS))
    q0_1 = lambda u: pl.multiple_of(u * half, half)
    kcopy = lambda u: 1 - u % 2
    k0_1 = lambda u: pl.multiple_of(((rows - half + u * half) // PACKED_ROWS) * PACKED_ROWS, PACKED_ROWS)

    def put_out(u, pr, res):
        for sb in every:
            acc_ref[sb, pl.ds(q0_1(u), half), lanes(pr)] = res[sb * half:(sb + 1) * half]

    run(rows // half,
        q_of=lambda u, pr: gather(q32_ref, every, q0_1(u), half, pr).astype(BF16),
        k_of=lambda u, pr: gather(kc_ref.at[kcopy(u)], every, k0_1(u), 2 * half, pr),
        v_of=lambda u, pr: gather(vc_ref.at[kcopy(u)], every, k0_1(u), 2 * half, pr),
        mask_of=lambda u: band1 & (cur1 | has_prev | (u > 0)),
        state_of=lambda u, pr: tuple(gather(ref, every, q0_1(u), half, pr) for ref in state_refs),
        put=put_out, first=False, last=True)
    o_ref[...] = acc_ref[...].astype(BF16)


def _prompt_attention(q_cm, k_cm, v_cm):
    n_slab_rows = q_cm.shape[1]
    npair = ATTN_STEP_LANES // LANES
    spec = pl.BlockSpec((N_CLASS, SPAN, ATTN_STEP_LANES), lambda h, i: (0, i, h))
    return pl.pallas_call(
        _prompt_attn_kernel,
        grid=(ATTN_WIDTH // ATTN_STEP_LANES, n_slab_rows // SPAN),
        in_specs=[spec, spec, spec],
        out_specs=spec,
        out_shape=jax.ShapeDtypeStruct(q_cm.shape, BF16),
        scratch_shapes=[
            pltpu.VMEM((2, N_CLASS, 2 * SPAN, ATTN_STEP_LANES), BF16),
            pltpu.VMEM((2, N_CLASS, 2 * SPAN, ATTN_STEP_LANES), BF16),
            pltpu.VMEM((N_CLASS, SPAN, ATTN_STEP_LANES), F32),
            pltpu.VMEM((N_CLASS, SPAN, ATTN_STEP_LANES), F32),
            pltpu.VMEM((N_CLASS, SPAN, ATTN_STEP_LANES), F32),
            pltpu.VMEM((N_CLASS, SPAN, ATTN_STEP_LANES), F32),
            pltpu.VMEM((2, ATTN_GROUP, npair, 2 * SPAN, 2 * SPAN), F32),
            pltpu.VMEM((2, ATTN_GROUP, npair, 2 * SPAN, 2 * SPAN), BF16),
            pltpu.VMEM((2, ATTN_GROUP, npair, SPAN, LANES), F32),
        ],
        compiler_params=pltpu.CompilerParams(
            dimension_semantics=("arbitrary", "arbitrary"), vmem_limit_bytes=VMEM_LIMIT_BYTES),
        name="prompt_attention",
    )(q_cm, k_cm, v_cm)


def _sample_attn_kernel(q_ref, kn_ref, vn_ref, kt_ref, vt_ref, o_ref, *, t_new, lbuf):
    width = ATTN_WIDTH
    nrow = t_new * N_HEADS
    q = q_ref[...]
    kn = kn_ref[...]
    vnew = vn_ref[...]

    row = lax.broadcasted_iota(jnp.int32, (nrow, width), 0)
    lane = lax.broadcasted_iota(jnp.int32, (nrow, width), 1)
    hmask = (lane // HEAD_DIM) == (row % N_HEADS)
    qtok = jnp.concatenate([jnp.broadcast_to(q[j:j + 1, :], (N_HEADS, width)) for j in range(t_new)], axis=0)
    qrows = jnp.where(hmask, qtok, 0.0)

    kt = kt_ref[...].reshape(width, lbuf).astype(BF16)
    vt = vt_ref[...].reshape(width, lbuf).astype(BF16)
    s = jnp.dot(qrows.astype(BF16), kt, preferred_element_type=F32)

    def reach_count(delta):
        cnt = jnp.zeros(delta.shape, F32)
        for window, dil in DILATIONS:
            hit = (delta >= 0) & (delta % dil == 0) & (delta <= window)
            cnt = cnt + jnp.where(hit, 1.0, 0.0)
        return cnt

    pos = lax.broadcasted_iota(jnp.int32, (nrow, lbuf), 1)
    tok = lax.broadcasted_iota(jnp.int32, (nrow, lbuf), 0) // N_HEADS
    cnt = reach_count(lbuf + tok - pos)
    tok1 = tok[:, 0:1]
    cnew = [reach_count(tok1 - j) for j in range(t_new)]
    snew = [jnp.sum(qrows * kn[j:j + 1, :], axis=-1, keepdims=True) for j in range(t_new)]

    s = jnp.where(cnt > 0.0, s, NEG)
    m = jnp.max(s, axis=-1, keepdims=True)
    for j in range(t_new):
        m = jnp.maximum(m, jnp.where(cnew[j] > 0.0, snew[j], NEG))
    e = jnp.exp2(s - m) * cnt
    l = jnp.sum(e, axis=-1, keepdims=True)
    o = lax.dot_general(e.astype(BF16), vt, (((1,), (1,)), ((), ())), preferred_element_type=F32)
    for j in range(t_new):
        ej = jnp.where(cnew[j] > 0.0, jnp.exp2(snew[j] - m), 0.0) * cnew[j]
        l = l + ej
        o = o + ej * vnew[j:j + 1, :]
    o = jnp.where(hmask, o / l, 0.0)
    for j in range(t_new):
        o_ref[j:j + 1, :] = jnp.sum(o[j * N_HEADS:(j + 1) * N_HEADS, :], axis=0, keepdims=True)


def _sample_attention_specs(q, k_new, v_new, cache_kt, cache_vt):
    nb, t_new, width = q.shape
    lbuf = cache_kt.shape[-1]
    pad = lambda a: jnp.pad(a, ((0, 0), (0, SUBLANES - t_new), (0, 0)))
    new_spec = pl.BlockSpec((None, SUBLANES, width), lambda b: (b, 0, 0))
    cache_spec = pl.BlockSpec((None, N_HEADS, HEAD_DIM, lbuf), lambda b: (b, 0, 0, 0))
    return ([pad(q), pad(k_new), pad(v_new), cache_kt, cache_vt],
            [new_spec, new_spec, new_spec, cache_spec, cache_spec],
            jax.ShapeDtypeStruct((nb, t_new, width), F32),
            pl.BlockSpec((None, t_new, width), lambda b: (b, 0, 0)),
            (t_new, lbuf))


def _sample_attention(q, k_new, v_new, cache_kt, cache_vt):
    args, in_specs, out_shape, out_spec, (t_new, lbuf) = _sample_attention_specs(q, k_new, v_new, cache_kt, cache_vt)
    return pl.pallas_call(
        functools.partial(_sample_attn_kernel, t_new=t_new, lbuf=lbuf),
        grid=(q.shape[0],),
        in_specs=in_specs,
        out_specs=out_spec,
        out_shape=out_shape,
        compiler_params=pltpu.CompilerParams(
            dimension_semantics=("arbitrary",), vmem_limit_bytes=VMEM_LIMIT_BYTES),
        name="sample_attention",
    )(*args)


def _finish_kernel(x_ref, attn_ref, gated_ref, p_ref, wo_ref, g2_ref, wup_ref, wdn_ref,
                   gg_ref, wg_ref, wple_ref, gf_ref, y_ref, *nat_refs, apply_final):
    if nat_refs:
        nat_ref, per_class = nat_refs[0], attn_ref.shape[1]
        for r in range(N_CLASS):
            for s in range(PAIRS):
                nat_ref[s, pl.ds(r, per_class, stride=N_CLASS), :] = attn_ref[r, :, s * LANES:(s + 1) * LANES].astype(F32)
        attn = jnp.concatenate([nat_ref[s] for s in range(PAIRS)], axis=-1).astype(BF16)
    else:
        attn = attn_ref[...]
    mix = jnp.concatenate([attn, gated_ref[...]], axis=-1)
    h = x_ref[...] + jnp.dot(mix, wo_ref[...], preferred_element_type=F32)
    n2 = _rms(h, g2_ref[...]).astype(BF16)
    f = jnp.zeros_like(h)
    for c in range(D_FF // FF_CHUNK):
        cs = slice(c * FF_CHUNK, (c + 1) * FF_CHUNK)
        up = jnp.dot(n2, wup_ref[:, cs], preferred_element_type=F32)
        act = jnp.square(jnp.maximum(up, 0.0)).astype(BF16)
        f = f + jnp.dot(act, wdn_ref[cs, :], preferred_element_type=F32)
    h = h + f
    gate = jax.nn.sigmoid(jnp.dot(_rms(h, gg_ref[...]).astype(BF16), wg_ref[...], preferred_element_type=F32))
    ple = jnp.dot(p_ref[...].astype(BF16), wple_ref[...], preferred_element_type=F32)
    h = h + gate * ple
    y_ref[...] = _rms(h, gf_ref[...]) if apply_final else h


def _finish(x, attn, gated, p, wo, g2, wup, wdn, gg, wg, wple, gf, *, rows, apply_final):
    n_rows = x.shape[0]
    row_spec = lambda width: pl.BlockSpec((rows, width), lambda i: (i, 0))
    if attn.ndim == 3:
        attn_spec = pl.BlockSpec((N_CLASS, rows // N_CLASS, ATTN_WIDTH), lambda i: (0, i, 0))
        scratch = [pltpu.VMEM((PAIRS, rows, LANES), F32)]
    else:
        attn_spec, scratch = row_spec(ATTN_WIDTH), []
    return pl.pallas_call(
        functools.partial(_finish_kernel, apply_final=apply_final),
        grid=(n_rows // rows,),
        in_specs=[
            row_spec(D_MODEL), attn_spec, row_spec(GMLP_WIDTH), row_spec(PLE_DIM),
            _const_spec(wo.shape), _const_spec((1, D_MODEL)), _const_spec(wup.shape), _const_spec(wdn.shape),
            _const_spec((1, D_MODEL)), _const_spec(wg.shape), _const_spec(wple.shape), _const_spec((1, D_MODEL)),
        ],
        out_specs=row_spec(D_MODEL),
        out_shape=jax.ShapeDtypeStruct((n_rows, D_MODEL), F32),
        scratch_shapes=scratch,
        compiler_params=pltpu.CompilerParams(
            dimension_semantics=("arbitrary",), vmem_limit_bytes=VMEM_LIMIT_BYTES),
        name="finish",
    )(x, attn, gated, p, wo, g2, wup, wdn, gg, wg, wple, gf)


def _pair_spatial(w):
    g, l, _ = w.shape
    return w.reshape(g // 2, 2, l, l).transpose(0, 2, 1, 3).reshape(g // 2, l, 2 * l).astype(BF16)


def _bias_lanes(b):
    return jnp.repeat(b.T, GMLP_WIDTH // GMLP_GROUPS, axis=1)


def kernel(x_prompt, x_sample, cache_k, cache_v, p_prompt, p_sample, norm1_g, w_in, ln_v_g, ln_v_b,
           w_spatial, b_spatial, w_out, norm2_g, w_up, w_down, gate_norm_g, w_gate, w_ple, final_g):
    depth = w_in.shape[0]
    nbp, seq, _ = x_prompt.shape
    nbs, t_new, _ = x_sample.shape
    lbuf = cache_k.shape[2]
    keep = min(WINDOW_MAX, seq)
    assert nbp == 1 and seq % (N_CLASS * SPAN) == 0 and (nbs * t_new) % CHUNK == 0 and t_new <= CHUNK
    n_s = nbs * t_new

    row2 = lambda a: a.reshape(1, -1)

    hp = x_prompt.reshape(seq, D_MODEL)
    hs = x_sample.reshape(n_s, D_MODEL)
    nk_p, nv_p, nk_s, nv_s, nvc_s = [], [], [], [], []
    for i in range(depth):
        last = i == depth - 1
        w_in_b = w_in[i].astype(BF16)
        wo_b, wup_b, wdn_b = w_out[i].astype(BF16), w_up[i].astype(BF16), w_down[i].astype(BF16)
        wg_b, wple_b = w_gate[i].astype(BF16), w_ple[i].astype(BF16)
        fin_w = (wo_b, row2(norm2_g[i]), wup_b, wdn_b, row2(gate_norm_g[i]), wg_b, wple_b, row2(final_g))
        ln_g, ln_b = row2(ln_v_g[i]), row2(ln_v_b[i])

        wsp_s = w_spatial[i][:, :t_new, :t_new].reshape(-1)
        bsp_s = b_spatial[i][:, :t_new].reshape(-1)
        q_s, _, _, kf_s, vf_s, gated_s, vn = _project(hs, row2(norm1_g[i]), w_in_b, ln_g, ln_b, wsp_s, bsp_s,
                                                      rows=n_s, tail_rows=n_s, pos_base=PAST_LEN, pos_period=t_new,
                                                      class_major=False, emit_vn=True, seq_rows=t_new)
        r3 = lambda a: a.reshape(nbs, t_new, ATTN_WIDTH)
        sample_ops = (r3(q_s).astype(F32), r3(kf_s), r3(vf_s),
                      cache_k[i].transpose(0, 2, 3, 1), cache_v[i].transpose(0, 2, 3, 1))

        ride = seq // PROMPT_ROWS == nbs
        wsp_p = _pair_spatial(w_spatial[i])
        bsp_p = _bias_lanes(b_spatial[i])
        q, k, v, kf, vf, gated, *rode = _project(hp, row2(norm1_g[i]), w_in_b, ln_g, ln_b, wsp_p, bsp_p,
                                                 rows=PROMPT_ROWS, tail_rows=keep, pos_base=0, pos_period=PROMPT_ROWS,
                                                 class_major=True, emit_vn=False, tail_transposed=True,
                                                 sample_attn=sample_ops if ride else None)
        attn_cm = _prompt_attention(q, k, v)
        hp = _finish(hp, attn_cm, gated, p_prompt[i].reshape(seq, PLE_DIM), *fin_w, rows=PROMPT_ROWS, apply_final=last)
        from_cm = lambda a: a.reshape(N_HEADS, HEAD_DIM, keep).transpose(2, 0, 1).reshape(nbp, keep, N_HEADS, HEAD_DIM)
        nk_p.append(from_cm(kf))
        nv_p.append(from_cm(vf))

        attn_s = rode[0] if ride else _sample_attention(*sample_ops)
        hs = _finish(hs, attn_s.reshape(n_s, ATTN_WIDTH).astype(BF16), gated_s, p_sample[i].reshape(n_s, PLE_DIM),
                     *fin_w, rows=n_s, apply_final=last)
        kf, vf = kf_s, vf_s
        nk_s.append(kf.reshape(nbs, t_new, N_HEADS, HEAD_DIM))
        nv_s.append(vf.reshape(nbs, t_new, N_HEADS, HEAD_DIM))
        nvc_s.append(vn.reshape(nbs, t_new, GMLP_WIDTH))

    return (hp.reshape(nbp, seq, D_MODEL), hs.reshape(nbs, t_new, D_MODEL),
            jnp.stack(nk_p), jnp.stack(nv_p), jnp.stack(nk_s), jnp.stack(nv_s), jnp.stack(nvc_s))
```

```python
import functools

import jax
import jax.numpy as jnp
from jax import lax
from jax.experimental import pallas as pl
from jax.experimental.pallas import tpu as pltpu

D_MODEL = 1024
N_HEADS = 8
HEAD_DIM = 64
ATTN_WIDTH = N_HEADS * HEAD_DIM
GMLP_GROUPS = 8
GMLP_WIDTH = 512
CHUNK = 128
DILATIONS = ((128, 1), (512, 4), (2048, 16))
WINDOW_MAX = 2048
PAST_LEN = 16384
ROT_DIM = HEAD_DIM // 4
ROPE_THETA = 500000.0
D_FF = 4 * D_MODEL
PLE_DIM = 256
EPS = 1e-6
NEG = -1e30
Q_SCALE = HEAD_DIM ** -0.5 * 1.4426950408889634

LANES = 128
VMEM_LIMIT_BYTES = 56 * 1024 * 1024

N_CLASS = 16
DEINT = 4
SPAN = 128
PAIRS = ATTN_WIDTH // LANES
ATTN_STEP_LANES = 2 * LANES
ATTN_GROUP = 1
SUBLANES = 8
PACKED_ROWS = 16
FF_CHUNK = 1024
PROMPT_ROWS = 512

BF16 = jnp.bfloat16
F32 = jnp.float32


def _rms(x, g):
    ms = jnp.mean(x * x, axis=-1, keepdims=True)
    return x * lax.rsqrt(ms + EPS) * g


def _const_spec(shape):
    nd = len(shape)
    return pl.BlockSpec(shape, lambda *_: (0,) * nd, pipeline_mode=pl.Buffered(1))


def _project_kernel(x_ref, g1_ref, w_ref, invf_ref, lng_ref, lnb_ref, wsp_ref, bsp_ref, *rest,
                    rows, tail_from, tail_transposed, pos_base, pos_period, class_major, emit_vn, seq_rows, rider):
    rest = list(rest)
    rider_in = [rest.pop(0) for _ in range(5)] if rider else None
    q_ref, k_ref, v_ref, kf_ref, vf_ref, gated_ref = (rest.pop(0) for _ in range(6))
    vn_ref = rest.pop(0) if emit_vn else None
    if rider:
        _sample_attn_kernel(*rider_in, rest.pop(0), t_new=rider[0], lbuf=rider[1])
    cos_off_ref, sin_off_ref = rest.pop(0), rest.pop(0)
    zs_ref, zq_ref = (rest.pop(0), rest.pop(0)) if class_major else (None, None)
    mix_ref = rest.pop(0) if seq_rows else None
    i = pl.program_id(0)

    @pl.when(i == 0)
    def _():
        off = (lax.broadcasted_iota(jnp.int32, (rows, LANES), 0) % pos_period).astype(F32)
        ang = off * invf_ref[...]
        cos_off_ref[...] = jnp.cos(ang)
        sin_off_ref[...] = jnp.sin(ang)

    tile_stride = rows if pos_period == rows else 0
    base = (i * tile_stride).astype(F32) + pos_base
    base_ang = base * invf_ref[...]
    cb, sb = jnp.cos(base_ang), jnp.sin(base_ang)
    co, so = cos_off_ref[...], sin_off_ref[...]
    cosf = cb * co - sb * so
    sint = sb * co + cb * so
    head_lane = lax.broadcasted_iota(jnp.int32, (1, LANES), 1) % HEAD_DIM
    sina = jnp.where(head_lane < ROT_DIM // 2, -sint, 0.0)
    sinb = jnp.where((head_lane >= ROT_DIM // 2) & (head_lane < ROT_DIM), sint, 0.0)

    def rope(z):
        return z * cosf + pltpu.roll(z, LANES - ROT_DIM // 2, 1) * sina + pltpu.roll(z, ROT_DIM // 2, 1) * sinb

    xn = _rms(x_ref[...], g1_ref[...]).astype(BF16)
    zq = jnp.dot(xn, w_ref[:, 0:ATTN_WIDTH], preferred_element_type=F32)
    zk = jnp.dot(xn, w_ref[:, ATTN_WIDTH:2 * ATTN_WIDTH], preferred_element_type=F32)
    zv = jnp.dot(xn, w_ref[:, 2 * ATTN_WIDTH:3 * ATTN_WIDTH], preferred_element_type=F32)
    slabs = [slice(s * LANES, (s + 1) * LANES) for s in range(PAIRS)]
    qr = [rope(zq[:, sl]) * Q_SCALE for sl in slabs]
    kr = [rope(zk[:, sl]) for sl in slabs]

    @pl.when(i >= tail_from)
    def _():
        if tail_transposed:
            kf_ref[...] = jnp.concatenate(kr, axis=-1).T
            vf_ref[...] = zv.T
        else:
            for s, sl in enumerate(slabs):
                kf_ref[:, sl] = kr[s]
            vf_ref[...] = zv

    if class_major:
        per_class = rows // N_CLASS
        for s, sl in enumerate(slabs):
            zs_ref[s] = qr[s]
            zs_ref[PAIRS + s] = kr[s]
            zs_ref[2 * PAIRS + s] = zv[:, sl]
        for n in range(3 * PAIRS):
            for c in range(DEINT):
                zq_ref[n, c] = zs_ref[n, pl.ds(c, rows // DEINT, stride=DEINT), :]
        for r in range(N_CLASS):
            for s, sl in enumerate(slabs):
                pick = lambda n: zq_ref[n, r % DEINT, pl.ds(r // DEINT, per_class, stride=DEINT), :].astype(BF16)
                q_ref[r, :, sl] = pick(s)
                k_ref[r, :, sl] = pick(PAIRS + s)
                v_ref[r, :, sl] = pick(2 * PAIRS + s)
    else:
        for s, sl in enumerate(slabs):
            q_ref[:, sl] = qr[s].astype(BF16)
            k_ref[:, sl] = kr[s].astype(BF16)
        v_ref[...] = zv.astype(BF16)

    o_u = 3 * ATTN_WIDTH
    u = jax.nn.gelu(jnp.dot(xn, w_ref[:, o_u:o_u + GMLP_WIDTH], preferred_element_type=F32))
    vc = jax.nn.gelu(jnp.dot(xn, w_ref[:, o_u + GMLP_WIDTH:o_u + 2 * GMLP_WIDTH], preferred_element_type=F32))
    mu = jnp.mean(vc, axis=-1, keepdims=True)
    cen = vc - mu
    var = jnp.mean(cen * cen, axis=-1, keepdims=True)
    vn = cen * lax.rsqrt(var + EPS) * lng_ref[...] + lnb_ref[...]
    if emit_vn:
        vn_ref[...] = vn
    if seq_rows:
        nseq = rows // seq_rows
        in_first = lax.broadcasted_iota(jnp.int32, (1, LANES), 1) < HEAD_DIM
        for s in range(GMLP_WIDTH // LANES):
            ga, gb = 2 * s, 2 * s + 1
            mix_ref[0, s] = vn[:, s * LANES:(s + 1) * LANES]
            xs = [mix_ref[0, s, pl.ds(j, nseq, stride=seq_rows), :] for j in range(seq_rows)]
            for r in range(seq_rows):
                acc = jnp.broadcast_to(
                    jnp.where(in_first, bsp_ref[ga * seq_rows + r], bsp_ref[gb * seq_rows + r]), (nseq, LANES))
                for j in range(r + 1):
                    w = jnp.where(in_first, wsp_ref[(ga * seq_rows + r) * seq_rows + j],
                                  wsp_ref[(gb * seq_rows + r) * seq_rows + j])
                    acc = acc + w * xs[j]
                mix_ref[1, s, pl.ds(r, nseq, stride=seq_rows), :] = acc
        mixed = jnp.concatenate([mix_ref[1, s] for s in range(GMLP_WIDTH // LANES)], axis=-1)
        gated_ref[...] = (u * mixed).astype(BF16)
        return
    vnb = vn.astype(BF16)

    row = lax.broadcasted_iota(jnp.int32, (CHUNK, 2 * CHUNK), 0)
    col = lax.broadcasted_iota(jnp.int32, (CHUNK, 2 * CHUNK), 1)
    tril = (col % CHUNK) <= row
    lane = lax.broadcasted_iota(jnp.int32, (CHUNK, LANES), 1)
    first = lane < HEAD_DIM
    zero = jnp.zeros((CHUNK, LANES), BF16)
    for s in range(GMLP_WIDTH // LANES):
        sl = slice(s * LANES, (s + 1) * LANES)
        wp = jnp.where(tril, wsp_ref[s], jnp.zeros((), BF16))
        bias = bsp_ref[:, sl]
        for c in range(rows // CHUNK):
            rs = slice(c * CHUNK, (c + 1) * CHUNK)
            vblk = vnb[rs, sl]
            rhs = jnp.concatenate([jnp.where(first, vblk, zero), jnp.where(first, zero, vblk)], axis=0)
            mixed = jnp.dot(wp, rhs, preferred_element_type=F32) + bias
            gated_ref[rs, sl] = (u[rs, sl] * mixed).astype(BF16)


def _project(x, g1, w_in_b, ln_g, ln_b, wsp, bsp, *, rows, tail_rows, pos_base, pos_period, class_major, emit_vn,
             seq_rows=None, tail_transposed=False, sample_attn=None):
    n_rows = x.shape[0]
    grid = n_rows // rows
    tail_from = (n_rows - tail_rows) // rows
    row_spec = lambda width: pl.BlockSpec((rows, width), lambda i: (i, 0))
    if tail_transposed:
        tail_shape = jax.ShapeDtypeStruct((ATTN_WIDTH, tail_rows), F32)
        tail_spec = pl.BlockSpec((ATTN_WIDTH, rows), lambda i: (0, jnp.maximum(i - tail_from, 0)))
    else:
        tail_shape = jax.ShapeDtypeStruct((tail_rows, ATTN_WIDTH), F32)
        tail_spec = pl.BlockSpec((rows, ATTN_WIDTH), lambda i: (jnp.maximum(i - tail_from, 0), 0))
    if class_major:
        qkv_shape = jax.ShapeDtypeStruct((N_CLASS, n_rows // N_CLASS, ATTN_WIDTH), BF16)
        qkv_spec = pl.BlockSpec((N_CLASS, rows // N_CLASS, ATTN_WIDTH), lambda i: (0, i, 0))
    else:
        qkv_shape = jax.ShapeDtypeStruct((n_rows, ATTN_WIDTH), BF16)
        qkv_spec = row_spec(ATTN_WIDTH)
    out_shape = [
        qkv_shape,
        qkv_shape,
        qkv_shape,
        tail_shape,
        tail_shape,
        jax.ShapeDtypeStruct((n_rows, GMLP_WIDTH), BF16),
    ]
    out_specs = [qkv_spec] * 3 + [tail_spec] * 2 + [row_spec(GMLP_WIDTH)]
    if emit_vn:
        out_shape.append(jax.ShapeDtypeStruct((n_rows, GMLP_WIDTH), F32))
        out_specs.append(row_spec(GMLP_WIDTH))
    scratch = [pltpu.VMEM((rows, LANES), F32), pltpu.VMEM((rows, LANES), F32)]
    if class_major:
        scratch.append(pltpu.VMEM((3 * PAIRS, rows, LANES), F32))
        scratch.append(pltpu.VMEM((3 * PAIRS, DEINT, rows // DEINT, LANES), F32))
    if seq_rows:
        scratch.append(pltpu.VMEM((2, GMLP_WIDTH // LANES, rows, LANES), F32))
        spatial_specs = [pl.BlockSpec(memory_space=pltpu.SMEM)] * 2
    else:
        spatial_specs = [_const_spec(wsp.shape), _const_spec(bsp.shape)]
    rider, rider_specs, rider_args = None, [], []
    if sample_attn is not None:
        rider_args, rider_specs, rider_out_shape, rider_out_spec, rider = _sample_attention_specs(*sample_attn)
        assert rider_args[0].shape[0] == grid
        out_shape.append(rider_out_shape)
        out_specs.append(rider_out_spec)
    inv_freq = ROPE_THETA ** (-jnp.arange(0, ROT_DIM, 2, dtype=F32) / ROT_DIM)
    per_head = jnp.concatenate([inv_freq, inv_freq, jnp.zeros((HEAD_DIM - ROT_DIM,), F32)])
    invf = jnp.tile(per_head, LANES // HEAD_DIM).reshape(1, LANES)
    return pl.pallas_call(
        functools.partial(_project_kernel, rows=rows, tail_from=tail_from, tail_transposed=tail_transposed,
                          pos_base=float(pos_base),
                          pos_period=pos_period, class_major=class_major, emit_vn=emit_vn, seq_rows=seq_rows,
                          rider=rider),
        grid=(grid,),
        in_specs=[
            row_spec(D_MODEL),
            _const_spec((1, D_MODEL)),
            _const_spec(w_in_b.shape),
            _const_spec((1, LANES)),
            _const_spec((1, GMLP_WIDTH)),
            _const_spec((1, GMLP_WIDTH)),
            *spatial_specs,
            *rider_specs,
        ],
        out_specs=out_specs,
        out_shape=out_shape,
        scratch_shapes=scratch,
        compiler_params=pltpu.CompilerParams(
            dimension_semantics=("arbitrary",), vmem_limit_bytes=VMEM_LIMIT_BYTES),
        name="project",
    )(x, g1, w_in_b, invf, ln_g, ln_b, wsp, bsp, *rider_args)


def _prompt_attn_kernel(q_ref, k_ref, v_ref, o_ref, kc_ref, vc_ref, q32_ref, acc_ref, m_ref, l_ref,
                        s_ref, p_ref, ms_ref):
    i = pl.program_id(1)
    rows = q_ref.shape[1]
    width = q_ref.shape[2]
    npair = width // LANES
    half = SUBLANES

    @pl.when((pl.program_id(0) == 0) & (i == 0))
    def _():
        p_ref[1] = jnp.zeros(p_ref.shape[1:], BF16)
        ms_ref[1] = jnp.zeros(ms_ref.shape[1:], F32)

    @pl.when(i == 0)
    def _():
        kc_ref[0, :, 0:rows, :] = jnp.zeros((N_CLASS, rows, width), BF16)
        vc_ref[0, :, 0:rows, :] = jnp.zeros((N_CLASS, rows, width), BF16)

    @pl.when(i > 0)
    def _():
        kc_ref[0, :, 0:rows, :] = kc_ref[0, :, rows:2 * rows, :]
        vc_ref[0, :, 0:rows, :] = vc_ref[0, :, rows:2 * rows, :]

    kc_ref[0, :, rows:2 * rows, :] = k_ref[...]
    vc_ref[0, :, rows:2 * rows, :] = v_ref[...]
    lo = rows - PACKED_ROWS
    for ref in (kc_ref, vc_ref):
        tail = ref[0, :, lo:2 * rows, :].astype(F32)
        ref[1, :, lo:2 * rows - PACKED_ROWS, :] = tail[:, half:half + rows, :].astype(BF16)
    q32_ref[...] = q_ref[...].astype(F32)
    has_prev = i > 0

    def lanes(pr):
        return slice(pr * LANES, (pr + 1) * LANES)

    mq, nk = rows, 2 * rows
    olane = lax.broadcasted_iota(jnp.int32, (mq, LANES), 1) < HEAD_DIM
    qzero = jnp.zeros((mq, LANES), BF16)
    ones_v = jnp.ones((nk, LANES), BF16)
    nt = (((1,), (1,)), ((), ()))

    def run(n_units, q_of, k_of, v_of, mask_of, state_of, put, first, last):
        units = lambda t: [t * ATTN_GROUP + j for j in range(ATTN_GROUP)]

        def scores(t, slot):
            for j, u in enumerate(units(t)):
                for pr in range(npair):
                    q = q_of(u, pr)
                    qq = jnp.concatenate([jnp.where(olane, q, qzero), jnp.where(olane, qzero, q)], axis=0)
                    s_ref[slot, j, pr] = lax.dot_general(qq, k_of(u, pr), nt, preferred_element_type=F32)

        def softmax(t, slot):
            for j, u in enumerate(units(t)):
                mask = mask_of(u)
                for pr in range(npair):
                    ms = []
                    for hh in range(2):
                        hs = slice(hh * mq, (hh + 1) * mq)
                        s = jnp.where(mask, s_ref[slot, j, pr, hs, :], NEG)
                        m = jnp.max(s, axis=-1, keepdims=True)
                        p_ref[slot, j, pr, hs, :] = jnp.exp2(s - m).astype(BF16)
                        ms.append(m)
                    ms_ref[slot, j, pr] = jnp.where(olane, ms[0], ms[1])

        def values(t, slot, valid):
            for j, u in enumerate(units(t)):
                for pr in range(npair):
                    vv = jnp.concatenate([v_of(u, pr), ones_v], axis=1)
                    r = jnp.dot(p_ref[slot, j, pr], vv, preferred_element_type=F32)
                    pv = jnp.where(olane, r[0:mq, 0:LANES], r[mq:2 * mq, 0:LANES])
                    l = jnp.where(olane, r[0:mq, LANES:2 * LANES], r[mq:2 * mq, LANES:2 * LANES])
                    m = ms_ref[slot, j, pr]
                    if not first:
                        acc0, m0, l0 = state_of(u, pr)
                        m_new = jnp.maximum(m0, m)
                        a = jnp.exp2(m0 - m_new)
                        b = jnp.exp2(m - m_new)
                        pv = acc0 * a + pv * b
                        l = l0 * a + l * b
                        m = m_new
                        if last:
                            pv = pv / l
                        if valid is not None:
                            pv, m, l = jnp.where(valid, pv, acc0), jnp.where(valid, m, m0), jnp.where(valid, l, l0)
                    put(u, pr, pv if last else (pv, m, l))

        n_trips = n_units // ATTN_GROUP
        scores(jnp.int32(0), 0)

        def trip(t, carry):
            slot = t % 2
            values(jnp.maximum(t - 1, 0), 1 - slot, t > 0)
            softmax(t, slot)
            scores(jnp.minimum(t + 1, n_trips - 1), 1 - slot)
            return carry

        lax.fori_loop(0, n_trips, trip, 0)
        values(jnp.int32(n_trips - 1), (n_trips - 1) % 2, None)

    def put_state(slabs_of, r0_of, n):
        def put(u, pr, res):
            for a, sb in enumerate(slabs_of(u)):
                rs = slice(a * n, (a + 1) * n)
                acc_ref[sb, pl.ds(r0_of(u), n), lanes(pr)] = res[0][rs]
                m_ref[sb, pl.ds(r0_of(u), n), lanes(pr)] = res[1][rs]
                l_ref[sb, pl.ds(r0_of(u), n), lanes(pr)] = res[2][rs]
        return put

    def gather(ref, slabs, r0, n, pr):
        return jnp.concatenate([ref[sb, pl.ds(r0, n), lanes(pr)] for sb in slabs], axis=0)

    state_refs = (acc_ref, m_ref, l_ref)
    kplain, vplain = kc_ref.at[0], vc_ref.at[0]

    qi = lax.broadcasted_iota(jnp.int32, (mq, nk), 0)
    kj = lax.broadcasted_iota(jnp.int32, (mq, nk), 1)
    diff = qi + rows - kj
    mask16 = (diff >= 0) & (diff <= SPAN) & ((kj >= rows) | has_prev)
    run(N_CLASS,
        q_of=lambda u, pr: q_ref[u, :, lanes(pr)],
        k_of=lambda u, pr: kplain[u, :, lanes(pr)],
        v_of=lambda u, pr: vplain[u, :, lanes(pr)],
        mask_of=lambda u: mask16, state_of=None,
        put=put_state(lambda u: [u], lambda u: 0, rows), first=True, last=False)

    n4 = N_CLASS // 4
    qb = rows // n4
    d4 = 4 * (qi % qb - kj % (2 * qb) + qb) + (qi // qb - kj // (2 * qb))
    band4 = (d4 >= 0) & (d4 <= SPAN)
    cur4 = kj % (2 * qb) >= qb
    slabs4 = lambda u: [u // n4 + 4 * a for a in range(n4)]
    q0_4 = lambda u: pl.multiple_of((u % n4) * qb, qb)
    k0_4 = lambda u: pl.multiple_of(rows - qb + (u % n4) * qb, qb)
    run(N_CLASS,
        q_of=lambda u, pr: gather(q_ref, slabs4(u), q0_4(u), qb, pr),
        k_of=lambda u, pr: gather(kplain, slabs4(u), k0_4(u), 2 * qb, pr),
        v_of=lambda u, pr: gather(vplain, slabs4(u), k0_4(u), 2 * qb, pr),
        mask_of=lambda u: band4 & (cur4 | has_prev | (u % n4 > 0)),
        state_of=lambda u, pr: tuple(gather(ref, slabs4(u), q0_4(u), qb, pr) for ref in state_refs),
        put=put_state(slabs4, q0_4, qb), first=False, last=False)

    d1 = N_CLASS * (qi % half - kj % (2 * half) + half) + (qi // half - kj // (2 * half))
    band1 = (d1 >= 0) & (d1 <= SPAN)
    cur1 = kj % (2 * half) >= half
    every = list(range(N_CLASS))
    q0_1 = lambda u: pl.multiple_of(u * half, half)
    kcopy = lambda u: 1 - u % 2
    k0_1 = lambda u: pl.multiple_of(((rows - half + u * half) // PACKED_ROWS) * PACKED_ROWS, PACKED_ROWS)

    def put_out(u, pr, res):
        for sb in every:
            acc_ref[sb, pl.ds(q0_1(u), half), lanes(pr)] = res[sb * half:(sb + 1) * half]

    run(rows // half,
        q_of=lambda u, pr: gather(q32_ref, every, q0_1(u), half, pr).astype(BF16),
        k_of=lambda u, pr: gather(kc_ref.at[kcopy(u)], every, k0_1(u), 2 * half, pr),
        v_of=lambda u, pr: gather(vc_ref.at[kcopy(u)], every, k0_1(u), 2 * half, pr),
        mask_of=lambda u: band1 & (cur1 | has_prev | (u > 0)),
        state_of=lambda u, pr: tuple(gather(ref, every, q0_1(u), half, pr) for ref in state_refs),
        put=put_out, first=False, last=True)
    o_ref[...] = acc_ref[...].astype(BF16)


def _prompt_attention(q_cm, k_cm, v_cm):
    n_slab_rows = q_cm.shape[1]
    npair = ATTN_STEP_LANES // LANES
    spec = pl.BlockSpec((N_CLASS, SPAN, ATTN_STEP_LANES), lambda h, i: (0, i, h))
    return pl.pallas_call(
        _prompt_attn_kernel,
        grid=(ATTN_WIDTH // ATTN_STEP_LANES, n_slab_rows // SPAN),
        in_specs=[spec, spec, spec],
        out_specs=spec,
        out_shape=jax.ShapeDtypeStruct(q_cm.shape, BF16),
        scratch_shapes=[
            pltpu.VMEM((2, N_CLASS, 2 * SPAN, ATTN_STEP_LANES), BF16),
            pltpu.VMEM((2, N_CLASS, 2 * SPAN, ATTN_STEP_LANES), BF16),
            pltpu.VMEM((N_CLASS, SPAN, ATTN_STEP_LANES), F32),
            pltpu.VMEM((N_CLASS, SPAN, ATTN_STEP_LANES), F32),
            pltpu.VMEM((N_CLASS, SPAN, ATTN_STEP_LANES), F32),
            pltpu.VMEM((N_CLASS, SPAN, ATTN_STEP_LANES), F32),
            pltpu.VMEM((2, ATTN_GROUP, npair, 2 * SPAN, 2 * SPAN), F32),
            pltpu.VMEM((2, ATTN_GROUP, npair, 2 * SPAN, 2 * SPAN), BF16),
            pltpu.VMEM((2, ATTN_GROUP, npair, SPAN, LANES), F32),
        ],
        compiler_params=pltpu.CompilerParams(
            dimension_semantics=("arbitrary", "arbitrary"), vmem_limit_bytes=VMEM_LIMIT_BYTES),
        name="prompt_attention",
    )(q_cm, k_cm, v_cm)


def _sample_attn_kernel(q_ref, kn_ref, vn_ref, kt_ref, vt_ref, o_ref, *, t_new, lbuf):
    width = ATTN_WIDTH
    nrow = t_new * N_HEADS
    q = q_ref[...]
    kn = kn_ref[...]
    vnew = vn_ref[...]

    row = lax.broadcasted_iota(jnp.int32, (nrow, width), 0)
    lane = lax.broadcasted_iota(jnp.int32, (nrow, width), 1)
    hmask = (lane // HEAD_DIM) == (row % N_HEADS)
    qtok = jnp.concatenate([jnp.broadcast_to(q[j:j + 1, :], (N_HEADS, width)) for j in range(t_new)], axis=0)
    qrows = jnp.where(hmask, qtok, 0.0)

    kt = kt_ref[...].reshape(width, lbuf).astype(BF16)
    vt = vt_ref[...].reshape(width, lbuf).astype(BF16)
    s = jnp.dot(qrows.astype(BF16), kt, preferred_element_type=F32)

    def reach_count(delta):
        cnt = jnp.zeros(delta.shape, F32)
        for window, dil in DILATIONS:
            hit = (delta >= 0) & (delta % dil == 0) & (delta <= window)
            cnt = cnt + jnp.where(hit, 1.0, 0.0)
        return cnt

    pos = lax.broadcasted_iota(jnp.int32, (nrow, lbuf), 1)
    tok = lax.broadcasted_iota(jnp.int32, (nrow, lbuf), 0) // N_HEADS
    cnt = reach_count(lbuf + tok - pos)
    tok1 = tok[:, 0:1]
    cnew = [reach_count(tok1 - j) for j in range(t_new)]
    snew = [jnp.sum(qrows * kn[j:j + 1, :], axis=-1, keepdims=True) for j in range(t_new)]

    s = jnp.where(cnt > 0.0, s, NEG)
    m = jnp.max(s, axis=-1, keepdims=True)
    for j in range(t_new):
        m = jnp.maximum(m, jnp.where(cnew[j] > 0.0, snew[j], NEG))
    e = jnp.exp2(s - m) * cnt
    l = jnp.sum(e, axis=-1, keepdims=True)
    o = lax.dot_general(e.astype(BF16), vt, (((1,), (1,)), ((), ())), preferred_element_type=F32)
    for j in range(t_new):
        ej = jnp.where(cnew[j] > 0.0, jnp.exp2(snew[j] - m), 0.0) * cnew[j]
        l = l + ej
        o = o + ej * vnew[j:j + 1, :]
    o = jnp.where(hmask, o / l, 0.0)
    for j in range(t_new):
        o_ref[j:j + 1, :] = jnp.sum(o[j * N_HEADS:(j + 1) * N_HEADS, :], axis=0, keepdims=True)


def _sample_attention_specs(q, k_new, v_new, cache_kt, cache_vt):
    nb, t_new, width = q.shape
    lbuf = cache_kt.shape[-1]
    pad = lambda a: jnp.pad(a, ((0, 0), (0, SUBLANES - t_new), (0, 0)))
    new_spec = pl.BlockSpec((None, SUBLANES, width), lambda b: (b, 0, 0))
    cache_spec = pl.BlockSpec((None, N_HEADS, HEAD_DIM, lbuf), lambda b: (b, 0, 0, 0))
    return ([pad(q), pad(k_new), pad(v_new), cache_kt, cache_vt],
            [new_spec, new_spec, new_spec, cache_spec, cache_spec],
            jax.ShapeDtypeStruct((nb, t_new, width), F32),
            pl.BlockSpec((None, t_new, width), lambda b: (b, 0, 0)),
            (t_new, lbuf))


def _sample_attention(q, k_new, v_new, cache_kt, cache_vt):
    args, in_specs, out_shape, out_spec, (t_new, lbuf) = _sample_attention_specs(q, k_new, v_new, cache_kt, cache_vt)
    return pl.pallas_call(
        functools.partial(_sample_attn_kernel, t_new=t_new, lbuf=lbuf),
        grid=(q.shape[0],),
        in_specs=in_specs,
        out_specs=out_spec,
        out_shape=out_shape,
        compiler_params=pltpu.CompilerParams(
            dimension_semantics=("arbitrary",), vmem_limit_bytes=VMEM_LIMIT_BYTES),
        name="sample_attention",
    )(*args)


def _finish_kernel(x_ref, attn_ref, gated_ref, p_ref, wo_ref, g2_ref, wup_ref, wdn_ref,
                   gg_ref, wg_ref, wple_ref, gf_ref, y_ref, *nat_refs, apply_final):
    if nat_refs:
        nat_ref, per_class = nat_refs[0], attn_ref.shape[1]
        for r in range(N_CLASS):
            for s in range(PAIRS):
                nat_ref[s, pl.ds(r, per_class, stride=N_CLASS), :] = attn_ref[r, :, s * LANES:(s + 1) * LANES].astype(F32)
        attn = jnp.concatenate([nat_ref[s] for s in range(PAIRS)], axis=-1).astype(BF16)
    else:
        attn = attn_ref[...]
    mix = jnp.concatenate([attn, gated_ref[...]], axis=-1)
    h = x_ref[...] + jnp.dot(mix, wo_ref[...], preferred_element_type=F32)
    n2 = _rms(h, g2_ref[...]).astype(BF16)
    f = jnp.zeros_like(h)
    for c in range(D_FF // FF_CHUNK):
        cs = slice(c * FF_CHUNK, (c + 1) * FF_CHUNK)
        up = jnp.dot(n2, wup_ref[:, cs], preferred_element_type=F32)
        act = jnp.square(jnp.maximum(up, 0.0)).astype(BF16)
        f = f + jnp.dot(act, wdn_ref[cs, :], preferred_element_type=F32)
    h = h + f
    gate = jax.nn.sigmoid(jnp.dot(_rms(h, gg_ref[...]).astype(BF16), wg_ref[...], preferred_element_type=F32))
    ple = jnp.dot(p_ref[...].astype(BF16), wple_ref[...], preferred_element_type=F32)
    h = h + gate * ple
    y_ref[...] = _rms(h, gf_ref[...]) if apply_final else h


def _finish(x, attn, gated, p, wo, g2, wup, wdn, gg, wg, wple, gf, *, rows, apply_final):
    n_rows = x.shape[0]
    row_spec = lambda width: pl.BlockSpec((rows, width), lambda i: (i, 0))
    if attn.ndim == 3:
        attn_spec = pl.BlockSpec((N_CLASS, rows // N_CLASS, ATTN_WIDTH), lambda i: (0, i, 0))
        scratch = [pltpu.VMEM((PAIRS, rows, LANES), F32)]
    else:
        attn_spec, scratch = row_spec(ATTN_WIDTH), []
    return pl.pallas_call(
        functools.partial(_finish_kernel, apply_final=apply_final),
        grid=(n_rows // rows,),
        in_specs=[
            row_spec(D_MODEL), attn_spec, row_spec(GMLP_WIDTH), row_spec(PLE_DIM),
            _const_spec(wo.shape), _const_spec((1, D_MODEL)), _const_spec(wup.shape), _const_spec(wdn.shape),
            _const_spec((1, D_MODEL)), _const_spec(wg.shape), _const_spec(wple.shape), _const_spec((1, D_MODEL)),
        ],
        out_specs=row_spec(D_MODEL),
        out_shape=jax.ShapeDtypeStruct((n_rows, D_MODEL), F32),
        scratch_shapes=scratch,
        compiler_params=pltpu.CompilerParams(
            dimension_semantics=("arbitrary",), vmem_limit_bytes=VMEM_LIMIT_BYTES),
        name="finish",
    )(x, attn, gated, p, wo, g2, wup, wdn, gg, wg, wple, gf)


def _pair_spatial(w):
    g, l, _ = w.shape
    return w.reshape(g // 2, 2, l, l).transpose(0, 2, 1, 3).reshape(g // 2, l, 2 * l).astype(BF16)


def _bias_lanes(b):
    return jnp.repeat(b.T, GMLP_WIDTH // GMLP_GROUPS, axis=1)


def kernel(x_prompt, x_sample, cache_k, cache_v, p_prompt, p_sample, norm1_g, w_in, ln_v_g, ln_v_b,
           w_spatial, b_spatial, w_out, norm2_g, w_up, w_down, gate_norm_g, w_gate, w_ple, final_g):
    depth = w_in.shape[0]
    nbp, seq, _ = x_prompt.shape
    nbs, t_new, _ = x_sample.shape
    lbuf = cache_k.shape[2]
    keep = min(WINDOW_MAX, seq)
    assert nbp == 1 and seq % (N_CLASS * SPAN) == 0 and (nbs * t_new) % CHUNK == 0 and t_new <= CHUNK
    n_s = nbs * t_new

    row2 = lambda a: a.reshape(1, -1)

    hp = x_prompt.reshape(seq, D_MODEL)
    hs = x_sample.reshape(n_s, D_MODEL)
    nk_p, nv_p, nk_s, nv_s, nvc_s = [], [], [], [], []
    for i in range(depth):
        last = i == depth - 1
        w_in_b = w_in[i].astype(BF16)
        wo_b, wup_b, wdn_b = w_out[i].astype(BF16), w_up[i].astype(BF16), w_down[i].astype(BF16)
        wg_b, wple_b = w_gate[i].astype(BF16), w_ple[i].astype(BF16)
        fin_w = (wo_b, row2(norm2_g[i]), wup_b, wdn_b, row2(gate_norm_g[i]), wg_b, wple_b, row2(final_g))
        ln_g, ln_b = row2(ln_v_g[i]), row2(ln_v_b[i])

        wsp_s = w_spatial[i][:, :t_new, :t_new].reshape(-1)
        bsp_s = b_spatial[i][:, :t_new].reshape(-1)
        q_s, _, _, kf_s, vf_s, gated_s, vn = _project(hs, row2(norm1_g[i]), w_in_b, ln_g, ln_b, wsp_s, bsp_s,
                                                      rows=n_s, tail_rows=n_s, pos_base=PAST_LEN, pos_period=t_new,
                                                      class_major=False, emit_vn=True, seq_rows=t_new)
        r3 = lambda a: a.reshape(nbs, t_new, ATTN_WIDTH)
        sample_ops = (r3(q_s).astype(F32), r3(kf_s), r3(vf_s),
                      cache_k[i].transpose(0, 2, 3, 1), cache_v[i].transpose(0, 2, 3, 1))

        ride = seq // PROMPT_ROWS == nbs
        wsp_p = _pair_spatial(w_spatial[i])
        bsp_p = _bias_lanes(b_spatial[i])
        q, k, v, kf, vf, gated, *rode = _project(hp, row2(norm1_g[i]), w_in_b, ln_g, ln_b, wsp_p, bsp_p,
                                                 rows=PROMPT_ROWS, tail_rows=keep, pos_base=0, pos_period=PROMPT_ROWS,
                                                 class_major=True, emit_vn=False, tail_transposed=True,
                                                 sample_attn=sample_ops if ride else None)
        attn_cm = _prompt_attention(q, k, v)
        hp = _finish(hp, attn_cm, gated, p_prompt[i].reshape(seq, PLE_DIM), *fin_w, rows=PROMPT_ROWS, apply_final=last)
        from_cm = lambda a: a.reshape(N_HEADS, HEAD_DIM, keep).transpose(2, 0, 1).reshape(nbp, keep, N_HEADS, HEAD_DIM)
        nk_p.append(from_cm(kf))
        nv_p.append(from_cm(vf))

        attn_s = rode[0] if ride else _sample_attention(*sample_ops)
        hs = _finish(hs, attn_s.reshape(n_s, ATTN_WIDTH).astype(BF16), gated_s, p_sample[i].reshape(n_s, PLE_DIM),
                     *fin_w, rows=n_s, apply_final=last)
        kf, vf = kf_s, vf_s
        nk_s.append(kf.reshape(nbs, t_new, N_HEADS, HEAD_DIM))
        nv_s.append(vf.reshape(nbs, t_new, N_HEADS, HEAD_DIM))
        nvc_s.append(vn.reshape(nbs, t_new, GMLP_WIDTH))

    return (hp.reshape(nbp, seq, D_MODEL), hs.reshape(nbs, t_new, D_MODEL),
            jnp.stack(nk_p), jnp.stack(nv_p), jnp.stack(nk_s), jnp.stack(nv_s), jnp.stack(nvc_s))
```

```python
import functools

import jax
import jax.numpy as jnp
from jax import lax
from jax.experimental import pallas as pl
from jax.experimental.pallas import tpu as pltpu

D_MODEL = 1024
N_HEADS = 8
HEAD_DIM = 64
ATTN_WIDTH = N_HEADS * HEAD_DIM
GMLP_GROUPS = 8
GMLP_WIDTH = 512
CHUNK = 128
DILATIONS = ((128, 1), (512, 4), (2048, 16))
WINDOW_MAX = 2048
PAST_LEN = 16384
ROT_DIM = HEAD_DIM // 4
ROPE_THETA = 500000.0
D_FF = 4 * D_MODEL
PLE_DIM = 256
EPS = 1e-6
NEG = -1e30
Q_SCALE = HEAD_DIM ** -0.5 * 1.4426950408889634

LANES = 128
VMEM_LIMIT_BYTES = 56 * 1024 * 1024

N_CLASS = 16
DEINT = 4
SPAN = 128
PAIRS = ATTN_WIDTH // LANES
ATTN_STEP_LANES = 2 * LANES
ATTN_GROUP = 2
SUBLANES = 8
PACKED_ROWS = 16
FF_CHUNK = 1024
PROMPT_ROWS = 512

BF16 = jnp.bfloat16
F32 = jnp.float32


def _rms(x, g):
    ms = jnp.mean(x * x, axis=-1, keepdims=True)
    return x * lax.rsqrt(ms + EPS) * g


def _const_spec(shape):
    nd = len(shape)
    return pl.BlockSpec(shape, lambda *_: (0,) * nd, pipeline_mode=pl.Buffered(1))


def _project_kernel(x_ref, g1_ref, w_ref, invf_ref, lng_ref, lnb_ref, wsp_ref, bsp_ref, *rest,
                    rows, tail_from, tail_transposed, pos_base, pos_period, class_major, emit_vn, seq_rows, rider):
    rest = list(rest)
    rider_in = [rest.pop(0) for _ in range(5)] if rider else None
    q_ref, k_ref, v_ref, kf_ref, vf_ref, gated_ref = (rest.pop(0) for _ in range(6))
    vn_ref = rest.pop(0) if emit_vn else None
    if rider:
        _sample_attn_kernel(*rider_in, rest.pop(0), t_new=rider[0], lbuf=rider[1])
    cos_off_ref, sin_off_ref = rest.pop(0), rest.pop(0)
    zs_ref, zq_ref = (rest.pop(0), rest.pop(0)) if class_major else (None, None)
    mix_ref = rest.pop(0) if seq_rows else None
    i = pl.program_id(0)

    @pl.when(i == 0)
    def _():
        off = (lax.broadcasted_iota(jnp.int32, (rows, LANES), 0) % pos_period).astype(F32)
        ang = off * invf_ref[...]
        cos_off_ref[...] = jnp.cos(ang)
        sin_off_ref[...] = jnp.sin(ang)

    tile_stride = rows if pos_period == rows else 0
    base = (i * tile_stride).astype(F32) + pos_base
    base_ang = base * invf_ref[...]
    cb, sb = jnp.cos(base_ang), jnp.sin(base_ang)
    co, so = cos_off_ref[...], sin_off_ref[...]
    cosf = cb * co - sb * so
    sint = sb * co + cb * so
    head_lane = lax.broadcasted_iota(jnp.int32, (1, LANES), 1) % HEAD_DIM
    sina = jnp.where(head_lane < ROT_DIM // 2, -sint, 0.0)
    sinb = jnp.where((head_lane >= ROT_DIM // 2) & (head_lane < ROT_DIM), sint, 0.0)

    def rope(z):
        return z * cosf + pltpu.roll(z, LANES - ROT_DIM // 2, 1) * sina + pltpu.roll(z, ROT_DIM // 2, 1) * sinb

    xn = _rms(x_ref[...], g1_ref[...]).astype(BF16)
    zq = jnp.dot(xn, w_ref[:, 0:ATTN_WIDTH], preferred_element_type=F32)
    zk = jnp.dot(xn, w_ref[:, ATTN_WIDTH:2 * ATTN_WIDTH], preferred_element_type=F32)
    zv = jnp.dot(xn, w_ref[:, 2 * ATTN_WIDTH:3 * ATTN_WIDTH], preferred_element_type=F32)
    slabs = [slice(s * LANES, (s + 1) * LANES) for s in range(PAIRS)]
    qr = [rope(zq[:, sl]) * Q_SCALE for sl in slabs]
    kr = [rope(zk[:, sl]) for sl in slabs]

    @pl.when(i >= tail_from)
    def _():
        if tail_transposed:
            kf_ref[...] = jnp.concatenate(kr, axis=-1).T
            vf_ref[...] = zv.T
        else:
            for s, sl in enumerate(slabs):
                kf_ref[:, sl] = kr[s]
            vf_ref[...] = zv

    if class_major:
        per_class = rows // N_CLASS
        for s, sl in enumerate(slabs):
            zs_ref[s] = qr[s]
            zs_ref[PAIRS + s] = kr[s]
            zs_ref[2 * PAIRS + s] = zv[:, sl]
        for n in range(3 * PAIRS):
            for c in range(DEINT):
                zq_ref[n, c] = zs_ref[n, pl.ds(c, rows // DEINT, stride=DEINT), :]
        for r in range(N_CLASS):
            for s, sl in enumerate(slabs):
                pick = lambda n: zq_ref[n, r % DEINT, pl.ds(r // DEINT, per_class, stride=DEINT), :].astype(BF16)
                q_ref[r, :, sl] = pick(s)
                k_ref[r, :, sl] = pick(PAIRS + s)
                v_ref[r, :, sl] = pick(2 * PAIRS + s)
    else:
        for s, sl in enumerate(slabs):
            q_ref[:, sl] = qr[s].astype(BF16)
            k_ref[:, sl] = kr[s].astype(BF16)
        v_ref[...] = zv.astype(BF16)

    o_u = 3 * ATTN_WIDTH
    u = jax.nn.gelu(jnp.dot(xn, w_ref[:, o_u:o_u + GMLP_WIDTH], preferred_element_type=F32))
    vc = jax.nn.gelu(jnp.dot(xn, w_ref[:, o_u + GMLP_WIDTH:o_u + 2 * GMLP_WIDTH], preferred_element_type=F32))
    mu = jnp.mean(vc, axis=-1, keepdims=True)
    cen = vc - mu
    var = jnp.mean(cen * cen, axis=-1, keepdims=True)
    vn = cen * lax.rsqrt(var + EPS) * lng_ref[...] + lnb_ref[...]
    if emit_vn:
        vn_ref[...] = vn
    if seq_rows:
        nseq = rows // seq_rows
        in_first = lax.broadcasted_iota(jnp.int32, (1, LANES), 1) < HEAD_DIM
        for s in range(GMLP_WIDTH // LANES):
            ga, gb = 2 * s, 2 * s + 1
            mix_ref[0, s] = vn[:, s * LANES:(s + 1) * LANES]
            xs = [mix_ref[0, s, pl.ds(j, nseq, stride=seq_rows), :] for j in range(seq_rows)]
            for r in range(seq_rows):
                acc = jnp.broadcast_to(
                    jnp.where(in_first, bsp_ref[ga * seq_rows + r], bsp_ref[gb * seq_rows + r]), (nseq, LANES))
                for j in range(r + 1):
                    w = jnp.where(in_first, wsp_ref[(ga * seq_rows + r) * seq_rows + j],
                                  wsp_ref[(gb * seq_rows + r) * seq_rows + j])
                    acc = acc + w * xs[j]
                mix_ref[1, s, pl.ds(r, nseq, stride=seq_rows), :] = acc
        mixed = jnp.concatenate([mix_ref[1, s] for s in range(GMLP_WIDTH // LANES)], axis=-1)
        gated_ref[...] = (u * mixed).astype(BF16)
        return
    vnb = vn.astype(BF16)

    row = lax.broadcasted_iota(jnp.int32, (CHUNK, 2 * CHUNK), 0)
    col = lax.broadcasted_iota(jnp.int32, (CHUNK, 2 * CHUNK), 1)
    tril = (col % CHUNK) <= row
    lane = lax.broadcasted_iota(jnp.int32, (CHUNK, LANES), 1)
    first = lane < HEAD_DIM
    zero = jnp.zeros((CHUNK, LANES), BF16)
    for s in range(GMLP_WIDTH // LANES):
        sl = slice(s * LANES, (s + 1) * LANES)
        wp = jnp.where(tril, wsp_ref[s], jnp.zeros((), BF16))
        bias = bsp_ref[:, sl]
        for c in range(rows // CHUNK):
            rs = slice(c * CHUNK, (c + 1) * CHUNK)
            vblk = vnb[rs, sl]
            rhs = jnp.concatenate([jnp.where(first, vblk, zero), jnp.where(first, zero, vblk)], axis=0)
            mixed = jnp.dot(wp, rhs, preferred_element_type=F32) + bias
            gated_ref[rs, sl] = (u[rs, sl] * mixed).astype(BF16)


def _project(x, g1, w_in_b, ln_g, ln_b, wsp, bsp, *, rows, tail_rows, pos_base, pos_period, class_major, emit_vn,
             seq_rows=None, tail_transposed=False, sample_attn=None):
    n_rows = x.shape[0]
    grid = n_rows // rows
    tail_from = (n_rows - tail_rows) // rows
    row_spec = lambda width: pl.BlockSpec((rows, width), lambda i: (i, 0))
    if tail_transposed:
        tail_shape = jax.ShapeDtypeStruct((ATTN_WIDTH, tail_rows), F32)
        tail_spec = pl.BlockSpec((ATTN_WIDTH, rows), lambda i: (0, jnp.maximum(i - tail_from, 0)))
    else:
        tail_shape = jax.ShapeDtypeStruct((tail_rows, ATTN_WIDTH), F32)
        tail_spec = pl.BlockSpec((rows, ATTN_WIDTH), lambda i: (jnp.maximum(i - tail_from, 0), 0))
    if class_major:
        qkv_shape = jax.ShapeDtypeStruct((N_CLASS, n_rows // N_CLASS, ATTN_WIDTH), BF16)
        qkv_spec = pl.BlockSpec((N_CLASS, rows // N_CLASS, ATTN_WIDTH), lambda i: (0, i, 0))
    else:
        qkv_shape = jax.ShapeDtypeStruct((n_rows, ATTN_WIDTH), BF16)
        qkv_spec = row_spec(ATTN_WIDTH)
    out_shape = [
        qkv_shape,
        qkv_shape,
        qkv_shape,
        tail_shape,
        tail_shape,
        jax.ShapeDtypeStruct((n_rows, GMLP_WIDTH), BF16),
    ]
    out_specs = [qkv_spec] * 3 + [tail_spec] * 2 + [row_spec(GMLP_WIDTH)]
    if emit_vn:
        out_shape.append(jax.ShapeDtypeStruct((n_rows, GMLP_WIDTH), F32))
        out_specs.append(row_spec(GMLP_WIDTH))
    scratch = [pltpu.VMEM((rows, LANES), F32), pltpu.VMEM((rows, LANES), F32)]
    if class_major:
        scratch.append(pltpu.VMEM((3 * PAIRS, rows, LANES), F32))
        scratch.append(pltpu.VMEM((3 * PAIRS, DEINT, rows // DEINT, LANES), F32))
    if seq_rows:
        scratch.append(pltpu.VMEM((2, GMLP_WIDTH // LANES, rows, LANES), F32))
        spatial_specs = [pl.BlockSpec(memory_space=pltpu.SMEM)] * 2
    else:
        spatial_specs = [_const_spec(wsp.shape), _const_spec(bsp.shape)]
    rider, rider_specs, rider_args = None, [], []
    if sample_attn is not None:
        rider_args, rider_specs, rider_out_shape, rider_out_spec, rider = _sample_attention_specs(*sample_attn)
        assert rider_args[0].shape[0] == grid
        out_shape.append(rider_out_shape)
        out_specs.append(rider_out_spec)
    inv_freq = ROPE_THETA ** (-jnp.arange(0, ROT_DIM, 2, dtype=F32) / ROT_DIM)
    per_head = jnp.concatenate([inv_freq, inv_freq, jnp.zeros((HEAD_DIM - ROT_DIM,), F32)])
    invf = jnp.tile(per_head, LANES // HEAD_DIM).reshape(1, LANES)
    return pl.pallas_call(
        functools.partial(_project_kernel, rows=rows, tail_from=tail_from, tail_transposed=tail_transposed,
                          pos_base=float(pos_base),
                          pos_period=pos_period, class_major=class_major, emit_vn=emit_vn, seq_rows=seq_rows,
                          rider=rider),
        grid=(grid,),
        in_specs=[
            row_spec(D_MODEL),
            _const_spec((1, D_MODEL)),
            _const_spec(w_in_b.shape),
            _const_spec((1, LANES)),
            _const_spec((1, GMLP_WIDTH)),
            _const_spec((1, GMLP_WIDTH)),
            *spatial_specs,
            *rider_specs,
        ],
        out_specs=out_specs,
        out_shape=out_shape,
        scratch_shapes=scratch,
        compiler_params=pltpu.CompilerParams(
            dimension_semantics=("arbitrary",), vmem_limit_bytes=VMEM_LIMIT_BYTES),
        name="project",
    )(x, g1, w_in_b, invf, ln_g, ln_b, wsp, bsp, *rider_args)


def _prompt_attn_kernel(q_ref, k_ref, v_ref, o_ref, kc_ref, vc_ref, q32_ref, acc_ref, m_ref, l_ref,
                        s_ref, p_ref, ms_ref):
    i = pl.program_id(1)
    rows = q_ref.shape[1]
    width = q_ref.shape[2]
    npair = width // LANES
    half = SUBLANES

    @pl.when((pl.program_id(0) == 0) & (i == 0))
    def _():
        p_ref[1] = jnp.zeros(p_ref.shape[1:], BF16)
        ms_ref[1] = jnp.zeros(ms_ref.shape[1:], F32)

    @pl.when(i == 0)
    def _():
        kc_ref[0, :, 0:rows, :] = jnp.zeros((N_CLASS, rows, width), BF16)
        vc_ref[0, :, 0:rows, :] = jnp.zeros((N_CLASS, rows, width), BF16)

    @pl.when(i > 0)
    def _():
        kc_ref[0, :, 0:rows, :] = kc_ref[0, :, rows:2 * rows, :]
        vc_ref[0, :, 0:rows, :] = vc_ref[0, :, rows:2 * rows, :]

    kc_ref[0, :, rows:2 * rows, :] = k_ref[...]
    vc_ref[0, :, rows:2 * rows, :] = v_ref[...]
    lo = rows - PACKED_ROWS
    for ref in (kc_ref, vc_ref):
        tail = ref[0, :, lo:2 * rows, :].astype(F32)
        ref[1, :, lo:2 * rows - PACKED_ROWS, :] = tail[:, half:half + rows, :].astype(BF16)
    q32_ref[...] = q_ref[...].astype(F32)
    has_prev = i > 0

    def lanes(pr):
        return slice(pr * LANES, (pr + 1) * LANES)

    mq, nk = rows, 2 * rows
    olane = lax.broadcasted_iota(jnp.int32, (mq, LANES), 1) < HEAD_DIM
    qzero = jnp.zeros((mq, LANES), BF16)
    ones_v = jnp.ones((nk, LANES), BF16)
    nt = (((1,), (1,)), ((), ()))

    def run(n_units, q_of, k_of, v_of, mask_of, state_of, put, first, last):
        units = lambda t: [t * ATTN_GROUP + j for j in range(ATTN_GROUP)]

        def scores(t, slot):
            for j, u in enumerate(units(t)):
                for pr in range(npair):
                    q = q_of(u, pr)
                    qq = jnp.concatenate([jnp.where(olane, q, qzero), jnp.where(olane, qzero, q)], axis=0)
                    s_ref[slot, j, pr] = lax.dot_general(qq, k_of(u, pr), nt, preferred_element_type=F32)

        def softmax(t, slot):
            for j, u in enumerate(units(t)):
                mask = mask_of(u)
                for pr in range(npair):
                    ms = []
                    for hh in range(2):
                        hs = slice(hh * mq, (hh + 1) * mq)
                        s = jnp.where(mask, s_ref[slot, j, pr, hs, :], NEG)
                        m = jnp.max(s, axis=-1, keepdims=True)
                        p_ref[slot, j, pr, hs, :] = jnp.exp2(s - m).astype(BF16)
                        ms.append(m)
                    ms_ref[slot, j, pr] = jnp.where(olane, ms[0], ms[1])

        def values(t, slot, valid):
            for j, u in enumerate(units(t)):
                for pr in range(npair):
                    vv = jnp.concatenate([v_of(u, pr), ones_v], axis=1)
                    r = jnp.dot(p_ref[slot, j, pr], vv, preferred_element_type=F32)
                    pv = jnp.where(olane, r[0:mq, 0:LANES], r[mq:2 * mq, 0:LANES])
                    l = jnp.where(olane, r[0:mq, LANES:2 * LANES], r[mq:2 * mq, LANES:2 * LANES])
                    m = ms_ref[slot, j, pr]
                    if not first:
                        acc0, m0, l0 = state_of(u, pr)
                        m_new = jnp.maximum(m0, m)
                        a = jnp.exp2(m0 - m_new)
                        b = jnp.exp2(m - m_new)
                        pv = acc0 * a + pv * b
                        l = l0 * a + l * b
                        m = m_new
                        if last:
                            pv = pv / l
                        if valid is not None:
                            pv, m, l = jnp.where(valid, pv, acc0), jnp.where(valid, m, m0), jnp.where(valid, l, l0)
                    put(u, pr, pv if last else (pv, m, l))

        n_trips = n_units // ATTN_GROUP
        scores(jnp.int32(0), 0)

        def trip(t, carry):
            slot = t % 2
            values(jnp.maximum(t - 1, 0), 1 - slot, t > 0)
            softmax(t, slot)
            scores(jnp.minimum(t + 1, n_trips - 1), 1 - slot)
            return carry

        lax.fori_loop(0, n_trips, trip, 0)
        values(jnp.int32(n_trips - 1), (n_trips - 1) % 2, None)

    def put_state(slabs_of, r0_of, n):
        def put(u, pr, res):
            for a, sb in enumerate(slabs_of(u)):
                rs = slice(a * n, (a + 1) * n)
                acc_ref[sb, pl.ds(r0_of(u), n), lanes(pr)] = res[0][rs]
                m_ref[sb, pl.ds(r0_of(u), n), lanes(pr)] = res[1][rs]
                l_ref[sb, pl.ds(r0_of(u), n), lanes(pr)] = res[2][rs]
        return put

    def gather(ref, slabs, r0, n, pr):
        return jnp.concatenate([ref[sb, pl.ds(r0, n), lanes(pr)] for sb in slabs], axis=0)

    state_refs = (acc_ref, m_ref, l_ref)
    kplain, vplain = kc_ref.at[0], vc_ref.at[0]

    qi = lax.broadcasted_iota(jnp.int32, (mq, nk), 0)
    kj = lax.broadcasted_iota(jnp.int32, (mq, nk), 1)
    diff = qi + rows - kj
    mask16 = (diff >= 0) & (diff <= SPAN) & ((kj >= rows) | has_prev)
    run(N_CLASS,
        q_of=lambda u, pr: q_ref[u, :, lanes(pr)],
        k_of=lambda u, pr: kplain[u, :, lanes(pr)],
        v_of=lambda u, pr: vplain[u, :, lanes(pr)],
        mask_of=lambda u: mask16, state_of=None,
        put=put_state(lambda u: [u], lambda u: 0, rows), first=True, last=False)

    n4 = N_CLASS // 4
    qb = rows // n4
    d4 = 4 * (qi % qb - kj % (2 * qb) + qb) + (qi // qb - kj // (2 * qb))
    band4 = (d4 >= 0) & (d4 <= SPAN)
    cur4 = kj % (2 * qb) >= qb
    slabs4 = lambda u: [u // n4 + 4 * a for a in range(n4)]
    q0_4 = lambda u: pl.multiple_of((u % n4) * qb, qb)
    k0_4 = lambda u: pl.multiple_of(rows - qb + (u % n4) * qb, qb)
    run(N_CLASS,
        q_of=lambda u, pr: gather(q_ref, slabs4(u), q0_4(u), qb, pr),
        k_of=lambda u, pr: gather(kplain, slabs4(u), k0_4(u), 2 * qb, pr),
        v_of=lambda u, pr: gather(vplain, slabs4(u), k0_4(u), 2 * qb, pr),
        mask_of=lambda u: band4 & (cur4 | has_prev | (u % n4 > 0)),
        state_of=lambda u, pr: tuple(gather(ref, slabs4(u), q0_4(u), qb, pr) for ref in state_refs),
        put=put_state(slabs4, q0_4, qb), first=False, last=False)

    d1 = N_CLASS * (qi % half - kj % (2 * half) + half) + (qi // half - kj // (2 * half))
    band1 = (d1 >= 0) & (d1 <= SPAN)
    cur1 = kj % (2 * half) >= half
    every = list(range(N_CLASS))
    q0_1 = lambda u: pl.multiple_of(u * half, half)
    kcopy = lambda u: 1 - u % 2
    k0_1 = lambda u: pl.multiple_of(((rows - half + u * half) // PACKED_ROWS) * PACKED_ROWS, PACKED_ROWS)

    def put_out(u, pr, res):
        for sb in every:
            acc_ref[sb, pl.ds(q0_1(u), half), lanes(pr)] = res[sb * half:(sb + 1) * half]

    run(rows // half,
        q_of=lambda u, pr: gather(q32_ref, every, q0_1(u), half, pr).astype(BF16),
        k_of=lambda u, pr: gather(kc_ref.at[kcopy(u)], every, k0_1(u), 2 * half, pr),
        v_of=lambda u, pr: gather(vc_ref.at[kcopy(u)], every, k0_1(u), 2 * half, pr),
        mask_of=lambda u: band1 & (cur1 | has_prev | (u > 0)),
        state_of=lambda u, pr: tuple(gather(ref, every, q0_1(u), half, pr) for ref in state_refs),
        put=put_out, first=False, last=True)
    o_ref[...] = acc_ref[...].astype(BF16)


def _prompt_attention(q_cm, k_cm, v_cm):
    n_slab_rows = q_cm.shape[1]
    npair = ATTN_STEP_LANES // LANES
    spec = pl.BlockSpec((N_CLASS, SPAN, ATTN_STEP_LANES), lambda h, i: (0, i, h))
    return pl.pallas_call(
        _prompt_attn_kernel,
        grid=(ATTN_WIDTH // ATTN_STEP_LANES, n_slab_rows // SPAN),
        in_specs=[spec, spec, spec],
        out_specs=spec,
        out_shape=jax.ShapeDtypeStruct(q_cm.shape, BF16),
        scratch_shapes=[
            pltpu.VMEM((2, N_CLASS, 2 * SPAN, ATTN_STEP_LANES), BF16),
            pltpu.VMEM((2, N_CLASS, 2 * SPAN, ATTN_STEP_LANES), BF16),
            pltpu.VMEM((N_CLASS, SPAN, ATTN_STEP_LANES), F32),
            pltpu.VMEM((N_CLASS, SPAN, ATTN_STEP_LANES), F32),
            pltpu.VMEM((N_CLASS, SPAN, ATTN_STEP_LANES), F32),
            pltpu.VMEM((N_CLASS, SPAN, ATTN_STEP_LANES), F32),
            pltpu.VMEM((2, ATTN_GROUP, npair, 2 * SPAN, 2 * SPAN), F32),
            pltpu.VMEM((2, ATTN_GROUP, npair, 2 * SPAN, 2 * SPAN), BF16),
            pltpu.VMEM((2, ATTN_GROUP, npair, SPAN, LANES), F32),
        ],
        compiler_params=pltpu.CompilerParams(
            dimension_semantics=("arbitrary", "arbitrary"), vmem_limit_bytes=VMEM_LIMIT_BYTES),
        name="prompt_attention",
    )(q_cm, k_cm, v_cm)


def _sample_attn_kernel(q_ref, kn_ref, vn_ref, kt_ref, vt_ref, o_ref, *, t_new, lbuf):
    width = ATTN_WIDTH
    nrow = t_new * N_HEADS
    q = q_ref[...]
    kn = kn_ref[...]
    vnew = vn_ref[...]

    row = lax.broadcasted_iota(jnp.int32, (nrow, width), 0)
    lane = lax.broadcasted_iota(jnp.int32, (nrow, width), 1)
    hmask = (lane // HEAD_DIM) == (row % N_HEADS)
    qtok = jnp.concatenate([jnp.broadcast_to(q[j:j + 1, :], (N_HEADS, width)) for j in range(t_new)], axis=0)
    qrows = jnp.where(hmask, qtok, 0.0)

    kt = kt_ref[...].reshape(width, lbuf).astype(BF16)
    vt = vt_ref[...].reshape(width, lbuf).astype(BF16)
    s = jnp.dot(qrows.astype(BF16), kt, preferred_element_type=F32)

    def reach_count(delta):
        cnt = jnp.zeros(delta.shape, F32)
        for window, dil in DILATIONS:
            hit = (delta >= 0) & (delta % dil == 0) & (delta <= window)
            cnt = cnt + jnp.where(hit, 1.0, 0.0)
        return cnt

    pos = lax.broadcasted_iota(jnp.int32, (nrow, lbuf), 1)
    tok = lax.broadcasted_iota(jnp.int32, (nrow, lbuf), 0) // N_HEADS
    cnt = reach_count(lbuf + tok - pos)
    tok1 = tok[:, 0:1]
    cnew = [reach_count(tok1 - j) for j in range(t_new)]
    snew = [jnp.sum(qrows * kn[j:j + 1, :], axis=-1, keepdims=True) for j in range(t_new)]

    s = jnp.where(cnt > 0.0, s, NEG)
    m = jnp.max(s, axis=-1, keepdims=True)
    for j in range(t_new):
        m = jnp.maximum(m, jnp.where(cnew[j] > 0.0, snew[j], NEG))
    e = jnp.exp2(s - m) * cnt
    l = jnp.sum(e, axis=-1, keepdims=True)
    o = lax.dot_general(e.astype(BF16), vt, (((1,), (1,)), ((), ())), preferred_element_type=F32)
    for j in range(t_new):
        ej = jnp.where(cnew[j] > 0.0, jnp.exp2(snew[j] - m), 0.0) * cnew[j]
        l = l + ej
        o = o + ej * vnew[j:j + 1, :]
    o = jnp.where(hmask, o / l, 0.0)
    for j in range(t_new):
        o_ref[j:j + 1, :] = jnp.sum(o[j * N_HEADS:(j + 1) * N_HEADS, :], axis=0, keepdims=True)


def _sample_attention_specs(q, k_new, v_new, cache_kt, cache_vt):
    nb, t_new, width = q.shape
    lbuf = cache_kt.shape[-1]
    pad = lambda a: jnp.pad(a, ((0, 0), (0, SUBLANES - t_new), (0, 0)))
    new_spec = pl.BlockSpec((None, SUBLANES, width), lambda b: (b, 0, 0))
    cache_spec = pl.BlockSpec((None, N_HEADS, HEAD_DIM, lbuf), lambda b: (b, 0, 0, 0))
    return ([pad(q), pad(k_new), pad(v_new), cache_kt, cache_vt],
            [new_spec, new_spec, new_spec, cache_spec, cache_spec],
            jax.ShapeDtypeStruct((nb, t_new, width), F32),
            pl.BlockSpec((None, t_new, width), lambda b: (b, 0, 0)),
            (t_new, lbuf))


def _sample_attention(q, k_new, v_new, cache_kt, cache_vt):
    args, in_specs, out_shape, out_spec, (t_new, lbuf) = _sample_attention_specs(q, k_new, v_new, cache_kt, cache_vt)
    return pl.pallas_call(
        functools.partial(_sample_attn_kernel, t_new=t_new, lbuf=lbuf),
        grid=(q.shape[0],),
        in_specs=in_specs,
        out_specs=out_spec,
        out_shape=out_shape,
        compiler_params=pltpu.CompilerParams(
            dimension_semantics=("arbitrary",), vmem_limit_bytes=VMEM_LIMIT_BYTES),
        name="sample_attention",
    )(*args)


def _finish_kernel(x_ref, attn_ref, gated_ref, p_ref, wo_ref, g2_ref, wup_ref, wdn_ref,
                   gg_ref, wg_ref, wple_ref, gf_ref, y_ref, *nat_refs, apply_final):
    if nat_refs:
        nat_ref, per_class = nat_refs[0], attn_ref.shape[1]
        for r in range(N_CLASS):
            for s in range(PAIRS):
                nat_ref[s, pl.ds(r, per_class, stride=N_CLASS), :] = attn_ref[r, :, s * LANES:(s + 1) * LANES].astype(F32)
        attn = jnp.concatenate([nat_ref[s] for s in range(PAIRS)], axis=-1).astype(BF16)
    else:
        attn = attn_ref[...]
    mix = jnp.concatenate([attn, gated_ref[...]], axis=-1)
    h = x_ref[...] + jnp.dot(mix, wo_ref[...], preferred_element_type=F32)
    n2 = _rms(h, g2_ref[...]).astype(BF16)
    f = jnp.zeros_like(h)
    for c in range(D_FF // FF_CHUNK):
        cs = slice(c * FF_CHUNK, (c + 1) * FF_CHUNK)
        up = jnp.dot(n2, wup_ref[:, cs], preferred_element_type=F32)
        act = jnp.square(jnp.maximum(up, 0.0)).astype(BF16)
        f = f + jnp.dot(act, wdn_ref[cs, :], preferred_element_type=F32)
    h = h + f
    gate = jax.nn.sigmoid(jnp.dot(_rms(h, gg_ref[...]).astype(BF16), wg_ref[...], preferred_element_type=F32))
    ple = jnp.dot(p_ref[...].astype(BF16), wple_ref[...], preferred_element_type=F32)
    h = h + gate * ple
    y_ref[...] = _rms(h, gf_ref[...]) if apply_final else h


def _finish(x, attn, gated, p, wo, g2, wup, wdn, gg, wg, wple, gf, *, rows, apply_final):
    n_rows = x.shape[0]
    row_spec = lambda width: pl.BlockSpec((rows, width), lambda i: (i, 0))
    if attn.ndim == 3:
        attn_spec = pl.BlockSpec((N_CLASS, rows // N_CLASS, ATTN_WIDTH), lambda i: (0, i, 0))
        scratch = [pltpu.VMEM((PAIRS, rows, LANES), F32)]
    else:
        attn_spec, scratch = row_spec(ATTN_WIDTH), []
    return pl.pallas_call(
        functools.partial(_finish_kernel, apply_final=apply_final),
        grid=(n_rows // rows,),
        in_specs=[
            row_spec(D_MODEL), attn_spec, row_spec(GMLP_WIDTH), row_spec(PLE_DIM),
            _const_spec(wo.shape), _const_spec((1, D_MODEL)), _const_spec(wup.shape), _const_spec(wdn.shape),
            _const_spec((1, D_MODEL)), _const_spec(wg.shape), _const_spec(wple.shape), _const_spec((1, D_MODEL)),
        ],
        out_specs=row_spec(D_MODEL),
        out_shape=jax.ShapeDtypeStruct((n_rows, D_MODEL), F32),
        scratch_shapes=scratch,
        compiler_params=pltpu.CompilerParams(
            dimension_semantics=("arbitrary",), vmem_limit_bytes=VMEM_LIMIT_BYTES),
        name="finish",
    )(x, attn, gated, p, wo, g2, wup, wdn, gg, wg, wple, gf)


def _finish_casting_kernel(x_ref, attn_ref, gated_ref, p_ref, wo_ref, g2_ref, wup_ref, wdn_ref,
                           gg_ref, wg_ref, wple_ref, gf_ref,
                           y_ref, wo_b_ref, wup_b_ref, wdn_b_ref, wg_b_ref, wple_b_ref,
                           h_ref, n2_ref, f_ref, *, apply_final):
    c = pl.program_id(0)

    @pl.when(c == 0)
    def _():
        wo_b = wo_ref[...].astype(BF16)
        wo_b_ref[...] = wo_b
        mix = jnp.concatenate([attn_ref[...], gated_ref[...]], axis=-1)
        h = x_ref[...] + jnp.dot(mix, wo_b, preferred_element_type=F32)
        h_ref[...] = h
        n2_ref[...] = _rms(h, g2_ref[...]).astype(BF16)
        f_ref[...] = jnp.zeros(f_ref.shape, F32)

    wup_b = wup_ref[...].astype(BF16)
    wdn_b = wdn_ref[...].astype(BF16)
    wup_b_ref[...] = wup_b
    wdn_b_ref[...] = wdn_b
    up = jnp.dot(n2_ref[...], wup_b, preferred_element_type=F32)
    act = jnp.square(jnp.maximum(up, 0.0)).astype(BF16)
    f_ref[...] += jnp.dot(act, wdn_b, preferred_element_type=F32)

    @pl.when(c == pl.num_programs(0) - 1)
    def _():
        wg_b = wg_ref[...].astype(BF16)
        wple_b = wple_ref[...].astype(BF16)
        wg_b_ref[...] = wg_b
        wple_b_ref[...] = wple_b
        h = h_ref[...] + f_ref[...]
        gate = jax.nn.sigmoid(jnp.dot(_rms(h, gg_ref[...]).astype(BF16), wg_b, preferred_element_type=F32))
        ple = jnp.dot(p_ref[...].astype(BF16), wple_b, preferred_element_type=F32)
        h = h + gate * ple
        y_ref[...] = _rms(h, gf_ref[...]) if apply_final else h


def _finish_casting(x, attn, gated, p, wo, g2, wup, wdn, gg, wg, wple, gf, *, apply_final):
    rows = x.shape[0]
    bf = lambda w: jax.ShapeDtypeStruct(w.shape, BF16)
    up_spec = pl.BlockSpec((D_MODEL, FF_CHUNK), lambda c: (0, c))
    dn_spec = pl.BlockSpec((FF_CHUNK, D_MODEL), lambda c: (c, 0))
    whole = lambda a: _const_spec(a.shape)
    held = lambda a: pl.BlockSpec(a.shape, lambda c: (0,) * a.ndim)
    return pl.pallas_call(
        functools.partial(_finish_casting_kernel, apply_final=apply_final),
        grid=(D_FF // FF_CHUNK,),
        in_specs=[whole(x), whole(attn), whole(gated), whole(p), whole(wo), whole(g2), up_spec, dn_spec,
                  whole(gg), whole(wg), whole(wple), whole(gf)],
        out_specs=[held(x), held(wo), up_spec, dn_spec, held(wg), held(wple)],
        out_shape=[jax.ShapeDtypeStruct((rows, D_MODEL), F32), bf(wo), bf(wup), bf(wdn), bf(wg), bf(wple)],
        scratch_shapes=[pltpu.VMEM((rows, D_MODEL), F32), pltpu.VMEM((rows, D_MODEL), BF16),
                        pltpu.VMEM((rows, D_MODEL), F32)],
        compiler_params=pltpu.CompilerParams(
            dimension_semantics=("arbitrary",), vmem_limit_bytes=VMEM_LIMIT_BYTES),
        name="finish_casting",
    )(x, attn, gated, p, wo, g2, wup, wdn, gg, wg, wple, gf)


def _pair_spatial(w):
    g, l, _ = w.shape
    return w.reshape(g // 2, 2, l, l).transpose(0, 2, 1, 3).reshape(g // 2, l, 2 * l).astype(BF16)


def _bias_lanes(b):
    return jnp.repeat(b.T, GMLP_WIDTH // GMLP_GROUPS, axis=1)


def kernel(x_prompt, x_sample, cache_k, cache_v, p_prompt, p_sample, norm1_g, w_in, ln_v_g, ln_v_b,
           w_spatial, b_spatial, w_out, norm2_g, w_up, w_down, gate_norm_g, w_gate, w_ple, final_g):
    depth = w_in.shape[0]
    nbp, seq, _ = x_prompt.shape
    nbs, t_new, _ = x_sample.shape
    lbuf = cache_k.shape[2]
    keep = min(WINDOW_MAX, seq)
    assert nbp == 1 and seq % (N_CLASS * SPAN) == 0 and (nbs * t_new) % CHUNK == 0 and t_new <= CHUNK
    n_s = nbs * t_new

    row2 = lambda a: a.reshape(1, -1)

    hp = x_prompt.reshape(seq, D_MODEL)
    hs = x_sample.reshape(n_s, D_MODEL)
    nk_p, nv_p, nk_s, nv_s, nvc_s = [], [], [], [], []
    for i in range(depth):
        last = i == depth - 1
        w_in_b = w_in[i].astype(BF16)
        g2, gg, gf = row2(norm2_g[i]), row2(gate_norm_g[i]), row2(final_g)
        ln_g, ln_b = row2(ln_v_g[i]), row2(ln_v_b[i])

        wsp_s = w_spatial[i][:, :t_new, :t_new].reshape(-1)
        bsp_s = b_spatial[i][:, :t_new].reshape(-1)
        q_s, _, _, kf_s, vf_s, gated_s, vn = _project(hs, row2(norm1_g[i]), w_in_b, ln_g, ln_b, wsp_s, bsp_s,
                                                      rows=n_s, tail_rows=n_s, pos_base=PAST_LEN, pos_period=t_new,
                                                      class_major=False, emit_vn=True, seq_rows=t_new)
        r3 = lambda a: a.reshape(nbs, t_new, ATTN_WIDTH)
        sample_ops = (r3(q_s).astype(F32), r3(kf_s), r3(vf_s),
                      cache_k[i].transpose(0, 2, 3, 1), cache_v[i].transpose(0, 2, 3, 1))

        ride = seq // PROMPT_ROWS == nbs
        wsp_p = _pair_spatial(w_spatial[i])
        bsp_p = _bias_lanes(b_spatial[i])
        q, k, v, kf, vf, gated, *rode = _project(hp, row2(norm1_g[i]), w_in_b, ln_g, ln_b, wsp_p, bsp_p,
                                                 rows=PROMPT_ROWS, tail_rows=keep, pos_base=0, pos_period=PROMPT_ROWS,
                                                 class_major=True, emit_vn=False, tail_transposed=True,
                                                 sample_attn=sample_ops if ride else None)

        attn_s = rode[0] if ride else _sample_attention(*sample_ops)
        hs, wo_b, wup_b, wdn_b, wg_b, wple_b = _finish_casting(
            hs, attn_s.reshape(n_s, ATTN_WIDTH).astype(BF16), gated_s, p_sample[i].reshape(n_s, PLE_DIM),
            w_out[i], g2, w_up[i], w_down[i], gg, w_gate[i], w_ple[i], gf, apply_final=last)
        nk_s.append(kf_s.reshape(nbs, t_new, N_HEADS, HEAD_DIM))
        nv_s.append(vf_s.reshape(nbs, t_new, N_HEADS, HEAD_DIM))

        attn_cm = _prompt_attention(q, k, v)
        hp = _finish(hp, attn_cm, gated, p_prompt[i].reshape(seq, PLE_DIM),
                     wo_b, g2, wup_b, wdn_b, gg, wg_b, wple_b, gf, rows=PROMPT_ROWS, apply_final=last)
        from_cm = lambda a: a.reshape(N_HEADS, HEAD_DIM, keep).transpose(2, 0, 1).reshape(nbp, keep, N_HEADS, HEAD_DIM)
        nk_p.append(from_cm(kf))
        nv_p.append(from_cm(vf))
        nvc_s.append(vn.reshape(nbs, t_new, GMLP_WIDTH))

    return (hp.reshape(nbp, seq, D_MODEL), hs.reshape(nbs, t_new, D_MODEL),
            jnp.stack(nk_p), jnp.stack(nv_p), jnp.stack(nk_s), jnp.stack(nv_s), jnp.stack(nvc_s))
```

```python
import functools

import jax
import jax.numpy as jnp
from jax import lax
from jax.experimental import pallas as pl
from jax.experimental.pallas import tpu as pltpu

D_MODEL = 1024
N_HEADS = 8
HEAD_DIM = 64
ATTN_WIDTH = N_HEADS * HEAD_DIM
GMLP_GROUPS = 8
GMLP_WIDTH = 512
CHUNK = 128
DILATIONS = ((128, 1), (512, 4), (2048, 16))
WINDOW_MAX = 2048
PAST_LEN = 16384
ROT_DIM = HEAD_DIM // 4
ROPE_THETA = 500000.0
D_FF = 4 * D_MODEL
PLE_DIM = 256
EPS = 1e-6
NEG = -1e30
Q_SCALE = HEAD_DIM ** -0.5 * 1.4426950408889634

LANES = 128
VMEM_LIMIT_BYTES = 56 * 1024 * 1024

N_CLASS = 16
DEINT = 4
SPAN = 128
PAIRS = ATTN_WIDTH // LANES
ATTN_STEP_LANES = 2 * LANES
ATTN_GROUP = 2
SUBLANES = 8
PACKED_ROWS = 16
FF_CHUNK = 1024
PROMPT_ROWS = 512

BF16 = jnp.bfloat16
F32 = jnp.float32


def _rms(x, g):
    ms = jnp.mean(x * x, axis=-1, keepdims=True)
    return x * lax.rsqrt(ms + EPS) * g


def _const_spec(shape):
    nd = len(shape)
    return pl.BlockSpec(shape, lambda *_: (0,) * nd, pipeline_mode=pl.Buffered(1))


def _project_kernel(x_ref, g1_ref, w_ref, invf_ref, lng_ref, lnb_ref, wsp_ref, bsp_ref, *rest,
                    rows, tail_from, tail_transposed, pos_base, pos_period, class_major, emit_vn, seq_rows, rider,
                    cast_weights):
    rest = list(rest)
    rider_in = [rest.pop(0) for _ in range(5)] if rider else None
    q_ref = rest.pop(0)
    k_ref, v_ref = (rest.pop(0), rest.pop(0)) if class_major else (None, None)
    kf_ref, vf_ref, gated_ref = (rest.pop(0) for _ in range(3))
    vn_ref = rest.pop(0) if emit_vn else None
    if rider:
        _sample_attn_kernel(*rider_in, rest.pop(0), t_new=rider[0], lbuf=rider[1])
    if cast_weights:
        w_bf16 = w_ref[...].astype(BF16)
        rest.pop(0)[...] = w_bf16
        wcols = lambda a, b: w_bf16[:, a:b]
    else:
        wcols = lambda a, b: w_ref[:, a:b]
    cos_off_ref, sin_off_ref = rest.pop(0), rest.pop(0)
    zs_ref, zq_ref = (rest.pop(0), rest.pop(0)) if class_major else (None, None)
    mix_ref = rest.pop(0) if seq_rows else None
    i = pl.program_id(0)

    @pl.when(i == 0)
    def _():
        off = (lax.broadcasted_iota(jnp.int32, (rows, LANES), 0) % pos_period).astype(F32)
        ang = off * invf_ref[...]
        cos_off_ref[...] = jnp.cos(ang)
        sin_off_ref[...] = jnp.sin(ang)

    tile_stride = rows if pos_period == rows else 0
    base = (i * tile_stride).astype(F32) + pos_base
    base_ang = base * invf_ref[...]
    cb, sb = jnp.cos(base_ang), jnp.sin(base_ang)
    co, so = cos_off_ref[...], sin_off_ref[...]
    cosf = cb * co - sb * so
    sint = sb * co + cb * so
    head_lane = lax.broadcasted_iota(jnp.int32, (1, LANES), 1) % HEAD_DIM
    sina = jnp.where(head_lane < ROT_DIM // 2, -sint, 0.0)
    sinb = jnp.where((head_lane >= ROT_DIM // 2) & (head_lane < ROT_DIM), sint, 0.0)

    def rope(z):
        return z * cosf + pltpu.roll(z, LANES - ROT_DIM // 2, 1) * sina + pltpu.roll(z, ROT_DIM // 2, 1) * sinb

    xn = _rms(x_ref[...], g1_ref[...]).astype(BF16)
    zq = jnp.dot(xn, wcols(0, ATTN_WIDTH), preferred_element_type=F32)
    zk = jnp.dot(xn, wcols(ATTN_WIDTH, 2 * ATTN_WIDTH), preferred_element_type=F32)
    zv = jnp.dot(xn, wcols(2 * ATTN_WIDTH, 3 * ATTN_WIDTH), preferred_element_type=F32)
    slabs = [slice(s * LANES, (s + 1) * LANES) for s in range(PAIRS)]
    qr = [rope(zq[:, sl]) * Q_SCALE for sl in slabs]
    kr = [rope(zk[:, sl]) for sl in slabs]

    @pl.when(i >= tail_from)
    def _():
        if tail_transposed:
            kf_ref[...] = jnp.concatenate(kr, axis=-1).T
            vf_ref[...] = zv.T
        else:
            for s, sl in enumerate(slabs):
                kf_ref[:, sl] = kr[s]
            vf_ref[...] = zv

    if class_major:
        per_class = rows // N_CLASS
        for s, sl in enumerate(slabs):
            zs_ref[s] = qr[s]
            zs_ref[PAIRS + s] = kr[s]
            zs_ref[2 * PAIRS + s] = zv[:, sl]
        for n in range(3 * PAIRS):
            for c in range(DEINT):
                zq_ref[n, c] = zs_ref[n, pl.ds(c, rows // DEINT, stride=DEINT), :]
        for r in range(N_CLASS):
            for s, sl in enumerate(slabs):
                pick = lambda n: zq_ref[n, r % DEINT, pl.ds(r // DEINT, per_class, stride=DEINT), :].astype(BF16)
                q_ref[r, :, sl] = pick(s)
                k_ref[r, :, sl] = pick(PAIRS + s)
                v_ref[r, :, sl] = pick(2 * PAIRS + s)
    else:
        for s, sl in enumerate(slabs):
            q_ref[:, sl] = qr[s]

    o_u = 3 * ATTN_WIDTH
    u = jax.nn.gelu(jnp.dot(xn, wcols(o_u, o_u + GMLP_WIDTH), preferred_element_type=F32))
    vc = jax.nn.gelu(jnp.dot(xn, wcols(o_u + GMLP_WIDTH, o_u + 2 * GMLP_WIDTH), preferred_element_type=F32))
    mu = jnp.mean(vc, axis=-1, keepdims=True)
    cen = vc - mu
    var = jnp.mean(cen * cen, axis=-1, keepdims=True)
    vn = cen * lax.rsqrt(var + EPS) * lng_ref[...] + lnb_ref[...]
    if emit_vn:
        vn_ref[...] = vn
    if seq_rows:
        nseq = rows // seq_rows
        in_first = lax.broadcasted_iota(jnp.int32, (1, LANES), 1) < HEAD_DIM
        for s in range(GMLP_WIDTH // LANES):
            ga, gb = 2 * s, 2 * s + 1
            mix_ref[0, s] = vn[:, s * LANES:(s + 1) * LANES]
            xs = [mix_ref[0, s, pl.ds(j, nseq, stride=seq_rows), :] for j in range(seq_rows)]
            for r in range(seq_rows):
                acc = jnp.broadcast_to(
                    jnp.where(in_first, bsp_ref[ga * seq_rows + r], bsp_ref[gb * seq_rows + r]), (nseq, LANES))
                for j in range(r + 1):
                    w = jnp.where(in_first, wsp_ref[(ga * seq_rows + r) * seq_rows + j],
                                  wsp_ref[(gb * seq_rows + r) * seq_rows + j])
                    acc = acc + w * xs[j]
                mix_ref[1, s, pl.ds(r, nseq, stride=seq_rows), :] = acc
        mixed = jnp.concatenate([mix_ref[1, s] for s in range(GMLP_WIDTH // LANES)], axis=-1)
        gated_ref[...] = (u * mixed).astype(BF16)
        return
    vnb = vn.astype(BF16)

    row = lax.broadcasted_iota(jnp.int32, (CHUNK, 2 * CHUNK), 0)
    col = lax.broadcasted_iota(jnp.int32, (CHUNK, 2 * CHUNK), 1)
    tril = (col % CHUNK) <= row
    lane = lax.broadcasted_iota(jnp.int32, (CHUNK, LANES), 1)
    first = lane < HEAD_DIM
    zero = jnp.zeros((CHUNK, LANES), BF16)
    for s in range(GMLP_WIDTH // LANES):
        sl = slice(s * LANES, (s + 1) * LANES)
        wp = jnp.where(tril, wsp_ref[s], jnp.zeros((), BF16))
        bias = bsp_ref[:, sl]
        for c in range(rows // CHUNK):
            rs = slice(c * CHUNK, (c + 1) * CHUNK)
            vblk = vnb[rs, sl]
            rhs = jnp.concatenate([jnp.where(first, vblk, zero), jnp.where(first, zero, vblk)], axis=0)
            mixed = jnp.dot(wp, rhs, preferred_element_type=F32) + bias
            gated_ref[rs, sl] = (u[rs, sl] * mixed).astype(BF16)


def _project(x, g1, w_in_b, ln_g, ln_b, wsp, bsp, *, rows, tail_rows, pos_base, pos_period, class_major, emit_vn,
             seq_rows=None, tail_transposed=False, sample_attn=None, cast_weights=False):
    n_rows = x.shape[0]
    grid = n_rows // rows
    tail_from = (n_rows - tail_rows) // rows
    row_spec = lambda width: pl.BlockSpec((rows, width), lambda i: (i, 0))
    if tail_transposed:
        tail_shape = jax.ShapeDtypeStruct((ATTN_WIDTH, tail_rows), F32)
        tail_spec = pl.BlockSpec((ATTN_WIDTH, rows), lambda i: (0, jnp.maximum(i - tail_from, 0)))
    else:
        tail_shape = jax.ShapeDtypeStruct((tail_rows, ATTN_WIDTH), F32)
        tail_spec = pl.BlockSpec((rows, ATTN_WIDTH), lambda i: (jnp.maximum(i - tail_from, 0), 0))
    if class_major:
        qkv_shapes = [jax.ShapeDtypeStruct((N_CLASS, n_rows // N_CLASS, ATTN_WIDTH), BF16)] * 3
        qkv_specs = [pl.BlockSpec((N_CLASS, rows // N_CLASS, ATTN_WIDTH), lambda i: (0, i, 0))] * 3
    else:
        qkv_shapes = [jax.ShapeDtypeStruct((n_rows, ATTN_WIDTH), F32)]
        qkv_specs = [row_spec(ATTN_WIDTH)]
    out_shape = qkv_shapes + [
        tail_shape,
        tail_shape,
        jax.ShapeDtypeStruct((n_rows, GMLP_WIDTH), BF16),
    ]
    out_specs = qkv_specs + [tail_spec] * 2 + [row_spec(GMLP_WIDTH)]
    if emit_vn:
        out_shape.append(jax.ShapeDtypeStruct((n_rows, GMLP_WIDTH), F32))
        out_specs.append(row_spec(GMLP_WIDTH))
    scratch = [pltpu.VMEM((rows, LANES), F32), pltpu.VMEM((rows, LANES), F32)]
    if class_major:
        scratch.append(pltpu.VMEM((3 * PAIRS, rows, LANES), F32))
        scratch.append(pltpu.VMEM((3 * PAIRS, DEINT, rows // DEINT, LANES), F32))
    if seq_rows:
        scratch.append(pltpu.VMEM((2, GMLP_WIDTH // LANES, rows, LANES), F32))
        spatial_specs = [pl.BlockSpec(memory_space=pltpu.SMEM)] * 2
    else:
        spatial_specs = [_const_spec(wsp.shape), _const_spec(bsp.shape)]
    rider, rider_specs, rider_args = None, [], []
    if sample_attn is not None:
        rider_args, rider_specs, rider_out_shape, rider_out_spec, rider = _sample_attention_specs(*sample_attn)
        assert rider_args[0].shape[0] == grid
        out_shape.append(rider_out_shape)
        out_specs.append(rider_out_spec)
    if cast_weights:
        out_shape.append(jax.ShapeDtypeStruct(w_in_b.shape, BF16))
        out_specs.append(pl.BlockSpec(w_in_b.shape, lambda i: (0, 0)))
    inv_freq = ROPE_THETA ** (-jnp.arange(0, ROT_DIM, 2, dtype=F32) / ROT_DIM)
    per_head = jnp.concatenate([inv_freq, inv_freq, jnp.zeros((HEAD_DIM - ROT_DIM,), F32)])
    invf = jnp.tile(per_head, LANES // HEAD_DIM).reshape(1, LANES)
    return pl.pallas_call(
        functools.partial(_project_kernel, rows=rows, tail_from=tail_from, tail_transposed=tail_transposed,
                          pos_base=float(pos_base),
                          pos_period=pos_period, class_major=class_major, emit_vn=emit_vn, seq_rows=seq_rows,
                          rider=rider, cast_weights=cast_weights),
        grid=(grid,),
        in_specs=[
            row_spec(D_MODEL),
            _const_spec((1, D_MODEL)),
            _const_spec(w_in_b.shape),
            _const_spec((1, LANES)),
            _const_spec((1, GMLP_WIDTH)),
            _const_spec((1, GMLP_WIDTH)),
            *spatial_specs,
            *rider_specs,
        ],
        out_specs=out_specs,
        out_shape=out_shape,
        scratch_shapes=scratch,
        compiler_params=pltpu.CompilerParams(
            dimension_semantics=("arbitrary",), vmem_limit_bytes=VMEM_LIMIT_BYTES),
        name="project",
    )(x, g1, w_in_b, invf, ln_g, ln_b, wsp, bsp, *rider_args)


def _prompt_attn_kernel(q_ref, k_ref, v_ref, o_ref, kc_ref, vc_ref, q32_ref, acc_ref, m_ref, l_ref,
                        s_ref, p_ref, ms_ref):
    i = pl.program_id(1)
    rows = q_ref.shape[1]
    width = q_ref.shape[2]
    npair = width // LANES
    half = SUBLANES

    @pl.when((pl.program_id(0) == 0) & (i == 0))
    def _():
        p_ref[1] = jnp.zeros(p_ref.shape[1:], BF16)
        ms_ref[1] = jnp.zeros(ms_ref.shape[1:], F32)

    @pl.when(i == 0)
    def _():
        kc_ref[0, :, 0:rows, :] = jnp.zeros((N_CLASS, rows, width), BF16)
        vc_ref[0, :, 0:rows, :] = jnp.zeros((N_CLASS, rows, width), BF16)

    @pl.when(i > 0)
    def _():
        kc_ref[0, :, 0:rows, :] = kc_ref[0, :, rows:2 * rows, :]
        vc_ref[0, :, 0:rows, :] = vc_ref[0, :, rows:2 * rows, :]

    kc_ref[0, :, rows:2 * rows, :] = k_ref[...]
    vc_ref[0, :, rows:2 * rows, :] = v_ref[...]
    lo = rows - PACKED_ROWS
    for ref in (kc_ref, vc_ref):
        tail = ref[0, :, lo:2 * rows, :].astype(F32)
        ref[1, :, lo:2 * rows - PACKED_ROWS, :] = tail[:, half:half + rows, :].astype(BF16)
    q32_ref[...] = q_ref[...].astype(F32)
    has_prev = i > 0

    def lanes(pr):
        return slice(pr * LANES, (pr + 1) * LANES)

    mq, nk = rows, 2 * rows
    olane = lax.broadcasted_iota(jnp.int32, (mq, LANES), 1) < HEAD_DIM
    qzero = jnp.zeros((mq, LANES), BF16)
    ones_v = jnp.ones((nk, LANES), BF16)
    nt = (((1,), (1,)), ((), ()))

    def run(n_units, q_of, k_of, v_of, mask_of, state_of, put, first, last):
        units = lambda t: [t * ATTN_GROUP + j for j in range(ATTN_GROUP)]

        def scores(t, slot):
            for j, u in enumerate(units(t)):
                for pr in range(npair):
                    q = q_of(u, pr)
                    qq = jnp.concatenate([jnp.where(olane, q, qzero), jnp.where(olane, qzero, q)], axis=0)
                    s_ref[slot, j, pr] = lax.dot_general(qq, k_of(u, pr), nt, preferred_element_type=F32)

        def softmax(t, slot):
            for j, u in enumerate(units(t)):
                mask = mask_of(u)
                for pr in range(npair):
                    ms = []
                    for hh in range(2):
                        hs = slice(hh * mq, (hh + 1) * mq)
                        s = jnp.where(mask, s_ref[slot, j, pr, hs, :], NEG)
                        m = jnp.max(s, axis=-1, keepdims=True)
                        p_ref[slot, j, pr, hs, :] = jnp.exp2(s - m).astype(BF16)
                        ms.append(m)
                    ms_ref[slot, j, pr] = jnp.where(olane, ms[0], ms[1])

        def values(t, slot, valid):
            for j, u in enumerate(units(t)):
                for pr in range(npair):
                    vv = jnp.concatenate([v_of(u, pr), ones_v], axis=1)
                    r = jnp.dot(p_ref[slot, j, pr], vv, preferred_element_type=F32)
                    pv = jnp.where(olane, r[0:mq, 0:LANES], r[mq:2 * mq, 0:LANES])
                    l = jnp.where(olane, r[0:mq, LANES:2 * LANES], r[mq:2 * mq, LANES:2 * LANES])
                    m = ms_ref[slot, j, pr]
                    if not first:
                        acc0, m0, l0 = state_of(u, pr)
                        m_new = jnp.maximum(m0, m)
                        a = jnp.exp2(m0 - m_new)
                        b = jnp.exp2(m - m_new)
                        pv = acc0 * a + pv * b
                        l = l0 * a + l * b
                        m = m_new
                        if last:
                            pv = pv / l
                        if valid is not None:
                            pv, m, l = jnp.where(valid, pv, acc0), jnp.where(valid, m, m0), jnp.where(valid, l, l0)
                    put(u, pr, pv if last else (pv, m, l))

        n_trips = n_units // ATTN_GROUP
        scores(jnp.int32(0), 0)

        def trip(t, carry):
            slot = t % 2
            values(jnp.maximum(t - 1, 0), 1 - slot, t > 0)
            softmax(t, slot)
            scores(jnp.minimum(t + 1, n_trips - 1), 1 - slot)
            return carry

        lax.fori_loop(0, n_trips, trip, 0)
        values(jnp.int32(n_trips - 1), (n_trips - 1) % 2, None)

    def put_state(slabs_of, r0_of, n):
        def put(u, pr, res):
            for a, sb in enumerate(slabs_of(u)):
                rs = slice(a * n, (a + 1) * n)
                acc_ref[sb, pl.ds(r0_of(u), n), lanes(pr)] = res[0][rs]
                m_ref[sb, pl.ds(r0_of(u), n), lanes(pr)] = res[1][rs]
                l_ref[sb, pl.ds(r0_of(u), n), lanes(pr)] = res[2][rs]
        return put

    def gather(ref, slabs, r0, n, pr):
        return jnp.concatenate([ref[sb, pl.ds(r0, n), lanes(pr)] for sb in slabs], axis=0)

    state_refs = (acc_ref, m_ref, l_ref)
    kplain, vplain = kc_ref.at[0], vc_ref.at[0]

    qi = lax.broadcasted_iota(jnp.int32, (mq, nk), 0)
    kj = lax.broadcasted_iota(jnp.int32, (mq, nk), 1)
    diff = qi + rows - kj
    mask16 = (diff >= 0) & (diff <= SPAN) & ((kj >= rows) | has_prev)
    run(N_CLASS,
        q_of=lambda u, pr: q_ref[u, :, lanes(pr)],
        k_of=lambda u, pr: kplain[u, :, lanes(pr)],
        v_of=lambda u, pr: vplain[u, :, lanes(pr)],
        mask_of=lambda u: mask16, state_of=None,
        put=put_state(lambda u: [u], lambda u: 0, rows), first=True, last=False)

    n4 = N_CLASS // 4
    qb = rows // n4
    d4 = 4 * (qi % qb - kj % (2 * qb) + qb) + (qi // qb - kj // (2 * qb))
    band4 = (d4 >= 0) & (d4 <= SPAN)
    cur4 = kj % (2 * qb) >= qb
    slabs4 = lambda u: [u // n4 + 4 * a for a in range(n4)]
    q0_4 = lambda u: pl.multiple_of((u % n4) * qb, qb)
    k0_4 = lambda u: pl.multiple_of(rows - qb + (u % n4) * qb, qb)
    run(N_CLASS,
        q_of=lambda u, pr: gather(q_ref, slabs4(u), q0_4(u), qb, pr),
        k_of=lambda u, pr: gather(kplain, slabs4(u), k0_4(u), 2 * qb, pr),
        v_of=lambda u, pr: gather(vplain, slabs4(u), k0_4(u), 2 * qb, pr),
        mask_of=lambda u: band4 & (cur4 | has_prev | (u % n4 > 0)),
        state_of=lambda u, pr: tuple(gather(ref, slabs4(u), q0_4(u), qb, pr) for ref in state_refs),
        put=put_state(slabs4, q0_4, qb), first=False, last=False)

    d1 = N_CLASS * (qi % half - kj % (2 * half) + half) + (qi // half - kj // (2 * half))
    band1 = (d1 >= 0) & (d1 <= SPAN)
    cur1 = kj % (2 * half) >= half
    every = list(range(N_CLASS))
    q0_1 = lambda u: pl.multiple_of(u * half, half)
    kcopy = lambda u: 1 - u % 2
    k0_1 = lambda u: pl.multiple_of(((rows - half + u * half) // PACKED_ROWS) * PACKED_ROWS, PACKED_ROWS)

    def put_out(u, pr, res):
        for sb in every:
            acc_ref[sb, pl.ds(q0_1(u), half), lanes(pr)] = res[sb * half:(sb + 1) * half]

    run(rows // half,
        q_of=lambda u, pr: gather(q32_ref, every, q0_1(u), half, pr).astype(BF16),
        k_of=lambda u, pr: gather(kc_ref.at[kcopy(u)], every, k0_1(u), 2 * half, pr),
        v_of=lambda u, pr: gather(vc_ref.at[kcopy(u)], every, k0_1(u), 2 * half, pr),
        mask_of=lambda u: band1 & (cur1 | has_prev | (u > 0)),
        state_of=lambda u, pr: tuple(gather(ref, every, q0_1(u), half, pr) for ref in state_refs),
        put=put_out, first=False, last=True)
    o_ref[...] = acc_ref[...].astype(BF16)


def _prompt_attention(q_cm, k_cm, v_cm):
    n_slab_rows = q_cm.shape[1]
    npair = ATTN_STEP_LANES // LANES
    spec = pl.BlockSpec((N_CLASS, SPAN, ATTN_STEP_LANES), lambda h, i: (0, i, h))
    return pl.pallas_call(
        _prompt_attn_kernel,
        grid=(ATTN_WIDTH // ATTN_STEP_LANES, n_slab_rows // SPAN),
        in_specs=[spec, spec, spec],
        out_specs=spec,
        out_shape=jax.ShapeDtypeStruct(q_cm.shape, BF16),
        scratch_shapes=[
            pltpu.VMEM((2, N_CLASS, 2 * SPAN, ATTN_STEP_LANES), BF16),
            pltpu.VMEM((2, N_CLASS, 2 * SPAN, ATTN_STEP_LANES), BF16),
            pltpu.VMEM((N_CLASS, SPAN, ATTN_STEP_LANES), F32),
            pltpu.VMEM((N_CLASS, SPAN, ATTN_STEP_LANES), F32),
            pltpu.VMEM((N_CLASS, SPAN, ATTN_STEP_LANES), F32),
            pltpu.VMEM((N_CLASS, SPAN, ATTN_STEP_LANES), F32),
            pltpu.VMEM((2, ATTN_GROUP, npair, 2 * SPAN, 2 * SPAN), F32),
            pltpu.VMEM((2, ATTN_GROUP, npair, 2 * SPAN, 2 * SPAN), BF16),
            pltpu.VMEM((2, ATTN_GROUP, npair, SPAN, LANES), F32),
        ],
        compiler_params=pltpu.CompilerParams(
            dimension_semantics=("arbitrary", "arbitrary"), vmem_limit_bytes=VMEM_LIMIT_BYTES),
        name="prompt_attention",
    )(q_cm, k_cm, v_cm)


def _sample_attn_kernel(q_ref, kn_ref, vn_ref, kt_ref, vt_ref, o_ref, *, t_new, lbuf):
    width = ATTN_WIDTH
    nrow = t_new * N_HEADS
    q = q_ref[...]
    kn = kn_ref[...]
    vnew = vn_ref[...]

    row = lax.broadcasted_iota(jnp.int32, (nrow, width), 0)
    lane = lax.broadcasted_iota(jnp.int32, (nrow, width), 1)
    hmask = (lane // HEAD_DIM) == (row % N_HEADS)
    qtok = jnp.concatenate([jnp.broadcast_to(q[j:j + 1, :], (N_HEADS, width)) for j in range(t_new)], axis=0)
    qrows = jnp.where(hmask, qtok, 0.0)

    kt = kt_ref[...].reshape(width, lbuf).astype(BF16)
    vt = vt_ref[...].reshape(width, lbuf).astype(BF16)
    s = jnp.dot(qrows.astype(BF16), kt, preferred_element_type=F32)

    def reach_count(delta):
        cnt = jnp.zeros(delta.shape, F32)
        for window, dil in DILATIONS:
            hit = (delta >= 0) & (delta % dil == 0) & (delta <= window)
            cnt = cnt + jnp.where(hit, 1.0, 0.0)
        return cnt

    pos = lax.broadcasted_iota(jnp.int32, (nrow, lbuf), 1)
    tok = lax.broadcasted_iota(jnp.int32, (nrow, lbuf), 0) // N_HEADS
    cnt = reach_count(lbuf + tok - pos)
    tok1 = tok[:, 0:1]
    cnew = [reach_count(tok1 - j) for j in range(t_new)]
    snew = [jnp.sum(qrows * kn[j:j + 1, :], axis=-1, keepdims=True) for j in range(t_new)]

    s = jnp.where(cnt > 0.0, s, NEG)
    m = jnp.max(s, axis=-1, keepdims=True)
    for j in range(t_new):
        m = jnp.maximum(m, jnp.where(cnew[j] > 0.0, snew[j], NEG))
    e = jnp.exp2(s - m) * cnt
    l = jnp.sum(e, axis=-1, keepdims=True)
    o = lax.dot_general(e.astype(BF16), vt, (((1,), (1,)), ((), ())), preferred_element_type=F32)
    for j in range(t_new):
        ej = jnp.where(cnew[j] > 0.0, jnp.exp2(snew[j] - m), 0.0) * cnew[j]
        l = l + ej
        o = o + ej * vnew[j:j + 1, :]
    o = jnp.where(hmask, o / l, 0.0)
    for j in range(t_new):
        o_ref[j:j + 1, :] = jnp.sum(o[j * N_HEADS:(j + 1) * N_HEADS, :], axis=0, keepdims=True)


def _sample_attention_specs(q, k_new, v_new, cache_kt, cache_vt):
    nb, t_new, width = q.shape
    lbuf = cache_kt.shape[-1]
    new_spec = pl.BlockSpec((None, t_new, width), lambda b: (b, 0, 0))
    cache_spec = pl.BlockSpec((None, N_HEADS, HEAD_DIM, lbuf), lambda b: (b, 0, 0, 0))
    return ([q, k_new, v_new, cache_kt, cache_vt],
            [new_spec, new_spec, new_spec, cache_spec, cache_spec],
            jax.ShapeDtypeStruct((nb, t_new, width), F32),
            pl.BlockSpec((None, t_new, width), lambda b: (b, 0, 0)),
            (t_new, lbuf))


def _sample_attention(q, k_new, v_new, cache_kt, cache_vt):
    args, in_specs, out_shape, out_spec, (t_new, lbuf) = _sample_attention_specs(q, k_new, v_new, cache_kt, cache_vt)
    return pl.pallas_call(
        functools.partial(_sample_attn_kernel, t_new=t_new, lbuf=lbuf),
        grid=(q.shape[0],),
        in_specs=in_specs,
        out_specs=out_spec,
        out_shape=out_shape,
        compiler_params=pltpu.CompilerParams(
            dimension_semantics=("arbitrary",), vmem_limit_bytes=VMEM_LIMIT_BYTES),
        name="sample_attention",
    )(*args)


def _finish_kernel(x_ref, attn_ref, gated_ref, p_ref, wo_ref, g2_ref, wup_ref, wdn_ref,
                   gg_ref, wg_ref, wple_ref, gf_ref, y_ref, nat_ref, *, apply_final):
    per_class = attn_ref.shape[1]
    for r in range(N_CLASS):
        for s in range(PAIRS):
            nat_ref[s, pl.ds(r, per_class, stride=N_CLASS), :] = attn_ref[r, :, s * LANES:(s + 1) * LANES].astype(F32)
    attn = jnp.concatenate([nat_ref[s] for s in range(PAIRS)], axis=-1).astype(BF16)
    mix = jnp.concatenate([attn, gated_ref[...]], axis=-1)
    h = x_ref[...] + jnp.dot(mix, wo_ref[...], preferred_element_type=F32)
    n2 = _rms(h, g2_ref[...]).astype(BF16)
    f = jnp.zeros_like(h)
    for c in range(D_FF // FF_CHUNK):
        cs = slice(c * FF_CHUNK, (c + 1) * FF_CHUNK)
        up = jnp.dot(n2, wup_ref[:, cs], preferred_element_type=F32)
        act = jnp.square(jnp.maximum(up, 0.0)).astype(BF16)
        f = f + jnp.dot(act, wdn_ref[cs, :], preferred_element_type=F32)
    h = h + f
    gate = jax.nn.sigmoid(jnp.dot(_rms(h, gg_ref[...]).astype(BF16), wg_ref[...], preferred_element_type=F32))
    ple = jnp.dot(p_ref[...].astype(BF16), wple_ref[...], preferred_element_type=F32)
    h = h + gate * ple
    y_ref[...] = _rms(h, gf_ref[...]) if apply_final else h


def _finish(x, attn, gated, p, wo, g2, wup, wdn, gg, wg, wple, gf, *, rows, apply_final):
    n_rows = x.shape[0]
    row_spec = lambda width: pl.BlockSpec((rows, width), lambda i: (i, 0))
    attn_spec = pl.BlockSpec((N_CLASS, rows // N_CLASS, ATTN_WIDTH), lambda i: (0, i, 0))
    scratch = [pltpu.VMEM((PAIRS, rows, LANES), F32)]
    return pl.pallas_call(
        functools.partial(_finish_kernel, apply_final=apply_final),
        grid=(n_rows // rows,),
        in_specs=[
            row_spec(D_MODEL), attn_spec, row_spec(GMLP_WIDTH), row_spec(PLE_DIM),
            _const_spec(wo.shape), _const_spec((1, D_MODEL)), _const_spec(wup.shape), _const_spec(wdn.shape),
            _const_spec((1, D_MODEL)), _const_spec(wg.shape), _const_spec(wple.shape), _const_spec((1, D_MODEL)),
        ],
        out_specs=row_spec(D_MODEL),
        out_shape=jax.ShapeDtypeStruct((n_rows, D_MODEL), F32),
        scratch_shapes=scratch,
        compiler_params=pltpu.CompilerParams(
            dimension_semantics=("arbitrary",), vmem_limit_bytes=VMEM_LIMIT_BYTES),
        name="finish",
    )(x, attn, gated, p, wo, g2, wup, wdn, gg, wg, wple, gf)


def _finish_casting_kernel(x_ref, attn_ref, gated_ref, p_ref, wo_ref, g2_ref, wup_ref, wdn_ref,
                           gg_ref, wg_ref, wple_ref, gf_ref,
                           y_ref, wo_b_ref, wup_b_ref, wdn_b_ref, wg_b_ref, wple_b_ref,
                           h_ref, n2_ref, f_ref, *, apply_final):
    c = pl.program_id(0)

    @pl.when(c == 0)
    def _():
        wo_b = wo_ref[...].astype(BF16)
        wo_b_ref[...] = wo_b
        mix = jnp.concatenate([attn_ref[...], gated_ref[...]], axis=-1)
        h = x_ref[...] + jnp.dot(mix, wo_b, preferred_element_type=F32)
        h_ref[...] = h
        n2_ref[...] = _rms(h, g2_ref[...]).astype(BF16)
        f_ref[...] = jnp.zeros(f_ref.shape, F32)

    wup_b = wup_ref[...].astype(BF16)
    wdn_b = wdn_ref[...].astype(BF16)
    wup_b_ref[...] = wup_b
    wdn_b_ref[...] = wdn_b
    up = jnp.dot(n2_ref[...], wup_b, preferred_element_type=F32)
    act = jnp.square(jnp.maximum(up, 0.0)).astype(BF16)
    f_ref[...] += jnp.dot(act, wdn_b, preferred_element_type=F32)

    @pl.when(c == pl.num_programs(0) - 1)
    def _():
        wg_b = wg_ref[...].astype(BF16)
        wple_b = wple_ref[...].astype(BF16)
        wg_b_ref[...] = wg_b
        wple_b_ref[...] = wple_b
        h = h_ref[...] + f_ref[...]
        gate = jax.nn.sigmoid(jnp.dot(_rms(h, gg_ref[...]).astype(BF16), wg_b, preferred_element_type=F32))
        ple = jnp.dot(p_ref[...].astype(BF16), wple_b, preferred_element_type=F32)
        h = h + gate * ple
        y_ref[...] = _rms(h, gf_ref[...]) if apply_final else h


def _finish_casting(x, attn, gated, p, wo, g2, wup, wdn, gg, wg, wple, gf, *, apply_final):
    rows = x.shape[0]
    bf = lambda w: jax.ShapeDtypeStruct(w.shape, BF16)
    up_spec = pl.BlockSpec((D_MODEL, FF_CHUNK), lambda c: (0, c))
    dn_spec = pl.BlockSpec((FF_CHUNK, D_MODEL), lambda c: (c, 0))
    whole = lambda a: _const_spec(a.shape)
    held = lambda a: pl.BlockSpec(a.shape, lambda c: (0,) * a.ndim)
    return pl.pallas_call(
        functools.partial(_finish_casting_kernel, apply_final=apply_final),
        grid=(D_FF // FF_CHUNK,),
        in_specs=[whole(x), whole(attn), whole(gated), whole(p), whole(wo), whole(g2), up_spec, dn_spec,
                  whole(gg), whole(wg), whole(wple), whole(gf)],
        out_specs=[held(x), held(wo), up_spec, dn_spec, held(wg), held(wple)],
        out_shape=[jax.ShapeDtypeStruct((rows, D_MODEL), F32), bf(wo), bf(wup), bf(wdn), bf(wg), bf(wple)],
        scratch_shapes=[pltpu.VMEM((rows, D_MODEL), F32), pltpu.VMEM((rows, D_MODEL), BF16),
                        pltpu.VMEM((rows, D_MODEL), F32)],
        compiler_params=pltpu.CompilerParams(
            dimension_semantics=("arbitrary",), vmem_limit_bytes=VMEM_LIMIT_BYTES),
        name="finish_casting",
    )(x, attn, gated, p, wo, g2, wup, wdn, gg, wg, wple, gf)


def _pair_spatial(w):
    g, l, _ = w.shape
    return w.reshape(g // 2, 2, l, l).transpose(0, 2, 1, 3).reshape(g // 2, l, 2 * l).astype(BF16)


def _bias_lanes(b):
    return jnp.repeat(b.T, GMLP_WIDTH // GMLP_GROUPS, axis=1)


def kernel(x_prompt, x_sample, cache_k, cache_v, p_prompt, p_sample, norm1_g, w_in, ln_v_g, ln_v_b,
           w_spatial, b_spatial, w_out, norm2_g, w_up, w_down, gate_norm_g, w_gate, w_ple, final_g):
    depth = w_in.shape[0]
    nbp, seq, _ = x_prompt.shape
    nbs, t_new, _ = x_sample.shape
    lbuf = cache_k.shape[2]
    keep = min(WINDOW_MAX, seq)
    assert nbp == 1 and seq % (N_CLASS * SPAN) == 0 and (nbs * t_new) % CHUNK == 0 and t_new <= CHUNK
    n_s = nbs * t_new

    row2 = lambda a: a.reshape(1, -1)

    hp = x_prompt.reshape(seq, D_MODEL)
    hs = x_sample.reshape(n_s, D_MODEL)
    nk_p, nv_p, nk_s, nv_s, nvc_s = [], [], [], [], []
    for i in range(depth):
        last = i == depth - 1
        g2, gg, gf = row2(norm2_g[i]), row2(gate_norm_g[i]), row2(final_g)
        ln_g, ln_b = row2(ln_v_g[i]), row2(ln_v_b[i])

        wsp_s = w_spatial[i][:, :t_new, :t_new].reshape(-1)
        bsp_s = b_spatial[i][:, :t_new].reshape(-1)
        q_s, kf_s, vf_s, gated_s, vn, w_in_b = _project(hs, row2(norm1_g[i]), w_in[i], ln_g, ln_b, wsp_s, bsp_s,
                                                        rows=n_s, tail_rows=n_s, pos_base=PAST_LEN, pos_period=t_new,
                                                        class_major=False, emit_vn=True, seq_rows=t_new,
                                                        cast_weights=True)
        r3 = lambda a: a.reshape(nbs, t_new, ATTN_WIDTH)
        sample_ops = (r3(q_s), r3(kf_s), r3(vf_s),
                      cache_k[i].transpose(0, 2, 3, 1), cache_v[i].transpose(0, 2, 3, 1))

        ride = seq // PROMPT_ROWS == nbs
        wsp_p = _pair_spatial(w_spatial[i])
        bsp_p = _bias_lanes(b_spatial[i])
        q, k, v, kf, vf, gated, *rode = _project(hp, row2(norm1_g[i]), w_in_b, ln_g, ln_b, wsp_p, bsp_p,
                                                 rows=PROMPT_ROWS, tail_rows=keep, pos_base=0, pos_period=PROMPT_ROWS,
                                                 class_major=True, emit_vn=False, tail_transposed=True,
                                                 sample_attn=sample_ops if ride else None)

        attn_s = rode[0] if ride else _sample_attention(*sample_ops)
        hs, wo_b, wup_b, wdn_b, wg_b, wple_b = _finish_casting(
            hs, attn_s.reshape(n_s, ATTN_WIDTH).astype(BF16), gated_s, p_sample[i].reshape(n_s, PLE_DIM),
            w_out[i], g2, w_up[i], w_down[i], gg, w_gate[i], w_ple[i], gf, apply_final=last)
        nk_s.append(kf_s.reshape(nbs, t_new, N_HEADS, HEAD_DIM))
        nv_s.append(vf_s.reshape(nbs, t_new, N_HEADS, HEAD_DIM))

        attn_cm = _prompt_attention(q, k, v)
        hp = _finish(hp, attn_cm, gated, p_prompt[i].reshape(seq, PLE_DIM),
                     wo_b, g2, wup_b, wdn_b, gg, wg_b, wple_b, gf, rows=PROMPT_ROWS, apply_final=last)
        from_cm = lambda a: a.reshape(N_HEADS, HEAD_DIM, keep).transpose(2, 0, 1).reshape(nbp, keep, N_HEADS, HEAD_DIM)
        nk_p.append(from_cm(kf))
        nv_p.append(from_cm(vf))
        nvc_s.append(vn.reshape(nbs, t_new, GMLP_WIDTH))

    return (hp.reshape(nbp, seq, D_MODEL), hs.reshape(nbs, t_new, D_MODEL),
            jnp.stack(nk_p), jnp.stack(nv_p), jnp.stack(nk_s), jnp.stack(nv_s), jnp.stack(nvc_s))
```

```python
import functools

import jax
import jax.numpy as jnp
from jax import lax
from jax.experimental import pallas as pl
from jax.experimental.pallas import tpu as pltpu

D_MODEL = 1024
N_HEADS = 8
HEAD_DIM = 64
ATTN_WIDTH = N_HEADS * HEAD_DIM
GMLP_GROUPS = 8
GMLP_WIDTH = 512
CHUNK = 128
DILATIONS = ((128, 1), (512, 4), (2048, 16))
WINDOW_MAX = 2048
PAST_LEN = 16384
ROT_DIM = HEAD_DIM // 4
ROPE_THETA = 500000.0
D_FF = 4 * D_MODEL
PLE_DIM = 256
EPS = 1e-6
NEG = -1e30
Q_SCALE = HEAD_DIM ** -0.5 * 1.4426950408889634

LANES = 128
VMEM_LIMIT_BYTES = 56 * 1024 * 1024

N_CLASS = 16
DEINT = 4
SPAN = 128
PAIRS = ATTN_WIDTH // LANES
ATTN_STEP_LANES = 2 * LANES
ATTN_GROUP = 2
SUBLANES = 8
PACKED_ROWS = 16
FF_CHUNK = 1024
PROMPT_ROWS = 512

BF16 = jnp.bfloat16
F32 = jnp.float32


def _rms(x, g):
    ms = jnp.mean(x * x, axis=-1, keepdims=True)
    return x * lax.rsqrt(ms + EPS) * g


def _const_spec(shape):
    nd = len(shape)
    return pl.BlockSpec(shape, lambda *_: (0,) * nd, pipeline_mode=pl.Buffered(1))


def _project_kernel(x_ref, g1_ref, w_ref, invf_ref, lng_ref, lnb_ref, wsp_ref, bsp_ref, *rest,
                    rows, tail_from, tail_transposed, pos_base, pos_period, class_major, emit_vn, seq_rows, rider,
                    cast_weights):
    rest = list(rest)
    rider_in = [rest.pop(0) for _ in range(5)] if rider else None
    q_ref = rest.pop(0)
    k_ref, v_ref = (rest.pop(0), rest.pop(0)) if class_major else (None, None)
    kf_ref, vf_ref, gated_ref = (rest.pop(0) for _ in range(3))
    vn_ref = rest.pop(0) if emit_vn else None
    rider_out = rest.pop(0) if rider else None
    if cast_weights:
        w_bf16 = w_ref[...].astype(BF16)
        rest.pop(0)[...] = w_bf16
        wcols = lambda a, b: w_bf16[:, a:b]
    else:
        wcols = lambda a, b: w_ref[:, a:b]
    cos_off_ref, sin_off_ref = rest.pop(0), rest.pop(0)
    zs_ref, zq_ref = (rest.pop(0), rest.pop(0)) if class_major else (None, None)
    mix_ref = rest.pop(0) if seq_rows else None
    i = pl.program_id(0)

    @pl.when(i == 0)
    def _():
        off = (lax.broadcasted_iota(jnp.int32, (rows, LANES), 0) % pos_period).astype(F32)
        ang = off * invf_ref[...]
        cos_off_ref[...] = jnp.cos(ang)
        sin_off_ref[...] = jnp.sin(ang)

    tile_stride = rows if pos_period == rows else 0
    base = (i * tile_stride).astype(F32) + pos_base
    base_ang = base * invf_ref[...]
    cb, sb = jnp.cos(base_ang), jnp.sin(base_ang)
    co, so = cos_off_ref[...], sin_off_ref[...]
    cosf = cb * co - sb * so
    sint = sb * co + cb * so
    head_lane = lax.broadcasted_iota(jnp.int32, (1, LANES), 1) % HEAD_DIM
    sina = jnp.where(head_lane < ROT_DIM // 2, -sint, 0.0)
    sinb = jnp.where((head_lane >= ROT_DIM // 2) & (head_lane < ROT_DIM), sint, 0.0)

    def rope(z):
        return z * cosf + pltpu.roll(z, LANES - ROT_DIM // 2, 1) * sina + pltpu.roll(z, ROT_DIM // 2, 1) * sinb

    xn = _rms(x_ref[...], g1_ref[...]).astype(BF16)
    zq = jnp.dot(xn, wcols(0, ATTN_WIDTH), preferred_element_type=F32)
    zk = jnp.dot(xn, wcols(ATTN_WIDTH, 2 * ATTN_WIDTH), preferred_element_type=F32)
    zv = jnp.dot(xn, wcols(2 * ATTN_WIDTH, 3 * ATTN_WIDTH), preferred_element_type=F32)
    slabs = [slice(s * LANES, (s + 1) * LANES) for s in range(PAIRS)]
    qr = [rope(zq[:, sl]) * Q_SCALE for sl in slabs]
    kr = [rope(zk[:, sl]) for sl in slabs]

    @pl.when(i >= tail_from)
    def _():
        if tail_transposed:
            kf_ref[...] = jnp.concatenate(kr, axis=-1).T
            vf_ref[...] = zv.T
        else:
            for s, sl in enumerate(slabs):
                kf_ref[:, sl] = kr[s]
            vf_ref[...] = zv

    if class_major:
        per_class = rows // N_CLASS
        for s, sl in enumerate(slabs):
            zs_ref[s] = qr[s]
            zs_ref[PAIRS + s] = kr[s]
            zs_ref[2 * PAIRS + s] = zv[:, sl]
        for n in range(3 * PAIRS):
            for c in range(DEINT):
                zq_ref[n, c] = zs_ref[n, pl.ds(c, rows // DEINT, stride=DEINT), :]
        for r in range(N_CLASS):
            for s, sl in enumerate(slabs):
                pick = lambda n: zq_ref[n, r % DEINT, pl.ds(r // DEINT, per_class, stride=DEINT), :].astype(BF16)
                q_ref[r, :, sl] = pick(s)
                k_ref[r, :, sl] = pick(PAIRS + s)
                v_ref[r, :, sl] = pick(2 * PAIRS + s)
    else:
        for s, sl in enumerate(slabs):
            q_ref[:, sl] = qr[s]

    o_u = 3 * ATTN_WIDTH
    u = jax.nn.gelu(jnp.dot(xn, wcols(o_u, o_u + GMLP_WIDTH), preferred_element_type=F32))
    vc = jax.nn.gelu(jnp.dot(xn, wcols(o_u + GMLP_WIDTH, o_u + 2 * GMLP_WIDTH), preferred_element_type=F32))
    mu = jnp.mean(vc, axis=-1, keepdims=True)
    cen = vc - mu
    var = jnp.mean(cen * cen, axis=-1, keepdims=True)
    vn = cen * lax.rsqrt(var + EPS) * lng_ref[...] + lnb_ref[...]
    if emit_vn:
        vn_ref[...] = vn
    if seq_rows:
        nseq = rows // seq_rows
        in_first = lax.broadcasted_iota(jnp.int32, (1, LANES), 1) < HEAD_DIM
        for s in range(GMLP_WIDTH // LANES):
            ga, gb = 2 * s, 2 * s + 1
            mix_ref[0, s] = vn[:, s * LANES:(s + 1) * LANES]
            xs = [mix_ref[0, s, pl.ds(j, nseq, stride=seq_rows), :] for j in range(seq_rows)]
            for r in range(seq_rows):
                acc = jnp.broadcast_to(
                    jnp.where(in_first, bsp_ref[ga * seq_rows + r], bsp_ref[gb * seq_rows + r]), (nseq, LANES))
                for j in range(r + 1):
                    w = jnp.where(in_first, wsp_ref[(ga * seq_rows + r) * seq_rows + j],
                                  wsp_ref[(gb * seq_rows + r) * seq_rows + j])
                    acc = acc + w * xs[j]
                mix_ref[1, s, pl.ds(r, nseq, stride=seq_rows), :] = acc
        mixed = jnp.concatenate([mix_ref[1, s] for s in range(GMLP_WIDTH // LANES)], axis=-1)
        gated_ref[...] = (u * mixed).astype(BF16)
        return
    vnb = vn.astype(BF16)

    row = lax.broadcasted_iota(jnp.int32, (CHUNK, 2 * CHUNK), 0)
    col = lax.broadcasted_iota(jnp.int32, (CHUNK, 2 * CHUNK), 1)
    tril = (col % CHUNK) <= row
    lane = lax.broadcasted_iota(jnp.int32, (CHUNK, LANES), 1)
    first = lane < HEAD_DIM
    zero = jnp.zeros((CHUNK, LANES), BF16)
    for s in range(GMLP_WIDTH // LANES):
        sl = slice(s * LANES, (s + 1) * LANES)
        wp = jnp.where(tril, wsp_ref[s], jnp.zeros((), BF16))
        bias = bsp_ref[:, sl]
        for c in range(rows // CHUNK):
            rs = slice(c * CHUNK, (c + 1) * CHUNK)
            vblk = vnb[rs, sl]
            rhs = jnp.concatenate([jnp.where(first, vblk, zero), jnp.where(first, zero, vblk)], axis=0)
            mixed = jnp.dot(wp, rhs, preferred_element_type=F32) + bias
            gated_ref[rs, sl] = (u[rs, sl] * mixed).astype(BF16)

    if rider:
        _sample_attn_kernel(*rider_in, rider_out, t_new=rider[0], lbuf=rider[1])


def _project(x, g1, w_in_b, ln_g, ln_b, wsp, bsp, *, rows, tail_rows, pos_base, pos_period, class_major, emit_vn,
             seq_rows=None, tail_transposed=False, sample_attn=None, cast_weights=False):
    n_rows = x.shape[0]
    grid = n_rows // rows
    tail_from = (n_rows - tail_rows) // rows
    row_spec = lambda width: pl.BlockSpec((rows, width), lambda i: (i, 0))
    if tail_transposed:
        tail_shape = jax.ShapeDtypeStruct((ATTN_WIDTH, tail_rows), F32)
        tail_spec = pl.BlockSpec((ATTN_WIDTH, rows), lambda i: (0, jnp.maximum(i - tail_from, 0)))
    else:
        tail_shape = jax.ShapeDtypeStruct((tail_rows, ATTN_WIDTH), F32)
        tail_spec = pl.BlockSpec((rows, ATTN_WIDTH), lambda i: (jnp.maximum(i - tail_from, 0), 0))
    if class_major:
        qkv_shapes = [jax.ShapeDtypeStruct((N_CLASS, n_rows // N_CLASS, ATTN_WIDTH), BF16)] * 3
        qkv_specs = [pl.BlockSpec((N_CLASS, rows // N_CLASS, ATTN_WIDTH), lambda i: (0, i, 0))] * 3
    else:
        qkv_shapes = [jax.ShapeDtypeStruct((n_rows, ATTN_WIDTH), F32)]
        qkv_specs = [row_spec(ATTN_WIDTH)]
    out_shape = qkv_shapes + [
        tail_shape,
        tail_shape,
        jax.ShapeDtypeStruct((n_rows, GMLP_WIDTH), BF16),
    ]
    out_specs = qkv_specs + [tail_spec] * 2 + [row_spec(GMLP_WIDTH)]
    if emit_vn:
        out_shape.append(jax.ShapeDtypeStruct((n_rows, GMLP_WIDTH), F32))
        out_specs.append(row_spec(GMLP_WIDTH))
    scratch = [pltpu.VMEM((rows, LANES), F32), pltpu.VMEM((rows, LANES), F32)]
    if class_major:
        scratch.append(pltpu.VMEM((3 * PAIRS, rows, LANES), F32))
        scratch.append(pltpu.VMEM((3 * PAIRS, DEINT, rows // DEINT, LANES), F32))
    if seq_rows:
        scratch.append(pltpu.VMEM((2, GMLP_WIDTH // LANES, rows, LANES), F32))
        spatial_specs = [pl.BlockSpec(memory_space=pltpu.SMEM)] * 2
    else:
        spatial_specs = [_const_spec(wsp.shape), _const_spec(bsp.shape)]
    rider, rider_specs, rider_args = None, [], []
    if sample_attn is not None:
        rider_args, rider_specs, rider_out_shape, rider_out_spec, rider = _sample_attention_specs(*sample_attn)
        assert rider_args[0].shape[0] == grid
        out_shape.append(rider_out_shape)
        out_specs.append(rider_out_spec)
    if cast_weights:
        out_shape.append(jax.ShapeDtypeStruct(w_in_b.shape, BF16))
        out_specs.append(pl.BlockSpec(w_in_b.shape, lambda i: (0, 0)))
    inv_freq = ROPE_THETA ** (-jnp.arange(0, ROT_DIM, 2, dtype=F32) / ROT_DIM)
    per_head = jnp.concatenate([inv_freq, inv_freq, jnp.zeros((HEAD_DIM - ROT_DIM,), F32)])
    invf = jnp.tile(per_head, LANES // HEAD_DIM).reshape(1, LANES)
    return pl.pallas_call(
        functools.partial(_project_kernel, rows=rows, tail_from=tail_from, tail_transposed=tail_transposed,
                          pos_base=float(pos_base),
                          pos_period=pos_period, class_major=class_major, emit_vn=emit_vn, seq_rows=seq_rows,
                          rider=rider, cast_weights=cast_weights),
        grid=(grid,),
        in_specs=[
            row_spec(D_MODEL),
            _const_spec((1, D_MODEL)),
            _const_spec(w_in_b.shape),
            _const_spec((1, LANES)),
            _const_spec((1, GMLP_WIDTH)),
            _const_spec((1, GMLP_WIDTH)),
            *spatial_specs,
            *rider_specs,
        ],
        out_specs=out_specs,
        out_shape=out_shape,
        scratch_shapes=scratch,
        compiler_params=pltpu.CompilerParams(
            dimension_semantics=("arbitrary",), vmem_limit_bytes=VMEM_LIMIT_BYTES),
        name="project",
    )(x, g1, w_in_b, invf, ln_g, ln_b, wsp, bsp, *rider_args)


def _prompt_attn_kernel(q_ref, k_ref, v_ref, o_ref, kc_ref, vc_ref, q32_ref, acc_ref, m_ref, l_ref,
                        s_ref, p_ref, ms_ref):
    i = pl.program_id(1)
    rows = q_ref.shape[1]
    width = q_ref.shape[2]
    npair = width // LANES
    half = SUBLANES

    @pl.when((pl.program_id(0) == 0) & (i == 0))
    def _():
        p_ref[1] = jnp.zeros(p_ref.shape[1:], BF16)
        ms_ref[1] = jnp.zeros(ms_ref.shape[1:], F32)

    @pl.when(i == 0)
    def _():
        kc_ref[0, :, 0:rows, :] = jnp.zeros((N_CLASS, rows, width), BF16)
        vc_ref[0, :, 0:rows, :] = jnp.zeros((N_CLASS, rows, width), BF16)

    @pl.when(i > 0)
    def _():
        kc_ref[0, :, 0:rows, :] = kc_ref[0, :, rows:2 * rows, :]
        vc_ref[0, :, 0:rows, :] = vc_ref[0, :, rows:2 * rows, :]

    kc_ref[0, :, rows:2 * rows, :] = k_ref[...]
    vc_ref[0, :, rows:2 * rows, :] = v_ref[...]
    lo = rows - PACKED_ROWS
    for ref in (kc_ref, vc_ref):
        tail = ref[0, :, lo:2 * rows, :].astype(F32)
        ref[1, :, lo:2 * rows - PACKED_ROWS, :] = tail[:, half:half + rows, :].astype(BF16)
    q32_ref[...] = q_ref[...].astype(F32)
    has_prev = i > 0

    def lanes(pr):
        return slice(pr * LANES, (pr + 1) * LANES)

    mq, nk = rows, 2 * rows
    olane = lax.broadcasted_iota(jnp.int32, (mq, LANES), 1) < HEAD_DIM
    qzero = jnp.zeros((mq, LANES), BF16)
    ones_v = jnp.ones((nk, LANES), BF16)
    nt = (((1,), (1,)), ((), ()))

    def run(n_units, q_of, k_of, v_of, mask_of, state_of, put, first, last):
        units = lambda t: [t * ATTN_GROUP + j for j in range(ATTN_GROUP)]

        def scores(t, slot):
            for j, u in enumerate(units(t)):
                for pr in range(npair):
                    q = q_of(u, pr)
                    qq = jnp.concatenate([jnp.where(olane, q, qzero), jnp.where(olane, qzero, q)], axis=0)
                    s_ref[slot, j, pr] = lax.dot_general(qq, k_of(u, pr), nt, preferred_element_type=F32)

        def softmax(t, slot):
            for j, u in enumerate(units(t)):
                mask = mask_of(u)
                for pr in range(npair):
                    ms = []
                    for hh in range(2):
                        hs = slice(hh * mq, (hh + 1) * mq)
                        s = jnp.where(mask, s_ref[slot, j, pr, hs, :], NEG)
                        m = jnp.max(s, axis=-1, keepdims=True)
                        p_ref[slot, j, pr, hs, :] = jnp.exp2(s - m).astype(BF16)
                        ms.append(m)
                    ms_ref[slot, j, pr] = jnp.where(olane, ms[0], ms[1])

        def values(t, slot, valid):
            for j, u in enumerate(units(t)):
                for pr in range(npair):
                    vv = jnp.concatenate([v_of(u, pr), ones_v], axis=1)
                    r = jnp.dot(p_ref[slot, j, pr], vv, preferred_element_type=F32)
                    pv = jnp.where(olane, r[0:mq, 0:LANES], r[mq:2 * mq, 0:LANES])
                    l = jnp.where(olane, r[0:mq, LANES:2 * LANES], r[mq:2 * mq, LANES:2 * LANES])
                    m = ms_ref[slot, j, pr]
                    if not first:
                        acc0, m0, l0 = state_of(u, pr)
                        m_new = jnp.maximum(m0, m)
                        a = jnp.exp2(m0 - m_new)
                        b = jnp.exp2(m - m_new)
                        pv = acc0 * a + pv * b
                        l = l0 * a + l * b
                        m = m_new
                        if last:
                            pv = pv / l
                        if valid is not None:
                            pv, m, l = jnp.where(valid, pv, acc0), jnp.where(valid, m, m0), jnp.where(valid, l, l0)
                    put(u, pr, pv if last else (pv, m, l))

        n_trips = n_units // ATTN_GROUP
        scores(jnp.int32(0), 0)

        def trip(t, carry):
            slot = t % 2
            values(jnp.maximum(t - 1, 0), 1 - slot, t > 0)
            softmax(t, slot)
            scores(jnp.minimum(t + 1, n_trips - 1), 1 - slot)
            return carry

        lax.fori_loop(0, n_trips, trip, 0)
        values(jnp.int32(n_trips - 1), (n_trips - 1) % 2, None)

    def put_state(slabs_of, r0_of, n):
        def put(u, pr, res):
            for a, sb in enumerate(slabs_of(u)):
                rs = slice(a * n, (a + 1) * n)
                acc_ref[sb, pl.ds(r0_of(u), n), lanes(pr)] = res[0][rs]
                m_ref[sb, pl.ds(r0_of(u), n), lanes(pr)] = res[1][rs]
                l_ref[sb, pl.ds(r0_of(u), n), lanes(pr)] = res[2][rs]
        return put

    def gather(ref, slabs, r0, n, pr):
        return jnp.concatenate([ref[sb, pl.ds(r0, n), lanes(pr)] for sb in slabs], axis=0)

    state_refs = (acc_ref, m_ref, l_ref)
    kplain, vplain = kc_ref.at[0], vc_ref.at[0]

    qi = lax.broadcasted_iota(jnp.int32, (mq, nk), 0)
    kj = lax.broadcasted_iota(jnp.int32, (mq, nk), 1)
    diff = qi + rows - kj
    mask16 = (diff >= 0) & (diff <= SPAN) & ((kj >= rows) | has_prev)
    run(N_CLASS,
        q_of=lambda u, pr: q_ref[u, :, lanes(pr)],
        k_of=lambda u, pr: kplain[u, :, lanes(pr)],
        v_of=lambda u, pr: vplain[u, :, lanes(pr)],
        mask_of=lambda u: mask16, state_of=None,
        put=put_state(lambda u: [u], lambda u: 0, rows), first=True, last=False)

    n4 = N_CLASS // 4
    qb = rows // n4
    d4 = 4 * (qi % qb - kj % (2 * qb) + qb) + (qi // qb - kj // (2 * qb))
    band4 = (d4 >= 0) & (d4 <= SPAN)
    cur4 = kj % (2 * qb) >= qb
    slabs4 = lambda u: [u // n4 + 4 * a for a in range(n4)]
    q0_4 = lambda u: pl.multiple_of((u % n4) * qb, qb)
    k0_4 = lambda u: pl.multiple_of(rows - qb + (u % n4) * qb, qb)
    run(N_CLASS,
        q_of=lambda u, pr: gather(q_ref, slabs4(u), q0_4(u), qb, pr),
        k_of=lambda u, pr: gather(kplain, slabs4(u), k0_4(u), 2 * qb, pr),
        v_of=lambda u, pr: gather(vplain, slabs4(u), k0_4(u), 2 * qb, pr),
        mask_of=lambda u: band4 & (cur4 | has_prev | (u % n4 > 0)),
        state_of=lambda u, pr: tuple(gather(ref, slabs4(u), q0_4(u), qb, pr) for ref in state_refs),
        put=put_state(slabs4, q0_4, qb), first=False, last=False)

    d1 = N_CLASS * (qi % half - kj % (2 * half) + half) + (qi // half - kj // (2 * half))
    band1 = (d1 >= 0) & (d1 <= SPAN)
    cur1 = kj % (2 * half) >= half
    every = list(range(N_CLASS))
    q0_1 = lambda u: pl.multiple_of(u * half, half)
    kcopy = lambda u: 1 - u % 2
    k0_1 = lambda u: pl.multiple_of(((rows - half + u * half) // PACKED_ROWS) * PACKED_ROWS, PACKED_ROWS)

    def put_out(u, pr, res):
        for sb in every:
            acc_ref[sb, pl.ds(q0_1(u), half), lanes(pr)] = res[sb * half:(sb + 1) * half]

    run(rows // half,
        q_of=lambda u, pr: gather(q32_ref, every, q0_1(u), half, pr).astype(BF16),
        k_of=lambda u, pr: gather(kc_ref.at[kcopy(u)], every, k0_1(u), 2 * half, pr),
        v_of=lambda u, pr: gather(vc_ref.at[kcopy(u)], every, k0_1(u), 2 * half, pr),
        mask_of=lambda u: band1 & (cur1 | has_prev | (u > 0)),
        state_of=lambda u, pr: tuple(gather(ref, every, q0_1(u), half, pr) for ref in state_refs),
        put=put_out, first=False, last=True)
    o_ref[...] = acc_ref[...].astype(BF16)


def _prompt_attention(q_cm, k_cm, v_cm):
    n_slab_rows = q_cm.shape[1]
    npair = ATTN_STEP_LANES // LANES
    spec = pl.BlockSpec((N_CLASS, SPAN, ATTN_STEP_LANES), lambda h, i: (0, i, h))
    return pl.pallas_call(
        _prompt_attn_kernel,
        grid=(ATTN_WIDTH // ATTN_STEP_LANES, n_slab_rows // SPAN),
        in_specs=[spec, spec, spec],
        out_specs=spec,
        out_shape=jax.ShapeDtypeStruct(q_cm.shape, BF16),
        scratch_shapes=[
            pltpu.VMEM((2, N_CLASS, 2 * SPAN, ATTN_STEP_LANES), BF16),
            pltpu.VMEM((2, N_CLASS, 2 * SPAN, ATTN_STEP_LANES), BF16),
            pltpu.VMEM((N_CLASS, SPAN, ATTN_STEP_LANES), F32),
            pltpu.VMEM((N_CLASS, SPAN, ATTN_STEP_LANES), F32),
            pltpu.VMEM((N_CLASS, SPAN, ATTN_STEP_LANES), F32),
            pltpu.VMEM((N_CLASS, SPAN, ATTN_STEP_LANES), F32),
            pltpu.VMEM((2, ATTN_GROUP, npair, 2 * SPAN, 2 * SPAN), F32),
            pltpu.VMEM((2, ATTN_GROUP, npair, 2 * SPAN, 2 * SPAN), BF16),
            pltpu.VMEM((2, ATTN_GROUP, npair, SPAN, LANES), F32),
        ],
        compiler_params=pltpu.CompilerParams(
            dimension_semantics=("arbitrary", "arbitrary"), vmem_limit_bytes=VMEM_LIMIT_BYTES),
        name="prompt_attention",
    )(q_cm, k_cm, v_cm)


def _sample_attn_kernel(q_ref, kn_ref, vn_ref, kt_ref, vt_ref, o_ref, *, t_new, lbuf):
    width = ATTN_WIDTH
    nrow = t_new * N_HEADS
    q = q_ref[...]
    kn = kn_ref[...]
    vnew = vn_ref[...]

    row = lax.broadcasted_iota(jnp.int32, (nrow, width), 0)
    lane = lax.broadcasted_iota(jnp.int32, (nrow, width), 1)
    hmask = (lane // HEAD_DIM) == (row % N_HEADS)
    qtok = jnp.concatenate([jnp.broadcast_to(q[j:j + 1, :], (N_HEADS, width)) for j in range(t_new)], axis=0)
    qrows = jnp.where(hmask, qtok, 0.0)

    kt = kt_ref[...].reshape(width, lbuf).astype(BF16)
    vt = vt_ref[...].reshape(width, lbuf).astype(BF16)
    s = jnp.dot(qrows.astype(BF16), kt, preferred_element_type=F32)

    def reach_count(delta):
        cnt = jnp.zeros(delta.shape, F32)
        for window, dil in DILATIONS:
            hit = (delta >= 0) & (delta % dil == 0) & (delta <= window)
            cnt = cnt + jnp.where(hit, 1.0, 0.0)
        return cnt

    pos = lax.broadcasted_iota(jnp.int32, (nrow, lbuf), 1)
    tok = lax.broadcasted_iota(jnp.int32, (nrow, lbuf), 0) // N_HEADS
    cnt = reach_count(lbuf + tok - pos)
    tok1 = tok[:, 0:1]
    cnew = [reach_count(tok1 - j) for j in range(t_new)]
    snew = [jnp.sum(qrows * kn[j:j + 1, :], axis=-1, keepdims=True) for j in range(t_new)]

    s = jnp.where(cnt > 0.0, s, NEG)
    m = jnp.max(s, axis=-1, keepdims=True)
    for j in range(t_new):
        m = jnp.maximum(m, jnp.where(cnew[j] > 0.0, snew[j], NEG))
    e = jnp.exp2(s - m) * cnt
    l = jnp.sum(e, axis=-1, keepdims=True)
    o = lax.dot_general(e.astype(BF16), vt, (((1,), (1,)), ((), ())), preferred_element_type=F32)
    for j in range(t_new):
        ej = jnp.where(cnew[j] > 0.0, jnp.exp2(snew[j] - m), 0.0) * cnew[j]
        l = l + ej
        o = o + ej * vnew[j:j + 1, :]
    o = jnp.where(hmask, o / l, 0.0)
    for j in range(t_new):
        o_ref[j:j + 1, :] = jnp.sum(o[j * N_HEADS:(j + 1) * N_HEADS, :], axis=0, keepdims=True)


def _sample_attention_specs(q, k_new, v_new, cache_kt, cache_vt):
    nb, t_new, width = q.shape
    lbuf = cache_kt.shape[-1]
    new_spec = pl.BlockSpec((None, t_new, width), lambda b: (b, 0, 0))
    cache_spec = pl.BlockSpec((None, N_HEADS, HEAD_DIM, lbuf), lambda b: (b, 0, 0, 0))
    return ([q, k_new, v_new, cache_kt, cache_vt],
            [new_spec, new_spec, new_spec, cache_spec, cache_spec],
            jax.ShapeDtypeStruct((nb, t_new, width), F32),
            pl.BlockSpec((None, t_new, width), lambda b: (b, 0, 0)),
            (t_new, lbuf))


def _sample_attention(q, k_new, v_new, cache_kt, cache_vt):
    args, in_specs, out_shape, out_spec, (t_new, lbuf) = _sample_attention_specs(q, k_new, v_new, cache_kt, cache_vt)
    return pl.pallas_call(
        functools.partial(_sample_attn_kernel, t_new=t_new, lbuf=lbuf),
        grid=(q.shape[0],),
        in_specs=in_specs,
        out_specs=out_spec,
        out_shape=out_shape,
        compiler_params=pltpu.CompilerParams(
            dimension_semantics=("arbitrary",), vmem_limit_bytes=VMEM_LIMIT_BYTES),
        name="sample_attention",
    )(*args)


def _finish_kernel(x_ref, attn_ref, gated_ref, p_ref, wo_ref, g2_ref, wup_ref, wdn_ref,
                   gg_ref, wg_ref, wple_ref, gf_ref, y_ref, nat_ref, *, apply_final):
    per_class = attn_ref.shape[1]
    for r in range(N_CLASS):
        for s in range(PAIRS):
            nat_ref[s, pl.ds(r, per_class, stride=N_CLASS), :] = attn_ref[r, :, s * LANES:(s + 1) * LANES].astype(F32)
    attn = jnp.concatenate([nat_ref[s] for s in range(PAIRS)], axis=-1).astype(BF16)
    mix = jnp.concatenate([attn, gated_ref[...]], axis=-1)
    h = x_ref[...] + jnp.dot(mix, wo_ref[...], preferred_element_type=F32)
    n2 = _rms(h, g2_ref[...]).astype(BF16)
    f = jnp.zeros_like(h)
    for c in range(D_FF // FF_CHUNK):
        cs = slice(c * FF_CHUNK, (c + 1) * FF_CHUNK)
        up = jnp.dot(n2, wup_ref[:, cs], preferred_element_type=F32)
        act = jnp.square(jnp.maximum(up, 0.0)).astype(BF16)
        f = f + jnp.dot(act, wdn_ref[cs, :], preferred_element_type=F32)
    h = h + f
    gate = jax.nn.sigmoid(jnp.dot(_rms(h, gg_ref[...]).astype(BF16), wg_ref[...], preferred_element_type=F32))
    ple = jnp.dot(p_ref[...].astype(BF16), wple_ref[...], preferred_element_type=F32)
    h = h + gate * ple
    y_ref[...] = _rms(h, gf_ref[...]) if apply_final else h


def _finish(x, attn, gated, p, wo, g2, wup, wdn, gg, wg, wple, gf, *, rows, apply_final):
    n_rows = x.shape[0]
    row_spec = lambda width: pl.BlockSpec((rows, width), lambda i: (i, 0))
    attn_spec = pl.BlockSpec((N_CLASS, rows // N_CLASS, ATTN_WIDTH), lambda i: (0, i, 0))
    scratch = [pltpu.VMEM((PAIRS, rows, LANES), F32)]
    return pl.pallas_call(
        functools.partial(_finish_kernel, apply_final=apply_final),
        grid=(n_rows // rows,),
        in_specs=[
            row_spec(D_MODEL), attn_spec, row_spec(GMLP_WIDTH), row_spec(PLE_DIM),
            _const_spec(wo.shape), _const_spec((1, D_MODEL)), _const_spec(wup.shape), _const_spec(wdn.shape),
            _const_spec((1, D_MODEL)), _const_spec(wg.shape), _const_spec(wple.shape), _const_spec((1, D_MODEL)),
        ],
        out_specs=row_spec(D_MODEL),
        out_shape=jax.ShapeDtypeStruct((n_rows, D_MODEL), F32),
        scratch_shapes=scratch,
        compiler_params=pltpu.CompilerParams(
            dimension_semantics=("arbitrary",), vmem_limit_bytes=VMEM_LIMIT_BYTES),
        name="finish",
    )(x, attn, gated, p, wo, g2, wup, wdn, gg, wg, wple, gf)


def _finish_casting_kernel(x_ref, attn_ref, gated_ref, p_ref, wo_ref, g2_ref, wup_ref, wdn_ref,
                           gg_ref, wg_ref, wple_ref, gf_ref,
                           y_ref, wo_b_ref, wup_b_ref, wdn_b_ref, wg_b_ref, wple_b_ref,
                           h_ref, n2_ref, f_ref, *, apply_final):
    c = pl.program_id(0)

    @pl.when(c == 0)
    def _():
        wo_b = wo_ref[...].astype(BF16)
        wo_b_ref[...] = wo_b
        mix = jnp.concatenate([attn_ref[...], gated_ref[...]], axis=-1)
        h = x_ref[...] + jnp.dot(mix, wo_b, preferred_element_type=F32)
        h_ref[...] = h
        n2_ref[...] = _rms(h, g2_ref[...]).astype(BF16)
        f_ref[...] = jnp.zeros(f_ref.shape, F32)

    wup_b = wup_ref[...].astype(BF16)
    wdn_b = wdn_ref[...].astype(BF16)
    wup_b_ref[...] = wup_b
    wdn_b_ref[...] = wdn_b
    up = jnp.dot(n2_ref[...], wup_b, preferred_element_type=F32)
    act = jnp.square(jnp.maximum(up, 0.0)).astype(BF16)
    f_ref[...] += jnp.dot(act, wdn_b, preferred_element_type=F32)

    @pl.when(c == pl.num_programs(0) - 1)
    def _():
        wg_b = wg_ref[...].astype(BF16)
        wple_b = wple_ref[...].astype(BF16)
        wg_b_ref[...] = wg_b
        wple_b_ref[...] = wple_b
        h = h_ref[...] + f_ref[...]
        gate = jax.nn.sigmoid(jnp.dot(_rms(h, gg_ref[...]).astype(BF16), wg_b, preferred_element_type=F32))
        ple = jnp.dot(p_ref[...].astype(BF16), wple_b, preferred_element_type=F32)
        h = h + gate * ple
        y_ref[...] = _rms(h, gf_ref[...]) if apply_final else h


def _finish_casting(x, attn, gated, p, wo, g2, wup, wdn, gg, wg, wple, gf, *, apply_final):
    rows = x.shape[0]
    bf = lambda w: jax.ShapeDtypeStruct(w.shape, BF16)
    up_spec = pl.BlockSpec((D_MODEL, FF_CHUNK), lambda c: (0, c))
    dn_spec = pl.BlockSpec((FF_CHUNK, D_MODEL), lambda c: (c, 0))
    whole = lambda a: _const_spec(a.shape)
    held = lambda a: pl.BlockSpec(a.shape, lambda c: (0,) * a.ndim)
    return pl.pallas_call(
        functools.partial(_finish_casting_kernel, apply_final=apply_final),
        grid=(D_FF // FF_CHUNK,),
        in_specs=[whole(x), whole(attn), whole(gated), whole(p), whole(wo), whole(g2), up_spec, dn_spec,
                  whole(gg), whole(wg), whole(wple), whole(gf)],
        out_specs=[held(x), held(wo), up_spec, dn_spec, held(wg), held(wple)],
        out_shape=[jax.ShapeDtypeStruct((rows, D_MODEL), F32), bf(wo), bf(wup), bf(wdn), bf(wg), bf(wple)],
        scratch_shapes=[pltpu.VMEM((rows, D_MODEL), F32), pltpu.VMEM((rows, D_MODEL), BF16),
                        pltpu.VMEM((rows, D_MODEL), F32)],
        compiler_params=pltpu.CompilerParams(
            dimension_semantics=("arbitrary",), vmem_limit_bytes=VMEM_LIMIT_BYTES),
        name="finish_casting",
    )(x, attn, gated, p, wo, g2, wup, wdn, gg, wg, wple, gf)


def _pair_spatial(w):
    g, l, _ = w.shape
    return w.reshape(g // 2, 2, l, l).transpose(0, 2, 1, 3).reshape(g // 2, l, 2 * l).astype(BF16)


def _bias_lanes(b):
    return jnp.repeat(b.T, GMLP_WIDTH // GMLP_GROUPS, axis=1)


def kernel(x_prompt, x_sample, cache_k, cache_v, p_prompt, p_sample, norm1_g, w_in, ln_v_g, ln_v_b,
           w_spatial, b_spatial, w_out, norm2_g, w_up, w_down, gate_norm_g, w_gate, w_ple, final_g):
    depth = w_in.shape[0]
    nbp, seq, _ = x_prompt.shape
    nbs, t_new, _ = x_sample.shape
    keep = min(WINDOW_MAX, seq)
    assert nbp == 1 and seq % (N_CLASS * SPAN) == 0 and (nbs * t_new) % CHUNK == 0 and t_new <= CHUNK
    n_s = nbs * t_new

    row2 = lambda a: a.reshape(1, -1)

    hp = x_prompt.reshape(seq, D_MODEL)
    hs = x_sample.reshape(n_s, D_MODEL)
    nk_p, nv_p, nk_s, nv_s, nvc_s = [], [], [], [], []
    for i in range(depth):
        last = i == depth - 1
        g2, gg, gf = row2(norm2_g[i]), row2(gate_norm_g[i]), row2(final_g)
        ln_g, ln_b = row2(ln_v_g[i]), row2(ln_v_b[i])

        wsp_s = w_spatial[i][:, :t_new, :t_new].reshape(-1)
        bsp_s = b_spatial[i][:, :t_new].reshape(-1)
        q_s, kf_s, vf_s, gated_s, vn, w_in_b = _project(hs, row2(norm1_g[i]), w_in[i], ln_g, ln_b, wsp_s, bsp_s,
                                                        rows=n_s, tail_rows=n_s, pos_base=PAST_LEN, pos_period=t_new,
                                                        class_major=False, emit_vn=True, seq_rows=t_new,
                                                        cast_weights=True)
        r3 = lambda a: a.reshape(nbs, t_new, ATTN_WIDTH)
        sample_ops = (r3(q_s), r3(kf_s), r3(vf_s),
                      cache_k[i].transpose(0, 2, 3, 1), cache_v[i].transpose(0, 2, 3, 1))

        ride = seq // PROMPT_ROWS == nbs
        wsp_p = _pair_spatial(w_spatial[i])
        bsp_p = _bias_lanes(b_spatial[i])
        q, k, v, kf, vf, gated, *rode = _project(hp, row2(norm1_g[i]), w_in_b, ln_g, ln_b, wsp_p, bsp_p,
                                                 rows=PROMPT_ROWS, tail_rows=keep, pos_base=0, pos_period=PROMPT_ROWS,
                                                 class_major=True, emit_vn=False, tail_transposed=True,
                                                 sample_attn=sample_ops if ride else None)

        attn_s = rode[0] if ride else _sample_attention(*sample_ops)
        hs, wo_b, wup_b, wdn_b, wg_b, wple_b = _finish_casting(
            hs, attn_s.reshape(n_s, ATTN_WIDTH).astype(BF16), gated_s, p_sample[i].reshape(n_s, PLE_DIM),
            w_out[i], g2, w_up[i], w_down[i], gg, w_gate[i], w_ple[i], gf, apply_final=last)
        nk_s.append(kf_s.reshape(nbs, t_new, N_HEADS, HEAD_DIM))
        nv_s.append(vf_s.reshape(nbs, t_new, N_HEADS, HEAD_DIM))

        attn_cm = _prompt_attention(q, k, v)
        hp = _finish(hp, attn_cm, gated, p_prompt[i].reshape(seq, PLE_DIM),
                     wo_b, g2, wup_b, wdn_b, gg, wg_b, wple_b, gf, rows=PROMPT_ROWS, apply_final=last)
        from_cm = lambda a: a.reshape(N_HEADS, HEAD_DIM, keep).transpose(2, 0, 1).reshape(nbp, keep, N_HEADS, HEAD_DIM)
        nk_p.append(from_cm(kf))
        nv_p.append(from_cm(vf))
        nvc_s.append(vn.reshape(nbs, t_new, GMLP_WIDTH))

    return (hp.reshape(nbp, seq, D_MODEL), hs.reshape(nbs, t_new, D_MODEL),
            jnp.stack(nk_p), jnp.stack(nv_p), jnp.stack(nk_s), jnp.stack(nv_s), jnp.stack(nvc_s))
```

```python
import functools

import jax
import jax.numpy as jnp
from jax import lax
from jax.experimental import pallas as pl
from jax.experimental.pallas import tpu as pltpu

D_MODEL = 1024
N_HEADS = 8
HEAD_DIM = 64
ATTN_WIDTH = N_HEADS * HEAD_DIM
GMLP_GROUPS = 8
GMLP_WIDTH = 512
CHUNK = 128
DILATIONS = ((128, 1), (512, 4), (2048, 16))
WINDOW_MAX = 2048
PAST_LEN = 16384
ROT_DIM = HEAD_DIM // 4
ROPE_THETA = 500000.0
D_FF = 4 * D_MODEL
PLE_DIM = 256
EPS = 1e-6
NEG = -1e30
Q_SCALE = HEAD_DIM ** -0.5 * 1.4426950408889634

LANES = 128
VMEM_LIMIT_BYTES = 56 * 1024 * 1024

N_CLASS = 16
DEINT = 4
SPAN = 128
PAIRS = ATTN_WIDTH // LANES
ATTN_STEP_LANES = 2 * LANES
ATTN_GROUP = 2
SUBLANES = 8
PACKED_ROWS = 16
FF_CHUNK = 1024
PROMPT_ROWS = 512

BF16 = jnp.bfloat16
F32 = jnp.float32


def _rms(x, g):
    ms = jnp.mean(x * x, axis=-1, keepdims=True)
    return x * lax.rsqrt(ms + EPS) * g


def _const_spec(shape):
    nd = len(shape)
    return pl.BlockSpec(shape, lambda *_: (0,) * nd, pipeline_mode=pl.Buffered(1))


def _project_kernel(x_ref, g1_ref, w_ref, invf_ref, lng_ref, lnb_ref, wsp_ref, bsp_ref, *rest,
                    rows, tail_from, tail_transposed, pos_base, pos_period, class_major, emit_vn, seq_rows, rider,
                    cast_weights):
    rest = list(rest)
    rider_in = [rest.pop(0) for _ in range(5)] if rider else None
    q_ref = rest.pop(0)
    k_ref, v_ref = (rest.pop(0), rest.pop(0)) if class_major else (None, None)
    kf_ref, vf_ref, gated_ref = (rest.pop(0) for _ in range(3))
    vn_ref = rest.pop(0) if emit_vn else None
    rider_out = rest.pop(0) if rider else None
    if cast_weights:
        w_bf16 = w_ref[...].astype(BF16)
        rest.pop(0)[...] = w_bf16
        wcols = lambda a, b: w_bf16[:, a:b]
    else:
        wcols = lambda a, b: w_ref[:, a:b]
    cos_off_ref, sin_off_ref = rest.pop(0), rest.pop(0)
    zs_ref, zq_ref = (rest.pop(0), rest.pop(0)) if class_major else (None, None)
    mix_ref = rest.pop(0) if seq_rows else None
    i = pl.program_id(0)

    @pl.when(i == 0)
    def _():
        off = (lax.broadcasted_iota(jnp.int32, (rows, LANES), 0) % pos_period).astype(F32)
        ang = off * invf_ref[...]
        cos_off_ref[...] = jnp.cos(ang)
        sin_off_ref[...] = jnp.sin(ang)

    tile_stride = rows if pos_period == rows else 0
    base = (i * tile_stride).astype(F32) + pos_base
    base_ang = base * invf_ref[...]
    cb, sb = jnp.cos(base_ang), jnp.sin(base_ang)
    co, so = cos_off_ref[...], sin_off_ref[...]
    cosf = cb * co - sb * so
    sint = sb * co + cb * so
    head_lane = lax.broadcasted_iota(jnp.int32, (1, LANES), 1) % HEAD_DIM
    sina = jnp.where(head_lane < ROT_DIM // 2, -sint, 0.0)
    sinb = jnp.where((head_lane >= ROT_DIM // 2) & (head_lane < ROT_DIM), sint, 0.0)

    def rope(z):
        return z * cosf + pltpu.roll(z, LANES - ROT_DIM // 2, 1) * sina + pltpu.roll(z, ROT_DIM // 2, 1) * sinb

    xn = _rms(x_ref[...], g1_ref[...]).astype(BF16)
    zq = jnp.dot(xn, wcols(0, ATTN_WIDTH), preferred_element_type=F32)
    zk = jnp.dot(xn, wcols(ATTN_WIDTH, 2 * ATTN_WIDTH), preferred_element_type=F32)
    zv = jnp.dot(xn, wcols(2 * ATTN_WIDTH, 3 * ATTN_WIDTH), preferred_element_type=F32)
    slabs = [slice(s * LANES, (s + 1) * LANES) for s in range(PAIRS)]
    qr = [rope(zq[:, sl]) * Q_SCALE for sl in slabs]
    kr = [rope(zk[:, sl]) for sl in slabs]

    @pl.when(i >= tail_from)
    def _():
        if tail_transposed:
            kf_ref[...] = jnp.concatenate(kr, axis=-1).T
            vf_ref[...] = zv.T
        else:
            for s, sl in enumerate(slabs):
                kf_ref[:, sl] = kr[s]
            vf_ref[...] = zv

    if class_major:
        per_class = rows // N_CLASS
        for s, sl in enumerate(slabs):
            zs_ref[s] = qr[s]
            zs_ref[PAIRS + s] = kr[s]
            zs_ref[2 * PAIRS + s] = zv[:, sl]
        for n in range(3 * PAIRS):
            for c in range(DEINT):
                zq_ref[n, c] = zs_ref[n, pl.ds(c, rows // DEINT, stride=DEINT), :]
        for r in range(N_CLASS):
            for s, sl in enumerate(slabs):
                pick = lambda n: zq_ref[n, r % DEINT, pl.ds(r // DEINT, per_class, stride=DEINT), :].astype(BF16)
                q_ref[r, :, sl] = pick(s)
                k_ref[r, :, sl] = pick(PAIRS + s)
                v_ref[r, :, sl] = pick(2 * PAIRS + s)
    else:
        for s, sl in enumerate(slabs):
            q_ref[:, sl] = qr[s]

    o_u = 3 * ATTN_WIDTH
    u = jax.nn.gelu(jnp.dot(xn, wcols(o_u, o_u + GMLP_WIDTH), preferred_element_type=F32))
    vc = jax.nn.gelu(jnp.dot(xn, wcols(o_u + GMLP_WIDTH, o_u + 2 * GMLP_WIDTH), preferred_element_type=F32))
    mu = jnp.mean(vc, axis=-1, keepdims=True)
    cen = vc - mu
    var = jnp.mean(cen * cen, axis=-1, keepdims=True)
    vn = cen * lax.rsqrt(var + EPS) * lng_ref[...] + lnb_ref[...]
    if emit_vn:
        vn_ref[...] = vn
    if seq_rows:
        nseq = rows // seq_rows
        in_first = lax.broadcasted_iota(jnp.int32, (1, LANES), 1) < HEAD_DIM
        for s in range(GMLP_WIDTH // LANES):
            ga, gb = 2 * s, 2 * s + 1
            mix_ref[0, s] = vn[:, s * LANES:(s + 1) * LANES]
            xs = [mix_ref[0, s, pl.ds(j, nseq, stride=seq_rows), :] for j in range(seq_rows)]
            for r in range(seq_rows):
                acc = jnp.broadcast_to(
                    jnp.where(in_first, bsp_ref[ga * seq_rows + r], bsp_ref[gb * seq_rows + r]), (nseq, LANES))
                for j in range(r + 1):
                    w = jnp.where(in_first, wsp_ref[(ga * seq_rows + r) * seq_rows + j],
                                  wsp_ref[(gb * seq_rows + r) * seq_rows + j])
                    acc = acc + w * xs[j]
                mix_ref[1, s, pl.ds(r, nseq, stride=seq_rows), :] = acc
        mixed = jnp.concatenate([mix_ref[1, s] for s in range(GMLP_WIDTH // LANES)], axis=-1)
        gated_ref[...] = (u * mixed).astype(BF16)
        return
    vnb = vn.astype(BF16)

    row = lax.broadcasted_iota(jnp.int32, (CHUNK, 2 * CHUNK), 0)
    col = lax.broadcasted_iota(jnp.int32, (CHUNK, 2 * CHUNK), 1)
    tril = (col % CHUNK) <= row
    lane = lax.broadcasted_iota(jnp.int32, (CHUNK, LANES), 1)
    first = lane < HEAD_DIM
    zero = jnp.zeros((CHUNK, LANES), BF16)
    for s in range(GMLP_WIDTH // LANES):
        sl = slice(s * LANES, (s + 1) * LANES)
        wp = jnp.where(tril, wsp_ref[s], jnp.zeros((), BF16))
        bias = bsp_ref[:, sl]
        for c in range(rows // CHUNK):
            rs = slice(c * CHUNK, (c + 1) * CHUNK)
            vblk = vnb[rs, sl]
            rhs = jnp.concatenate([jnp.where(first, vblk, zero), jnp.where(first, zero, vblk)], axis=0)
            mixed = jnp.dot(wp, rhs, preferred_element_type=F32) + bias
            gated_ref[rs, sl] = (u[rs, sl] * mixed).astype(BF16)

    if rider:
        _sample_attn_kernel(*rider_in, rider_out, t_new=rider[0], lbuf=rider[1])


def _project(x, g1, w_in_b, ln_g, ln_b, wsp, bsp, *, rows, tail_rows, pos_base, pos_period, class_major, emit_vn,
             seq_rows=None, tail_transposed=False, sample_attn=None, cast_weights=False):
    n_rows = x.shape[0]
    grid = n_rows // rows
    tail_from = (n_rows - tail_rows) // rows
    row_spec = lambda width: pl.BlockSpec((rows, width), lambda i: (i, 0))
    if tail_transposed:
        tail_shape = jax.ShapeDtypeStruct((ATTN_WIDTH, tail_rows), F32)
        tail_spec = pl.BlockSpec((ATTN_WIDTH, rows), lambda i: (0, jnp.maximum(i - tail_from, 0)))
    else:
        tail_shape = jax.ShapeDtypeStruct((tail_rows, ATTN_WIDTH), F32)
        tail_spec = pl.BlockSpec((rows, ATTN_WIDTH), lambda i: (jnp.maximum(i - tail_from, 0), 0))
    if class_major:
        qkv_shapes = [jax.ShapeDtypeStruct((N_CLASS, n_rows // N_CLASS, ATTN_WIDTH), BF16)] * 3
        qkv_specs = [pl.BlockSpec((N_CLASS, rows // N_CLASS, ATTN_WIDTH), lambda i: (0, i, 0))] * 3
    else:
        qkv_shapes = [jax.ShapeDtypeStruct((n_rows, ATTN_WIDTH), F32)]
        qkv_specs = [row_spec(ATTN_WIDTH)]
    out_shape = qkv_shapes + [
        tail_shape,
        tail_shape,
        jax.ShapeDtypeStruct((n_rows, GMLP_WIDTH), BF16),
    ]
    out_specs = qkv_specs + [tail_spec] * 2 + [row_spec(GMLP_WIDTH)]
    if emit_vn:
        out_shape.append(jax.ShapeDtypeStruct((n_rows, GMLP_WIDTH), F32))
        out_specs.append(row_spec(GMLP_WIDTH))
    scratch = [pltpu.VMEM((rows, LANES), F32), pltpu.VMEM((rows, LANES), F32)]
    if class_major:
        scratch.append(pltpu.VMEM((3 * PAIRS, rows, LANES), F32))
        scratch.append(pltpu.VMEM((3 * PAIRS, DEINT, rows // DEINT, LANES), F32))
    if seq_rows:
        scratch.append(pltpu.VMEM((2, GMLP_WIDTH // LANES, rows, LANES), F32))
        spatial_specs = [pl.BlockSpec(memory_space=pltpu.SMEM)] * 2
    else:
        spatial_specs = [_const_spec(wsp.shape), _const_spec(bsp.shape)]
    rider, rider_specs, rider_args = None, [], []
    if sample_attn is not None:
        rider_args, rider_specs, rider_out_shape, rider_out_spec, rider = _sample_attention_specs(*sample_attn)
        assert rider_args[0].shape[0] == grid
        out_shape.append(rider_out_shape)
        out_specs.append(rider_out_spec)
    if cast_weights:
        out_shape.append(jax.ShapeDtypeStruct(w_in_b.shape, BF16))
        out_specs.append(pl.BlockSpec(w_in_b.shape, lambda i: (0, 0)))
    inv_freq = ROPE_THETA ** (-jnp.arange(0, ROT_DIM, 2, dtype=F32) / ROT_DIM)
    per_head = jnp.concatenate([inv_freq, inv_freq, jnp.zeros((HEAD_DIM - ROT_DIM,), F32)])
    invf = jnp.tile(per_head, LANES // HEAD_DIM).reshape(1, LANES)
    return pl.pallas_call(
        functools.partial(_project_kernel, rows=rows, tail_from=tail_from, tail_transposed=tail_transposed,
                          pos_base=float(pos_base),
                          pos_period=pos_period, class_major=class_major, emit_vn=emit_vn, seq_rows=seq_rows,
                          rider=rider, cast_weights=cast_weights),
        grid=(grid,),
        in_specs=[
            row_spec(D_MODEL),
            _const_spec((1, D_MODEL)),
            _const_spec(w_in_b.shape),
            _const_spec((1, LANES)),
            _const_spec((1, GMLP_WIDTH)),
            _const_spec((1, GMLP_WIDTH)),
            *spatial_specs,
            *rider_specs,
        ],
        out_specs=out_specs,
        out_shape=out_shape,
        scratch_shapes=scratch,
        compiler_params=pltpu.CompilerParams(
            dimension_semantics=("arbitrary",), vmem_limit_bytes=VMEM_LIMIT_BYTES),
        name="project",
    )(x, g1, w_in_b, invf, ln_g, ln_b, wsp, bsp, *rider_args)


def _prompt_attn_kernel(q_ref, k_ref, v_ref, o_ref, kc_ref, vc_ref, q32_ref, acc_ref, m_ref, l_ref,
                        s_ref, p_ref, ms_ref):
    i = pl.program_id(1)
    rows = q_ref.shape[1]
    width = q_ref.shape[2]
    npair = width // LANES
    half = SUBLANES

    @pl.when(i == 0)
    def _():
        kc_ref[0, :, 0:rows, :] = jnp.zeros((N_CLASS, rows, width), BF16)
        vc_ref[0, :, 0:rows, :] = jnp.zeros((N_CLASS, rows, width), BF16)

    @pl.when(i > 0)
    def _():
        kc_ref[0, :, 0:rows, :] = kc_ref[0, :, rows:2 * rows, :]
        vc_ref[0, :, 0:rows, :] = vc_ref[0, :, rows:2 * rows, :]

    kc_ref[0, :, rows:2 * rows, :] = k_ref[...]
    vc_ref[0, :, rows:2 * rows, :] = v_ref[...]
    lo = rows - PACKED_ROWS
    for ref in (kc_ref, vc_ref):
        tail = ref[0, :, lo:2 * rows, :].astype(F32)
        ref[1, :, lo:2 * rows - PACKED_ROWS, :] = tail[:, half:half + rows, :].astype(BF16)
    q32_ref[...] = q_ref[...].astype(F32)
    has_prev = i > 0

    def lanes(pr):
        return slice(pr * LANES, (pr + 1) * LANES)

    mq, nk = rows, 2 * rows
    olane = lax.broadcasted_iota(jnp.int32, (mq, LANES), 1) < HEAD_DIM
    qzero = jnp.zeros((mq, LANES), BF16)
    ones_v = jnp.ones((nk, LANES), BF16)
    nt = (((1,), (1,)), ((), ()))

    def run(n_units, q_of, k_of, v_of, mask_of, state_of, put, first, last):
        units = lambda t: [t * ATTN_GROUP + j for j in range(ATTN_GROUP)]

        def scores(t, slot):
            for j, u in enumerate(units(t)):
                for pr in range(npair):
                    q = q_of(u, pr)
                    qq = jnp.concatenate([jnp.where(olane, q, qzero), jnp.where(olane, qzero, q)], axis=0)
                    s_ref[slot, j, pr] = lax.dot_general(qq, k_of(u, pr), nt, preferred_element_type=F32)

        def softmax(t, slot):
            for j, u in enumerate(units(t)):
                mask = mask_of(u)
                for pr in range(npair):
                    ms = []
                    for hh in range(2):
                        hs = slice(hh * mq, (hh + 1) * mq)
                        s = jnp.where(mask, s_ref[slot, j, pr, hs, :], NEG)
                        m = jnp.max(s, axis=-1, keepdims=True)
                        p_ref[slot, j, pr, hs, :] = jnp.exp2(s - m).astype(BF16)
                        ms.append(m)
                    ms_ref[slot, j, pr] = jnp.where(olane, ms[0], ms[1])

        def values(t, slot):
            for j, u in enumerate(units(t)):
                for pr in range(npair):
                    vv = jnp.concatenate([v_of(u, pr), ones_v], axis=1)
                    r = jnp.dot(p_ref[slot, j, pr], vv, preferred_element_type=F32)
                    pv = jnp.where(olane, r[0:mq, 0:LANES], r[mq:2 * mq, 0:LANES])
                    l = jnp.where(olane, r[0:mq, LANES:2 * LANES], r[mq:2 * mq, LANES:2 * LANES])
                    m = ms_ref[slot, j, pr]
                    if not first:
                        acc0, m0, l0 = state_of(u, pr)
                        m_new = jnp.maximum(m0, m)
                        a = jnp.exp2(m0 - m_new)
                        b = jnp.exp2(m - m_new)
                        pv = acc0 * a + pv * b
                        l = l0 * a + l * b
                        m = m_new
                    put(u, pr, pv / l if last else (pv, m, l))

        n_trips = n_units // ATTN_GROUP
        scores(0, 0)
        for t in range(n_trips):
            slot = t % 2
            if t > 0:
                values(t - 1, 1 - slot)
            softmax(t, slot)
            if t + 1 < n_trips:
                scores(t + 1, 1 - slot)
        values(n_trips - 1, (n_trips - 1) % 2)

    def put_state(slabs_of, r0_of, n):
        def put(u, pr, res):
            for a, sb in enumerate(slabs_of(u)):
                rs = slice(a * n, (a + 1) * n)
                acc_ref[sb, pl.ds(r0_of(u), n), lanes(pr)] = res[0][rs]
                m_ref[sb, pl.ds(r0_of(u), n), lanes(pr)] = res[1][rs]
                l_ref[sb, pl.ds(r0_of(u), n), lanes(pr)] = res[2][rs]
        return put

    def gather(ref, slabs, r0, n, pr):
        return jnp.concatenate([ref[sb, pl.ds(r0, n), lanes(pr)] for sb in slabs], axis=0)

    state_refs = (acc_ref, m_ref, l_ref)
    kplain, vplain = kc_ref.at[0], vc_ref.at[0]

    qi = lax.broadcasted_iota(jnp.int32, (mq, nk), 0)
    kj = lax.broadcasted_iota(jnp.int32, (mq, nk), 1)
    diff = qi + rows - kj
    mask16 = (diff >= 0) & (diff <= SPAN) & ((kj >= rows) | has_prev)
    run(N_CLASS,
        q_of=lambda u, pr: q_ref[u, :, lanes(pr)],
        k_of=lambda u, pr: kplain[u, :, lanes(pr)],
        v_of=lambda u, pr: vplain[u, :, lanes(pr)],
        mask_of=lambda u: mask16, state_of=None,
        put=put_state(lambda u: [u], lambda u: 0, rows), first=True, last=False)

    n4 = N_CLASS // 4
    qb = rows // n4
    d4 = 4 * (qi % qb - kj % (2 * qb) + qb) + (qi // qb - kj // (2 * qb))
    band4 = (d4 >= 0) & (d4 <= SPAN)
    cur4 = kj % (2 * qb) >= qb
    slabs4 = lambda u: [u // n4 + 4 * a for a in range(n4)]
    q0_4 = lambda u: (u % n4) * qb
    k0_4 = lambda u: rows - qb + (u % n4) * qb
    run(N_CLASS,
        q_of=lambda u, pr: gather(q_ref, slabs4(u), q0_4(u), qb, pr),
        k_of=lambda u, pr: gather(kplain, slabs4(u), k0_4(u), 2 * qb, pr),
        v_of=lambda u, pr: gather(vplain, slabs4(u), k0_4(u), 2 * qb, pr),
        mask_of=lambda u: band4 if u % n4 > 0 else band4 & (cur4 | has_prev),
        state_of=lambda u, pr: tuple(gather(ref, slabs4(u), q0_4(u), qb, pr) for ref in state_refs),
        put=put_state(slabs4, q0_4, qb), first=False, last=False)

    d1 = N_CLASS * (qi % half - kj % (2 * half) + half) + (qi // half - kj // (2 * half))
    band1 = (d1 >= 0) & (d1 <= SPAN)
    cur1 = kj % (2 * half) >= half
    every = list(range(N_CLASS))
    q0_1 = lambda u: u * half
    kcopy = lambda u: 1 - u % 2
    k0_1 = lambda u: ((rows - half + u * half) // PACKED_ROWS) * PACKED_ROWS

    def put_out(u, pr, res):
        for sb in every:
            acc_ref[sb, pl.ds(q0_1(u), half), lanes(pr)] = res[sb * half:(sb + 1) * half]

    run(rows // half,
        q_of=lambda u, pr: gather(q32_ref, every, q0_1(u), half, pr).astype(BF16),
        k_of=lambda u, pr: gather(kc_ref.at[kcopy(u)], every, k0_1(u), 2 * half, pr),
        v_of=lambda u, pr: gather(vc_ref.at[kcopy(u)], every, k0_1(u), 2 * half, pr),
        mask_of=lambda u: band1 if u > 0 else band1 & (cur1 | has_prev),
        state_of=lambda u, pr: tuple(gather(ref, every, q0_1(u), half, pr) for ref in state_refs),
        put=put_out, first=False, last=True)
    o_ref[...] = acc_ref[...].astype(BF16)


def _prompt_attention(q_cm, k_cm, v_cm):
    n_slab_rows = q_cm.shape[1]
    npair = ATTN_STEP_LANES // LANES
    spec = pl.BlockSpec((N_CLASS, SPAN, ATTN_STEP_LANES), lambda h, i: (0, i, h))
    return pl.pallas_call(
        _prompt_attn_kernel,
        grid=(ATTN_WIDTH // ATTN_STEP_LANES, n_slab_rows // SPAN),
        in_specs=[spec, spec, spec],
        out_specs=spec,
        out_shape=jax.ShapeDtypeStruct(q_cm.shape, BF16),
        scratch_shapes=[
            pltpu.VMEM((2, N_CLASS, 2 * SPAN, ATTN_STEP_LANES), BF16),
            pltpu.VMEM((2, N_CLASS, 2 * SPAN, ATTN_STEP_LANES), BF16),
            pltpu.VMEM((N_CLASS, SPAN, ATTN_STEP_LANES), F32),
            pltpu.VMEM((N_CLASS, SPAN, ATTN_STEP_LANES), F32),
            pltpu.VMEM((N_CLASS, SPAN, ATTN_STEP_LANES), F32),
            pltpu.VMEM((N_CLASS, SPAN, ATTN_STEP_LANES), F32),
            pltpu.VMEM((2, ATTN_GROUP, npair, 2 * SPAN, 2 * SPAN), F32),
            pltpu.VMEM((2, ATTN_GROUP, npair, 2 * SPAN, 2 * SPAN), BF16),
            pltpu.VMEM((2, ATTN_GROUP, npair, SPAN, LANES), F32),
        ],
        compiler_params=pltpu.CompilerParams(
            dimension_semantics=("arbitrary", "arbitrary"), vmem_limit_bytes=VMEM_LIMIT_BYTES),
        name="prompt_attention",
    )(q_cm, k_cm, v_cm)


def _sample_attn_kernel(q_ref, kn_ref, vn_ref, kt_ref, vt_ref, o_ref, *, t_new, lbuf):
    width = ATTN_WIDTH
    nrow = t_new * N_HEADS
    q = q_ref[...]
    kn = kn_ref[...]
    vnew = vn_ref[...]

    row = lax.broadcasted_iota(jnp.int32, (nrow, width), 0)
    lane = lax.broadcasted_iota(jnp.int32, (nrow, width), 1)
    hmask = (lane // HEAD_DIM) == (row % N_HEADS)
    qtok = jnp.concatenate([jnp.broadcast_to(q[j:j + 1, :], (N_HEADS, width)) for j in range(t_new)], axis=0)
    qrows = jnp.where(hmask, qtok, 0.0)

    kt = kt_ref[...].reshape(width, lbuf).astype(BF16)
    vt = vt_ref[...].reshape(width, lbuf).astype(BF16)
    s = jnp.dot(qrows.astype(BF16), kt, preferred_element_type=F32)

    def reach_count(delta):
        cnt = jnp.zeros(delta.shape, F32)
        for window, dil in DILATIONS:
            hit = (delta >= 0) & (delta % dil == 0) & (delta <= window)
            cnt = cnt + jnp.where(hit, 1.0, 0.0)
        return cnt

    pos = lax.broadcasted_iota(jnp.int32, (nrow, lbuf), 1)
    tok = lax.broadcasted_iota(jnp.int32, (nrow, lbuf), 0) // N_HEADS
    cnt = reach_count(lbuf + tok - pos)
    tok1 = tok[:, 0:1]
    cnew = [reach_count(tok1 - j) for j in range(t_new)]
    snew = [jnp.sum(qrows * kn[j:j + 1, :], axis=-1, keepdims=True) for j in range(t_new)]

    s = jnp.where(cnt > 0.0, s, NEG)
    m = jnp.max(s, axis=-1, keepdims=True)
    for j in range(t_new):
        m = jnp.maximum(m, jnp.where(cnew[j] > 0.0, snew[j], NEG))
    e = jnp.exp2(s - m) * cnt
    l = jnp.sum(e, axis=-1, keepdims=True)
    o = lax.dot_general(e.astype(BF16), vt, (((1,), (1,)), ((), ())), preferred_element_type=F32)
    for j in range(t_new):
        ej = jnp.where(cnew[j] > 0.0, jnp.exp2(snew[j] - m), 0.0) * cnew[j]
        l = l + ej
        o = o + ej * vnew[j:j + 1, :]
    o = jnp.where(hmask, o / l, 0.0)
    for j in range(t_new):
        o_ref[j:j + 1, :] = jnp.sum(o[j * N_HEADS:(j + 1) * N_HEADS, :], axis=0, keepdims=True)


def _sample_attention_specs(q, k_new, v_new, cache_kt, cache_vt):
    nb, t_new, width = q.shape
    lbuf = cache_kt.shape[-1]
    new_spec = pl.BlockSpec((None, t_new, width), lambda b: (b, 0, 0))
    cache_spec = pl.BlockSpec((None, N_HEADS, HEAD_DIM, lbuf), lambda b: (b, 0, 0, 0))
    return ([q, k_new, v_new, cache_kt, cache_vt],
            [new_spec, new_spec, new_spec, cache_spec, cache_spec],
            jax.ShapeDtypeStruct((nb, t_new, width), F32),
            pl.BlockSpec((None, t_new, width), lambda b: (b, 0, 0)),
            (t_new, lbuf))


def _sample_attention(q, k_new, v_new, cache_kt, cache_vt):
    args, in_specs, out_shape, out_spec, (t_new, lbuf) = _sample_attention_specs(q, k_new, v_new, cache_kt, cache_vt)
    return pl.pallas_call(
        functools.partial(_sample_attn_kernel, t_new=t_new, lbuf=lbuf),
        grid=(q.shape[0],),
        in_specs=in_specs,
        out_specs=out_spec,
        out_shape=out_shape,
        compiler_params=pltpu.CompilerParams(
            dimension_semantics=("arbitrary",), vmem_limit_bytes=VMEM_LIMIT_BYTES),
        name="sample_attention",
    )(*args)


def _finish_kernel(x_ref, attn_ref, gated_ref, p_ref, wo_ref, g2_ref, wup_ref, wdn_ref,
                   gg_ref, wg_ref, wple_ref, gf_ref, y_ref, nat_ref, *, apply_final):
    per_class = attn_ref.shape[1]
    for r in range(N_CLASS):
        for s in range(PAIRS):
            nat_ref[s, pl.ds(r, per_class, stride=N_CLASS), :] = attn_ref[r, :, s * LANES:(s + 1) * LANES].astype(F32)
    attn = jnp.concatenate([nat_ref[s] for s in range(PAIRS)], axis=-1).astype(BF16)
    mix = jnp.concatenate([attn, gated_ref[...]], axis=-1)
    h = x_ref[...] + jnp.dot(mix, wo_ref[...], preferred_element_type=F32)
    n2 = _rms(h, g2_ref[...]).astype(BF16)
    f = jnp.zeros_like(h)
    for c in range(D_FF // FF_CHUNK):
        cs = slice(c * FF_CHUNK, (c + 1) * FF_CHUNK)
        up = jnp.dot(n2, wup_ref[:, cs], preferred_element_type=F32)
        act = jnp.square(jnp.maximum(up, 0.0)).astype(BF16)
        f = f + jnp.dot(act, wdn_ref[cs, :], preferred_element_type=F32)
    h = h + f
    gate = jax.nn.sigmoid(jnp.dot(_rms(h, gg_ref[...]).astype(BF16), wg_ref[...], preferred_element_type=F32))
    ple = jnp.dot(p_ref[...].astype(BF16), wple_ref[...], preferred_element_type=F32)
    h = h + gate * ple
    y_ref[...] = _rms(h, gf_ref[...]) if apply_final else h


def _finish(x, attn, gated, p, wo, g2, wup, wdn, gg, wg, wple, gf, *, rows, apply_final):
    n_rows = x.shape[0]
    row_spec = lambda width: pl.BlockSpec((rows, width), lambda i: (i, 0))
    attn_spec = pl.BlockSpec((N_CLASS, rows // N_CLASS, ATTN_WIDTH), lambda i: (0, i, 0))
    scratch = [pltpu.VMEM((PAIRS, rows, LANES), F32)]
    return pl.pallas_call(
        functools.partial(_finish_kernel, apply_final=apply_final),
        grid=(n_rows // rows,),
        in_specs=[
            row_spec(D_MODEL), attn_spec, row_spec(GMLP_WIDTH), row_spec(PLE_DIM),
            _const_spec(wo.shape), _const_spec((1, D_MODEL)), _const_spec(wup.shape), _const_spec(wdn.shape),
            _const_spec((1, D_MODEL)), _const_spec(wg.shape), _const_spec(wple.shape), _const_spec((1, D_MODEL)),
        ],
        out_specs=row_spec(D_MODEL),
        out_shape=jax.ShapeDtypeStruct((n_rows, D_MODEL), F32),
        scratch_shapes=scratch,
        compiler_params=pltpu.CompilerParams(
            dimension_semantics=("arbitrary",), vmem_limit_bytes=VMEM_LIMIT_BYTES),
        name="finish",
    )(x, attn, gated, p, wo, g2, wup, wdn, gg, wg, wple, gf)


def _finish_casting_kernel(x_ref, attn_ref, gated_ref, p_ref, wo_ref, g2_ref, wup_ref, wdn_ref,
                           gg_ref, wg_ref, wple_ref, gf_ref,
                           y_ref, wo_b_ref, wup_b_ref, wdn_b_ref, wg_b_ref, wple_b_ref,
                           h_ref, n2_ref, f_ref, *, apply_final):
    c = pl.program_id(0)

    @pl.when(c == 0)
    def _():
        wo_b = wo_ref[...].astype(BF16)
        wo_b_ref[...] = wo_b
        mix = jnp.concatenate([attn_ref[...], gated_ref[...]], axis=-1)
        h = x_ref[...] + jnp.dot(mix, wo_b, preferred_element_type=F32)
        h_ref[...] = h
        n2_ref[...] = _rms(h, g2_ref[...]).astype(BF16)
        f_ref[...] = jnp.zeros(f_ref.shape, F32)

    wup_b = wup_ref[...].astype(BF16)
    wdn_b = wdn_ref[...].astype(BF16)
    wup_b_ref[...] = wup_b
    wdn_b_ref[...] = wdn_b
    up = jnp.dot(n2_ref[...], wup_b, preferred_element_type=F32)
    act = jnp.square(jnp.maximum(up, 0.0)).astype(BF16)
    f_ref[...] += jnp.dot(act, wdn_b, preferred_element_type=F32)

    @pl.when(c == pl.num_programs(0) - 1)
    def _():
        wg_b = wg_ref[...].astype(BF16)
        wple_b = wple_ref[...].astype(BF16)
        wg_b_ref[...] = wg_b
        wple_b_ref[...] = wple_b
        h = h_ref[...] + f_ref[...]
        gate = jax.nn.sigmoid(jnp.dot(_rms(h, gg_ref[...]).astype(BF16), wg_b, preferred_element_type=F32))
        ple = jnp.dot(p_ref[...].astype(BF16), wple_b, preferred_element_type=F32)
        h = h + gate * ple
        y_ref[...] = _rms(h, gf_ref[...]) if apply_final else h


def _finish_casting(x, attn, gated, p, wo, g2, wup, wdn, gg, wg, wple, gf, *, apply_final):
    rows = x.shape[0]
    bf = lambda w: jax.ShapeDtypeStruct(w.shape, BF16)
    up_spec = pl.BlockSpec((D_MODEL, FF_CHUNK), lambda c: (0, c))
    dn_spec = pl.BlockSpec((FF_CHUNK, D_MODEL), lambda c: (c, 0))
    whole = lambda a: _const_spec(a.shape)
    held = lambda a: pl.BlockSpec(a.shape, lambda c: (0,) * a.ndim)
    return pl.pallas_call(
        functools.partial(_finish_casting_kernel, apply_final=apply_final),
        grid=(D_FF // FF_CHUNK,),
        in_specs=[whole(x), whole(attn), whole(gated), whole(p), whole(wo), whole(g2), up_spec, dn_spec,
                  whole(gg), whole(wg), whole(wple), whole(gf)],
        out_specs=[held(x), held(wo), up_spec, dn_spec, held(wg), held(wple)],
        out_shape=[jax.ShapeDtypeStruct((rows, D_MODEL), F32), bf(wo), bf(wup), bf(wdn), bf(wg), bf(wple)],
        scratch_shapes=[pltpu.VMEM((rows, D_MODEL), F32), pltpu.VMEM((rows, D_MODEL), BF16),
                        pltpu.VMEM((rows, D_MODEL), F32)],
        compiler_params=pltpu.CompilerParams(
            dimension_semantics=("arbitrary",), vmem_limit_bytes=VMEM_LIMIT_BYTES),
        name="finish_casting",
    )(x, attn, gated, p, wo, g2, wup, wdn, gg, wg, wple, gf)


def _pair_spatial(w):
    g, l, _ = w.shape
    return w.reshape(g // 2, 2, l, l).transpose(0, 2, 1, 3).reshape(g // 2, l, 2 * l).astype(BF16)


def _bias_lanes(b):
    return jnp.repeat(b.T, GMLP_WIDTH // GMLP_GROUPS, axis=1)


def kernel(x_prompt, x_sample, cache_k, cache_v, p_prompt, p_sample, norm1_g, w_in, ln_v_g, ln_v_b,
           w_spatial, b_spatial, w_out, norm2_g, w_up, w_down, gate_norm_g, w_gate, w_ple, final_g):
    depth = w_in.shape[0]
    nbp, seq, _ = x_prompt.shape
    nbs, t_new, _ = x_sample.shape
    keep = min(WINDOW_MAX, seq)
    assert nbp == 1 and seq % (N_CLASS * SPAN) == 0 and (nbs * t_new) % CHUNK == 0 and t_new <= CHUNK
    n_s = nbs * t_new

    row2 = lambda a: a.reshape(1, -1)

    hp = x_prompt.reshape(seq, D_MODEL)
    hs = x_sample.reshape(n_s, D_MODEL)
    nk_p, nv_p, nk_s, nv_s, nvc_s = [], [], [], [], []
    for i in range(depth):
        last = i == depth - 1
        g2, gg, gf = row2(norm2_g[i]), row2(gate_norm_g[i]), row2(final_g)
        ln_g, ln_b = row2(ln_v_g[i]), row2(ln_v_b[i])

        wsp_s = w_spatial[i][:, :t_new, :t_new].reshape(-1)
        bsp_s = b_spatial[i][:, :t_new].reshape(-1)
        q_s, kf_s, vf_s, gated_s, vn, w_in_b = _project(hs, row2(norm1_g[i]), w_in[i], ln_g, ln_b, wsp_s, bsp_s,
                                                        rows=n_s, tail_rows=n_s, pos_base=PAST_LEN, pos_period=t_new,
                                                        class_major=False, emit_vn=True, seq_rows=t_new,
                                                        cast_weights=True)
        r3 = lambda a: a.reshape(nbs, t_new, ATTN_WIDTH)
        sample_ops = (r3(q_s), r3(kf_s), r3(vf_s),
                      cache_k[i].transpose(0, 2, 3, 1), cache_v[i].transpose(0, 2, 3, 1))

        ride = seq // PROMPT_ROWS == nbs
        wsp_p = _pair_spatial(w_spatial[i])
        bsp_p = _bias_lanes(b_spatial[i])
        q, k, v, kf, vf, gated, *rode = _project(hp, row2(norm1_g[i]), w_in_b, ln_g, ln_b, wsp_p, bsp_p,
                                                 rows=PROMPT_ROWS, tail_rows=keep, pos_base=0, pos_period=PROMPT_ROWS,
                                                 class_major=True, emit_vn=False, tail_transposed=True,
                                                 sample_attn=sample_ops if ride else None)

        attn_s = rode[0] if ride else _sample_attention(*sample_ops)
        hs, wo_b, wup_b, wdn_b, wg_b, wple_b = _finish_casting(
            hs, attn_s.reshape(n_s, ATTN_WIDTH).astype(BF16), gated_s, p_sample[i].reshape(n_s, PLE_DIM),
            w_out[i], g2, w_up[i], w_down[i], gg, w_gate[i], w_ple[i], gf, apply_final=last)
        nk_s.append(kf_s.reshape(nbs, t_new, N_HEADS, HEAD_DIM))
        nv_s.append(vf_s.reshape(nbs, t_new, N_HEADS, HEAD_DIM))

        attn_cm = _prompt_attention(q, k, v)
        hp = _finish(hp, attn_cm, gated, p_prompt[i].reshape(seq, PLE_DIM),
                     wo_b, g2, wup_b, wdn_b, gg, wg_b, wple_b, gf, rows=PROMPT_ROWS, apply_final=last)
        from_cm = lambda a: a.reshape(N_HEADS, HEAD_DIM, keep).transpose(2, 0, 1).reshape(nbp, keep, N_HEADS, HEAD_DIM)
        nk_p.append(from_cm(kf))
        nv_p.append(from_cm(vf))
        nvc_s.append(vn.reshape(nbs, t_new, GMLP_WIDTH))

    return (hp.reshape(nbp, seq, D_MODEL), hs.reshape(nbs, t_new, D_MODEL),
            jnp.stack(nk_p), jnp.stack(nv_p), jnp.stack(nk_s), jnp.stack(nv_s), jnp.stack(nvc_s))
```

```python
import functools

import jax
import jax.numpy as jnp
from jax import lax
from jax.experimental import pallas as pl
from jax.experimental.pallas import tpu as pltpu

D_MODEL = 1024
N_HEADS = 8
HEAD_DIM = 64
ATTN_WIDTH = N_HEADS * HEAD_DIM
GMLP_GROUPS = 8
GMLP_WIDTH = 512
CHUNK = 128
DILATIONS = ((128, 1), (512, 4), (2048, 16))
WINDOW_MAX = 2048
PAST_LEN = 16384
ROT_DIM = HEAD_DIM // 4
ROPE_THETA = 500000.0
D_FF = 4 * D_MODEL
PLE_DIM = 256
EPS = 1e-6
NEG = -1e30
Q_SCALE = HEAD_DIM ** -0.5 * 1.4426950408889634

LANES = 128
VMEM_LIMIT_BYTES = 56 * 1024 * 1024

N_CLASS = 16
DEINT = 4
SPAN = 128
PAIRS = ATTN_WIDTH // LANES
ATTN_STEP_LANES = 2 * LANES
ATTN_GROUP = 1
STAGE_LAG = 1
SUBLANES = 8
PACKED_ROWS = 16
FF_CHUNK = 1024
PROMPT_ROWS = 512

BF16 = jnp.bfloat16
F32 = jnp.float32


def _rms(x, g):
    ms = jnp.mean(x * x, axis=-1, keepdims=True)
    return x * lax.rsqrt(ms + EPS) * g


def _const_spec(shape):
    nd = len(shape)
    return pl.BlockSpec(shape, lambda *_: (0,) * nd, pipeline_mode=pl.Buffered(1))


def _project_kernel(x_ref, g1_ref, w_ref, invf_ref, lng_ref, lnb_ref, wsp_ref, bsp_ref, *rest,
                    rows, tail_from, tail_transposed, pos_base, pos_period, class_major, emit_vn, seq_rows, rider,
                    cast_weights):
    rest = list(rest)
    rider_in = [rest.pop(0) for _ in range(5)] if rider else None
    q_ref = rest.pop(0)
    k_ref, v_ref = (rest.pop(0), rest.pop(0)) if class_major else (None, None)
    kf_ref, vf_ref, gated_ref = (rest.pop(0) for _ in range(3))
    vn_ref = rest.pop(0) if emit_vn else None
    rider_out = rest.pop(0) if rider else None
    if cast_weights:
        w_bf16 = w_ref[...].astype(BF16)
        rest.pop(0)[...] = w_bf16
        wcols = lambda a, b: w_bf16[:, a:b]
    else:
        wcols = lambda a, b: w_ref[:, a:b]
    cos_off_ref, sin_off_ref = rest.pop(0), rest.pop(0)
    zs_ref, zq_ref = (rest.pop(0), rest.pop(0)) if class_major else (None, None)
    mix_ref = rest.pop(0) if seq_rows else None
    i = pl.program_id(0)

    @pl.when(i == 0)
    def _():
        off = (lax.broadcasted_iota(jnp.int32, (rows, LANES), 0) % pos_period).astype(F32)
        ang = off * invf_ref[...]
        cos_off_ref[...] = jnp.cos(ang)
        sin_off_ref[...] = jnp.sin(ang)

    tile_stride = rows if pos_period == rows else 0
    base = (i * tile_stride).astype(F32) + pos_base
    base_ang = base * invf_ref[...]
    cb, sb = jnp.cos(base_ang), jnp.sin(base_ang)
    co, so = cos_off_ref[...], sin_off_ref[...]
    cosf = cb * co - sb * so
    sint = sb * co + cb * so
    head_lane = lax.broadcasted_iota(jnp.int32, (1, LANES), 1) % HEAD_DIM
    sina = jnp.where(head_lane < ROT_DIM // 2, -sint, 0.0)
    sinb = jnp.where((head_lane >= ROT_DIM // 2) & (head_lane < ROT_DIM), sint, 0.0)

    def rope(z):
        return z * cosf + pltpu.roll(z, LANES - ROT_DIM // 2, 1) * sina + pltpu.roll(z, ROT_DIM // 2, 1) * sinb

    xn = _rms(x_ref[...], g1_ref[...]).astype(BF16)
    zq = jnp.dot(xn, wcols(0, ATTN_WIDTH), preferred_element_type=F32)
    zk = jnp.dot(xn, wcols(ATTN_WIDTH, 2 * ATTN_WIDTH), preferred_element_type=F32)
    zv = jnp.dot(xn, wcols(2 * ATTN_WIDTH, 3 * ATTN_WIDTH), preferred_element_type=F32)
    slabs = [slice(s * LANES, (s + 1) * LANES) for s in range(PAIRS)]
    qr = [rope(zq[:, sl]) * Q_SCALE for sl in slabs]
    kr = [rope(zk[:, sl]) for sl in slabs]

    @pl.when(i >= tail_from)
    def _():
        if tail_transposed:
            kf_ref[...] = jnp.concatenate(kr, axis=-1).T
            vf_ref[...] = zv.T
        else:
            for s, sl in enumerate(slabs):
                kf_ref[:, sl] = kr[s]
            vf_ref[...] = zv

    if class_major:
        per_class = rows // N_CLASS
        for s, sl in enumerate(slabs):
            zs_ref[s] = qr[s]
            zs_ref[PAIRS + s] = kr[s]
            zs_ref[2 * PAIRS + s] = zv[:, sl]
        for n in range(3 * PAIRS):
            for c in range(DEINT):
                zq_ref[n, c] = zs_ref[n, pl.ds(c, rows // DEINT, stride=DEINT), :]
        for r in range(N_CLASS):
            for s, sl in enumerate(slabs):
                pick = lambda n: zq_ref[n, r % DEINT, pl.ds(r // DEINT, per_class, stride=DEINT), :].astype(BF16)
                q_ref[r, :, sl] = pick(s)
                k_ref[r, :, sl] = pick(PAIRS + s)
                v_ref[r, :, sl] = pick(2 * PAIRS + s)
    else:
        for s, sl in enumerate(slabs):
            q_ref[:, sl] = qr[s]

    o_u = 3 * ATTN_WIDTH
    u = jax.nn.gelu(jnp.dot(xn, wcols(o_u, o_u + GMLP_WIDTH), preferred_element_type=F32))
    vc = jax.nn.gelu(jnp.dot(xn, wcols(o_u + GMLP_WIDTH, o_u + 2 * GMLP_WIDTH), preferred_element_type=F32))
    mu = jnp.mean(vc, axis=-1, keepdims=True)
    cen = vc - mu
    var = jnp.mean(cen * cen, axis=-1, keepdims=True)
    vn = cen * lax.rsqrt(var + EPS) * lng_ref[...] + lnb_ref[...]
    if emit_vn:
        vn_ref[...] = vn
    if seq_rows:
        nseq = rows // seq_rows
        in_first = lax.broadcasted_iota(jnp.int32, (1, LANES), 1) < HEAD_DIM
        for s in range(GMLP_WIDTH // LANES):
            ga, gb = 2 * s, 2 * s + 1
            mix_ref[0, s] = vn[:, s * LANES:(s + 1) * LANES]
            xs = [mix_ref[0, s, pl.ds(j, nseq, stride=seq_rows), :] for j in range(seq_rows)]
            for r in range(seq_rows):
                acc = jnp.broadcast_to(
                    jnp.where(in_first, bsp_ref[ga * seq_rows + r], bsp_ref[gb * seq_rows + r]), (nseq, LANES))
                for j in range(r + 1):
                    w = jnp.where(in_first, wsp_ref[(ga * seq_rows + r) * seq_rows + j],
                                  wsp_ref[(gb * seq_rows + r) * seq_rows + j])
                    acc = acc + w * xs[j]
                mix_ref[1, s, pl.ds(r, nseq, stride=seq_rows), :] = acc
        mixed = jnp.concatenate([mix_ref[1, s] for s in range(GMLP_WIDTH // LANES)], axis=-1)
        gated_ref[...] = (u * mixed).astype(BF16)
        return
    vnb = vn.astype(BF16)

    row = lax.broadcasted_iota(jnp.int32, (CHUNK, 2 * CHUNK), 0)
    col = lax.broadcasted_iota(jnp.int32, (CHUNK, 2 * CHUNK), 1)
    tril = (col % CHUNK) <= row
    lane = lax.broadcasted_iota(jnp.int32, (CHUNK, LANES), 1)
    first = lane < HEAD_DIM
    zero = jnp.zeros((CHUNK, LANES), BF16)
    for s in range(GMLP_WIDTH // LANES):
        sl = slice(s * LANES, (s + 1) * LANES)
        wp = jnp.where(tril, wsp_ref[s], jnp.zeros((), BF16))
        bias = bsp_ref[:, sl]
        for c in range(rows // CHUNK):
            rs = slice(c * CHUNK, (c + 1) * CHUNK)
            vblk = vnb[rs, sl]
            rhs = jnp.concatenate([jnp.where(first, vblk, zero), jnp.where(first, zero, vblk)], axis=0)
            mixed = jnp.dot(wp, rhs, preferred_element_type=F32) + bias
            gated_ref[rs, sl] = (u[rs, sl] * mixed).astype(BF16)

    if rider:
        _sample_attn_kernel(*rider_in, rider_out, t_new=rider[0], lbuf=rider[1])


def _project(x, g1, w_in_b, ln_g, ln_b, wsp, bsp, *, rows, tail_rows, pos_base, pos_period, class_major, emit_vn,
             seq_rows=None, tail_transposed=False, sample_attn=None, cast_weights=False):
    n_rows = x.shape[0]
    grid = n_rows // rows
    tail_from = (n_rows - tail_rows) // rows
    row_spec = lambda width: pl.BlockSpec((rows, width), lambda i: (i, 0))
    if tail_transposed:
        tail_shape = jax.ShapeDtypeStruct((ATTN_WIDTH, tail_rows), F32)
        tail_spec = pl.BlockSpec((ATTN_WIDTH, rows), lambda i: (0, jnp.maximum(i - tail_from, 0)))
    else:
        tail_shape = jax.ShapeDtypeStruct((tail_rows, ATTN_WIDTH), F32)
        tail_spec = pl.BlockSpec((rows, ATTN_WIDTH), lambda i: (jnp.maximum(i - tail_from, 0), 0))
    if class_major:
        qkv_shapes = [jax.ShapeDtypeStruct((N_CLASS, n_rows // N_CLASS, ATTN_WIDTH), BF16)] * 3
        qkv_specs = [pl.BlockSpec((N_CLASS, rows // N_CLASS, ATTN_WIDTH), lambda i: (0, i, 0))] * 3
    else:
        qkv_shapes = [jax.ShapeDtypeStruct((n_rows, ATTN_WIDTH), F32)]
        qkv_specs = [row_spec(ATTN_WIDTH)]
    out_shape = qkv_shapes + [
        tail_shape,
        tail_shape,
        jax.ShapeDtypeStruct((n_rows, GMLP_WIDTH), BF16),
    ]
    out_specs = qkv_specs + [tail_spec] * 2 + [row_spec(GMLP_WIDTH)]
    if emit_vn:
        out_shape.append(jax.ShapeDtypeStruct((n_rows, GMLP_WIDTH), F32))
        out_specs.append(row_spec(GMLP_WIDTH))
    scratch = [pltpu.VMEM((rows, LANES), F32), pltpu.VMEM((rows, LANES), F32)]
    if class_major:
        scratch.append(pltpu.VMEM((3 * PAIRS, rows, LANES), F32))
        scratch.append(pltpu.VMEM((3 * PAIRS, DEINT, rows // DEINT, LANES), F32))
    if seq_rows:
        scratch.append(pltpu.VMEM((2, GMLP_WIDTH // LANES, rows, LANES), F32))
        spatial_specs = [pl.BlockSpec(memory_space=pltpu.SMEM)] * 2
    else:
        spatial_specs = [_const_spec(wsp.shape), _const_spec(bsp.shape)]
    rider, rider_specs, rider_args = None, [], []
    if sample_attn is not None:
        rider_args, rider_specs, rider_out_shape, rider_out_spec, rider = _sample_attention_specs(*sample_attn)
        assert rider_args[0].shape[0] == grid
        out_shape.append(rider_out_shape)
        out_specs.append(rider_out_spec)
    if cast_weights:
        out_shape.append(jax.ShapeDtypeStruct(w_in_b.shape, BF16))
        out_specs.append(pl.BlockSpec(w_in_b.shape, lambda i: (0, 0)))
    inv_freq = ROPE_THETA ** (-jnp.arange(0, ROT_DIM, 2, dtype=F32) / ROT_DIM)
    per_head = jnp.concatenate([inv_freq, inv_freq, jnp.zeros((HEAD_DIM - ROT_DIM,), F32)])
    invf = jnp.tile(per_head, LANES // HEAD_DIM).reshape(1, LANES)
    return pl.pallas_call(
        functools.partial(_project_kernel, rows=rows, tail_from=tail_from, tail_transposed=tail_transposed,
                          pos_base=float(pos_base),
                          pos_period=pos_period, class_major=class_major, emit_vn=emit_vn, seq_rows=seq_rows,
                          rider=rider, cast_weights=cast_weights),
        grid=(grid,),
        in_specs=[
            row_spec(D_MODEL),
            _const_spec((1, D_MODEL)),
            _const_spec(w_in_b.shape),
            _const_spec((1, LANES)),
            _const_spec((1, GMLP_WIDTH)),
            _const_spec((1, GMLP_WIDTH)),
            *spatial_specs,
            *rider_specs,
        ],
        out_specs=out_specs,
        out_shape=out_shape,
        scratch_shapes=scratch,
        compiler_params=pltpu.CompilerParams(
            dimension_semantics=("arbitrary",), vmem_limit_bytes=VMEM_LIMIT_BYTES),
        name="project",
    )(x, g1, w_in_b, invf, ln_g, ln_b, wsp, bsp, *rider_args)


def _prompt_attn_kernel(q_ref, k_ref, v_ref, o_ref, kc_ref, vc_ref, q32_ref, acc_ref, m_ref, l_ref,
                        s_ref, p_ref, ms_ref):
    i = pl.program_id(1)
    rows = q_ref.shape[1]
    width = q_ref.shape[2]
    npair = width // LANES
    half = SUBLANES

    @pl.when(i == 0)
    def _():
        kc_ref[0, :, 0:rows, :] = jnp.zeros((N_CLASS, rows, width), BF16)
        vc_ref[0, :, 0:rows, :] = jnp.zeros((N_CLASS, rows, width), BF16)

    @pl.when(i > 0)
    def _():
        kc_ref[0, :, 0:rows, :] = kc_ref[0, :, rows:2 * rows, :]
        vc_ref[0, :, 0:rows, :] = vc_ref[0, :, rows:2 * rows, :]

    kc_ref[0, :, rows:2 * rows, :] = k_ref[...]
    vc_ref[0, :, rows:2 * rows, :] = v_ref[...]
    lo = rows - PACKED_ROWS
    for ref in (kc_ref, vc_ref):
        tail = ref[0, :, lo:2 * rows, :].astype(F32)
        ref[1, :, lo:2 * rows - PACKED_ROWS, :] = tail[:, half:half + rows, :].astype(BF16)
    q32_ref[...] = q_ref[...].astype(F32)
    has_prev = i > 0

    def lanes(pr):
        return slice(pr * LANES, (pr + 1) * LANES)

    mq, nk = rows, 2 * rows
    olane = lax.broadcasted_iota(jnp.int32, (mq, LANES), 1) < HEAD_DIM
    qzero = jnp.zeros((mq, LANES), BF16)
    ones_v = jnp.ones((nk, LANES), BF16)
    nt = (((1,), (1,)), ((), ()))

    def run(n_units, q_of, k_of, v_of, mask_of, state_of, put, first, last):
        units = lambda t: [t * ATTN_GROUP + j for j in range(ATTN_GROUP)]

        def scores(t, slot):
            for j, u in enumerate(units(t)):
                for pr in range(npair):
                    q = q_of(u, pr)
                    qq = jnp.concatenate([jnp.where(olane, q, qzero), jnp.where(olane, qzero, q)], axis=0)
                    s_ref[slot, j, pr] = lax.dot_general(qq, k_of(u, pr), nt, preferred_element_type=F32)

        def softmax(t, slot):
            for j, u in enumerate(units(t)):
                mask = mask_of(u)
                for pr in range(npair):
                    ms = []
                    for hh in range(2):
                        hs = slice(hh * mq, (hh + 1) * mq)
                        s = jnp.where(mask, s_ref[slot, j, pr, hs, :], NEG)
                        m = jnp.max(s, axis=-1, keepdims=True)
                        p_ref[slot, j, pr, hs, :] = jnp.exp2(s - m).astype(BF16)
                        ms.append(m)
                    ms_ref[slot, j, pr] = jnp.where(olane, ms[0], ms[1])

        def values(t, slot):
            for j, u in enumerate(units(t)):
                for pr in range(npair):
                    vv = jnp.concatenate([v_of(u, pr), ones_v], axis=1)
                    r = jnp.dot(p_ref[slot, j, pr], vv, preferred_element_type=F32)
                    pv = jnp.where(olane, r[0:mq, 0:LANES], r[mq:2 * mq, 0:LANES])
                    l = jnp.where(olane, r[0:mq, LANES:2 * LANES], r[mq:2 * mq, LANES:2 * LANES])
                    m = ms_ref[slot, j, pr]
                    if not first:
                        acc0, m0, l0 = state_of(u, pr)
                        m_new = jnp.maximum(m0, m)
                        a = jnp.exp2(m0 - m_new)
                        b = jnp.exp2(m - m_new)
                        pv = acc0 * a + pv * b
                        l = l0 * a + l * b
                        m = m_new
                    put(u, pr, pv / l if last else (pv, m, l))

        n_trips = n_units // ATTN_GROUP
        slots = STAGE_LAG + 1
        for step in range(n_trips + 2 * STAGE_LAG):
            c, b, a = step - 2 * STAGE_LAG, step - STAGE_LAG, step
            if 0 <= c < n_trips:
                values(c, c % slots)
            if 0 <= b < n_trips:
                softmax(b, b % slots)
            if a < n_trips:
                scores(a, a % slots)

    def put_state(slabs_of, r0_of, n):
        def put(u, pr, res):
            for a, sb in enumerate(slabs_of(u)):
                rs = slice(a * n, (a + 1) * n)
                acc_ref[sb, pl.ds(r0_of(u), n), lanes(pr)] = res[0][rs]
                m_ref[sb, pl.ds(r0_of(u), n), lanes(pr)] = res[1][rs]
                l_ref[sb, pl.ds(r0_of(u), n), lanes(pr)] = res[2][rs]
        return put

    def gather(ref, slabs, r0, n, pr):
        return jnp.concatenate([ref[sb, pl.ds(r0, n), lanes(pr)] for sb in slabs], axis=0)

    state_refs = (acc_ref, m_ref, l_ref)
    kplain, vplain = kc_ref.at[0], vc_ref.at[0]

    qi = lax.broadcasted_iota(jnp.int32, (mq, nk), 0)
    kj = lax.broadcasted_iota(jnp.int32, (mq, nk), 1)
    diff = qi + rows - kj
    mask16 = (diff >= 0) & (diff <= SPAN) & ((kj >= rows) | has_prev)
    run(N_CLASS,
        q_of=lambda u, pr: q_ref[u, :, lanes(pr)],
        k_of=lambda u, pr: kplain[u, :, lanes(pr)],
        v_of=lambda u, pr: vplain[u, :, lanes(pr)],
        mask_of=lambda u: mask16, state_of=None,
        put=put_state(lambda u: [u], lambda u: 0, rows), first=True, last=False)

    n4 = N_CLASS // 4
    qb = rows // n4
    d4 = 4 * (qi % qb - kj % (2 * qb) + qb) + (qi // qb - kj // (2 * qb))
    band4 = (d4 >= 0) & (d4 <= SPAN)
    cur4 = kj % (2 * qb) >= qb
    slabs4 = lambda u: [u // n4 + 4 * a for a in range(n4)]
    q0_4 = lambda u: (u % n4) * qb
    k0_4 = lambda u: rows - qb + (u % n4) * qb
    run(N_CLASS,
        q_of=lambda u, pr: gather(q_ref, slabs4(u), q0_4(u), qb, pr),
        k_of=lambda u, pr: gather(kplain, slabs4(u), k0_4(u), 2 * qb, pr),
        v_of=lambda u, pr: gather(vplain, slabs4(u), k0_4(u), 2 * qb, pr),
        mask_of=lambda u: band4 if u % n4 > 0 else band4 & (cur4 | has_prev),
        state_of=lambda u, pr: tuple(gather(ref, slabs4(u), q0_4(u), qb, pr) for ref in state_refs),
        put=put_state(slabs4, q0_4, qb), first=False, last=False)

    d1 = N_CLASS * (qi % half - kj % (2 * half) + half) + (qi // half - kj // (2 * half))
    band1 = (d1 >= 0) & (d1 <= SPAN)
    cur1 = kj % (2 * half) >= half
    every = list(range(N_CLASS))
    q0_1 = lambda u: u * half
    kcopy = lambda u: 1 - u % 2
    k0_1 = lambda u: ((rows - half + u * half) // PACKED_ROWS) * PACKED_ROWS

    def put_out(u, pr, res):
        for sb in every:
            acc_ref[sb, pl.ds(q0_1(u), half), lanes(pr)] = res[sb * half:(sb + 1) * half]

    run(rows // half,
        q_of=lambda u, pr: gather(q32_ref, every, q0_1(u), half, pr).astype(BF16),
        k_of=lambda u, pr: gather(kc_ref.at[kcopy(u)], every, k0_1(u), 2 * half, pr),
        v_of=lambda u, pr: gather(vc_ref.at[kcopy(u)], every, k0_1(u), 2 * half, pr),
        mask_of=lambda u: band1 if u > 0 else band1 & (cur1 | has_prev),
        state_of=lambda u, pr: tuple(gather(ref, every, q0_1(u), half, pr) for ref in state_refs),
        put=put_out, first=False, last=True)
    o_ref[...] = acc_ref[...].astype(BF16)


def _prompt_attention(q_cm, k_cm, v_cm):
    n_slab_rows = q_cm.shape[1]
    npair = ATTN_STEP_LANES // LANES
    spec = pl.BlockSpec((N_CLASS, SPAN, ATTN_STEP_LANES), lambda h, i: (0, i, h))
    return pl.pallas_call(
        _prompt_attn_kernel,
        grid=(ATTN_WIDTH // ATTN_STEP_LANES, n_slab_rows // SPAN),
        in_specs=[spec, spec, spec],
        out_specs=spec,
        out_shape=jax.ShapeDtypeStruct(q_cm.shape, BF16),
        scratch_shapes=[
            pltpu.VMEM((2, N_CLASS, 2 * SPAN, ATTN_STEP_LANES), BF16),
            pltpu.VMEM((2, N_CLASS, 2 * SPAN, ATTN_STEP_LANES), BF16),
            pltpu.VMEM((N_CLASS, SPAN, ATTN_STEP_LANES), F32),
            pltpu.VMEM((N_CLASS, SPAN, ATTN_STEP_LANES), F32),
            pltpu.VMEM((N_CLASS, SPAN, ATTN_STEP_LANES), F32),
            pltpu.VMEM((N_CLASS, SPAN, ATTN_STEP_LANES), F32),
            pltpu.VMEM((STAGE_LAG + 1, ATTN_GROUP, npair, 2 * SPAN, 2 * SPAN), F32),
            pltpu.VMEM((STAGE_LAG + 1, ATTN_GROUP, npair, 2 * SPAN, 2 * SPAN), BF16),
            pltpu.VMEM((STAGE_LAG + 1, ATTN_GROUP, npair, SPAN, LANES), F32),
        ],
        compiler_params=pltpu.CompilerParams(
            dimension_semantics=("arbitrary", "arbitrary"), vmem_limit_bytes=VMEM_LIMIT_BYTES),
        name="prompt_attention",
    )(q_cm, k_cm, v_cm)


def _sample_attn_kernel(q_ref, kn_ref, vn_ref, kt_ref, vt_ref, o_ref, *, t_new, lbuf):
    width = ATTN_WIDTH
    nrow = t_new * N_HEADS
    q = q_ref[...]
    kn = kn_ref[...]
    vnew = vn_ref[...]

    row = lax.broadcasted_iota(jnp.int32, (nrow, width), 0)
    lane = lax.broadcasted_iota(jnp.int32, (nrow, width), 1)
    hmask = (lane // HEAD_DIM) == (row % N_HEADS)
    qtok = jnp.concatenate([jnp.broadcast_to(q[j:j + 1, :], (N_HEADS, width)) for j in range(t_new)], axis=0)
    qrows = jnp.where(hmask, qtok, 0.0)

    kt = kt_ref[...].reshape(width, lbuf).astype(BF16)
    vt = vt_ref[...].reshape(width, lbuf).astype(BF16)
    s = jnp.dot(qrows.astype(BF16), kt, preferred_element_type=F32)

    def reach_count(delta):
        cnt = jnp.zeros(delta.shape, F32)
        for window, dil in DILATIONS:
            hit = (delta >= 0) & (delta % dil == 0) & (delta <= window)
            cnt = cnt + jnp.where(hit, 1.0, 0.0)
        return cnt

    pos = lax.broadcasted_iota(jnp.int32, (nrow, lbuf), 1)
    tok = lax.broadcasted_iota(jnp.int32, (nrow, lbuf), 0) // N_HEADS
    cnt = reach_count(lbuf + tok - pos)
    tok1 = tok[:, 0:1]
    cnew = [reach_count(tok1 - j) for j in range(t_new)]
    snew = [jnp.sum(qrows * kn[j:j + 1, :], axis=-1, keepdims=True) for j in range(t_new)]

    s = jnp.where(cnt > 0.0, s, NEG)
    m = jnp.max(s, axis=-1, keepdims=True)
    for j in range(t_new):
        m = jnp.maximum(m, jnp.where(cnew[j] > 0.0, snew[j], NEG))
    e = jnp.exp2(s - m) * cnt
    l = jnp.sum(e, axis=-1, keepdims=True)
    o = lax.dot_general(e.astype(BF16), vt, (((1,), (1,)), ((), ())), preferred_element_type=F32)
    for j in range(t_new):
        ej = jnp.where(cnew[j] > 0.0, jnp.exp2(snew[j] - m), 0.0) * cnew[j]
        l = l + ej
        o = o + ej * vnew[j:j + 1, :]
    o = jnp.where(hmask, o / l, 0.0)
    for j in range(t_new):
        o_ref[j:j + 1, :] = jnp.sum(o[j * N_HEADS:(j + 1) * N_HEADS, :], axis=0, keepdims=True)


def _sample_attention_specs(q, k_new, v_new, cache_kt, cache_vt):
    nb, t_new, width = q.shape
    lbuf = cache_kt.shape[-1]
    new_spec = pl.BlockSpec((None, t_new, width), lambda b: (b, 0, 0))
    cache_spec = pl.BlockSpec((None, N_HEADS, HEAD_DIM, lbuf), lambda b: (b, 0, 0, 0))
    return ([q, k_new, v_new, cache_kt, cache_vt],
            [new_spec, new_spec, new_spec, cache_spec, cache_spec],
            jax.ShapeDtypeStruct((nb, t_new, width), F32),
            pl.BlockSpec((None, t_new, width), lambda b: (b, 0, 0)),
            (t_new, lbuf))


def _sample_attention(q, k_new, v_new, cache_kt, cache_vt):
    args, in_specs, out_shape, out_spec, (t_new, lbuf) = _sample_attention_specs(q, k_new, v_new, cache_kt, cache_vt)
    return pl.pallas_call(
        functools.partial(_sample_attn_kernel, t_new=t_new, lbuf=lbuf),
        grid=(q.shape[0],),
        in_specs=in_specs,
        out_specs=out_spec,
        out_shape=out_shape,
        compiler_params=pltpu.CompilerParams(
            dimension_semantics=("arbitrary",), vmem_limit_bytes=VMEM_LIMIT_BYTES),
        name="sample_attention",
    )(*args)


def _finish_kernel(x_ref, attn_ref, gated_ref, p_ref, wo_ref, g2_ref, wup_ref, wdn_ref,
                   gg_ref, wg_ref, wple_ref, gf_ref, y_ref, nat_ref, *, apply_final):
    per_class = attn_ref.shape[1]
    for r in range(N_CLASS):
        for s in range(PAIRS):
            nat_ref[s, pl.ds(r, per_class, stride=N_CLASS), :] = attn_ref[r, :, s * LANES:(s + 1) * LANES].astype(F32)
    attn = jnp.concatenate([nat_ref[s] for s in range(PAIRS)], axis=-1).astype(BF16)
    mix = jnp.concatenate([attn, gated_ref[...]], axis=-1)
    h = x_ref[...] + jnp.dot(mix, wo_ref[...], preferred_element_type=F32)
    n2 = _rms(h, g2_ref[...]).astype(BF16)
    f = jnp.zeros_like(h)
    for c in range(D_FF // FF_CHUNK):
        cs = slice(c * FF_CHUNK, (c + 1) * FF_CHUNK)
        up = jnp.dot(n2, wup_ref[:, cs], preferred_element_type=F32)
        act = jnp.square(jnp.maximum(up, 0.0)).astype(BF16)
        f = f + jnp.dot(act, wdn_ref[cs, :], preferred_element_type=F32)
    h = h + f
    gate = jax.nn.sigmoid(jnp.dot(_rms(h, gg_ref[...]).astype(BF16), wg_ref[...], preferred_element_type=F32))
    ple = jnp.dot(p_ref[...].astype(BF16), wple_ref[...], preferred_element_type=F32)
    h = h + gate * ple
    y_ref[...] = _rms(h, gf_ref[...]) if apply_final else h


def _finish(x, attn, gated, p, wo, g2, wup, wdn, gg, wg, wple, gf, *, rows, apply_final):
    n_rows = x.shape[0]
    row_spec = lambda width: pl.BlockSpec((rows, width), lambda i: (i, 0))
    attn_spec = pl.BlockSpec((N_CLASS, rows // N_CLASS, ATTN_WIDTH), lambda i: (0, i, 0))
    scratch = [pltpu.VMEM((PAIRS, rows, LANES), F32)]
    return pl.pallas_call(
        functools.partial(_finish_kernel, apply_final=apply_final),
        grid=(n_rows // rows,),
        in_specs=[
            row_spec(D_MODEL), attn_spec, row_spec(GMLP_WIDTH), row_spec(PLE_DIM),
            _const_spec(wo.shape), _const_spec((1, D_MODEL)), _const_spec(wup.shape), _const_spec(wdn.shape),
            _const_spec((1, D_MODEL)), _const_spec(wg.shape), _const_spec(wple.shape), _const_spec((1, D_MODEL)),
        ],
        out_specs=row_spec(D_MODEL),
        out_shape=jax.ShapeDtypeStruct((n_rows, D_MODEL), F32),
        scratch_shapes=scratch,
        compiler_params=pltpu.CompilerParams(
            dimension_semantics=("arbitrary",), vmem_limit_bytes=VMEM_LIMIT_BYTES),
        name="finish",
    )(x, attn, gated, p, wo, g2, wup, wdn, gg, wg, wple, gf)


def _finish_casting_kernel(x_ref, attn_ref, gated_ref, p_ref, wo_ref, g2_ref, wup_ref, wdn_ref,
                           gg_ref, wg_ref, wple_ref, gf_ref,
                           y_ref, wo_b_ref, wup_b_ref, wdn_b_ref, wg_b_ref, wple_b_ref,
                           h_ref, n2_ref, f_ref, *, apply_final):
    c = pl.program_id(0)

    @pl.when(c == 0)
    def _():
        wo_b = wo_ref[...].astype(BF16)
        wo_b_ref[...] = wo_b
        mix = jnp.concatenate([attn_ref[...], gated_ref[...]], axis=-1)
        h = x_ref[...] + jnp.dot(mix, wo_b, preferred_element_type=F32)
        h_ref[...] = h
        n2_ref[...] = _rms(h, g2_ref[...]).astype(BF16)
        f_ref[...] = jnp.zeros(f_ref.shape, F32)

    wup_b = wup_ref[...].astype(BF16)
    wdn_b = wdn_ref[...].astype(BF16)
    wup_b_ref[...] = wup_b
    wdn_b_ref[...] = wdn_b
    up = jnp.dot(n2_ref[...], wup_b, preferred_element_type=F32)
    act = jnp.square(jnp.maximum(up, 0.0)).astype(BF16)
    f_ref[...] += jnp.dot(act, wdn_b, preferred_element_type=F32)

    @pl.when(c == pl.num_programs(0) - 1)
    def _():
        wg_b = wg_ref[...].astype(BF16)
        wple_b = wple_ref[...].astype(BF16)
        wg_b_ref[...] = wg_b
        wple_b_ref[...] = wple_b
        h = h_ref[...] + f_ref[...]
        gate = jax.nn.sigmoid(jnp.dot(_rms(h, gg_ref[...]).astype(BF16), wg_b, preferred_element_type=F32))
        ple = jnp.dot(p_ref[...].astype(BF16), wple_b, preferred_element_type=F32)
        h = h + gate * ple
        y_ref[...] = _rms(h, gf_ref[...]) if apply_final else h


def _finish_casting(x, attn, gated, p, wo, g2, wup, wdn, gg, wg, wple, gf, *, apply_final):
    rows = x.shape[0]
    bf = lambda w: jax.ShapeDtypeStruct(w.shape, BF16)
    up_spec = pl.BlockSpec((D_MODEL, FF_CHUNK), lambda c: (0, c))
    dn_spec = pl.BlockSpec((FF_CHUNK, D_MODEL), lambda c: (c, 0))
    whole = lambda a: _const_spec(a.shape)
    held = lambda a: pl.BlockSpec(a.shape, lambda c: (0,) * a.ndim)
    return pl.pallas_call(
        functools.partial(_finish_casting_kernel, apply_final=apply_final),
        grid=(D_FF // FF_CHUNK,),
        in_specs=[whole(x), whole(attn), whole(gated), whole(p), whole(wo), whole(g2), up_spec, dn_spec,
                  whole(gg), whole(wg), whole(wple), whole(gf)],
        out_specs=[held(x), held(wo), up_spec, dn_spec, held(wg), held(wple)],
        out_shape=[jax.ShapeDtypeStruct((rows, D_MODEL), F32), bf(wo), bf(wup), bf(wdn), bf(wg), bf(wple)],
        scratch_shapes=[pltpu.VMEM((rows, D_MODEL), F32), pltpu.VMEM((rows, D_MODEL), BF16),
                        pltpu.VMEM((rows, D_MODEL), F32)],
        compiler_params=pltpu.CompilerParams(
            dimension_semantics=("arbitrary",), vmem_limit_bytes=VMEM_LIMIT_BYTES),
        name="finish_casting",
    )(x, attn, gated, p, wo, g2, wup, wdn, gg, wg, wple, gf)


def _pair_spatial(w):
    g, l, _ = w.shape
    return w.reshape(g // 2, 2, l, l).transpose(0, 2, 1, 3).reshape(g // 2, l, 2 * l).astype(BF16)


def _bias_lanes(b):
    return jnp.repeat(b.T, GMLP_WIDTH // GMLP_GROUPS, axis=1)


def kernel(x_prompt, x_sample, cache_k, cache_v, p_prompt, p_sample, norm1_g, w_in, ln_v_g, ln_v_b,
           w_spatial, b_spatial, w_out, norm2_g, w_up, w_down, gate_norm_g, w_gate, w_ple, final_g):
    depth = w_in.shape[0]
    nbp, seq, _ = x_prompt.shape
    nbs, t_new, _ = x_sample.shape
    keep = min(WINDOW_MAX, seq)
    assert nbp == 1 and seq % (N_CLASS * SPAN) == 0 and (nbs * t_new) % CHUNK == 0 and t_new <= CHUNK
    n_s = nbs * t_new

    row2 = lambda a: a.reshape(1, -1)

    hp = x_prompt.reshape(seq, D_MODEL)
    hs = x_sample.reshape(n_s, D_MODEL)
    nk_p, nv_p, nk_s, nv_s, nvc_s = [], [], [], [], []
    for i in range(depth):
        last = i == depth - 1
        g2, gg, gf = row2(norm2_g[i]), row2(gate_norm_g[i]), row2(final_g)
        ln_g, ln_b = row2(ln_v_g[i]), row2(ln_v_b[i])

        wsp_s = w_spatial[i][:, :t_new, :t_new].reshape(-1)
        bsp_s = b_spatial[i][:, :t_new].reshape(-1)
        q_s, kf_s, vf_s, gated_s, vn, w_in_b = _project(hs, row2(norm1_g[i]), w_in[i], ln_g, ln_b, wsp_s, bsp_s,
                                                        rows=n_s, tail_rows=n_s, pos_base=PAST_LEN, pos_period=t_new,
                                                        class_major=False, emit_vn=True, seq_rows=t_new,
                                                        cast_weights=True)
        r3 = lambda a: a.reshape(nbs, t_new, ATTN_WIDTH)
        sample_ops = (r3(q_s), r3(kf_s), r3(vf_s),
                      cache_k[i].transpose(0, 2, 3, 1), cache_v[i].transpose(0, 2, 3, 1))

        ride = seq // PROMPT_ROWS == nbs
        wsp_p = _pair_spatial(w_spatial[i])
        bsp_p = _bias_lanes(b_spatial[i])
        q, k, v, kf, vf, gated, *rode = _project(hp, row2(norm1_g[i]), w_in_b, ln_g, ln_b, wsp_p, bsp_p,
                                                 rows=PROMPT_ROWS, tail_rows=keep, pos_base=0, pos_period=PROMPT_ROWS,
                                                 class_major=True, emit_vn=False, tail_transposed=True,
                                                 sample_attn=sample_ops if ride else None)

        attn_s = rode[0] if ride else _sample_attention(*sample_ops)
        hs, wo_b, wup_b, wdn_b, wg_b, wple_b = _finish_casting(
            hs, attn_s.reshape(n_s, ATTN_WIDTH).astype(BF16), gated_s, p_sample[i].reshape(n_s, PLE_DIM),
            w_out[i], g2, w_up[i], w_down[i], gg, w_gate[i], w_ple[i], gf, apply_final=last)
        nk_s.append(kf_s.reshape(nbs, t_new, N_HEADS, HEAD_DIM))
        nv_s.append(vf_s.reshape(nbs, t_new, N_HEADS, HEAD_DIM))

        attn_cm = _prompt_attention(q, k, v)
        hp = _finish(hp, attn_cm, gated, p_prompt[i].reshape(seq, PLE_DIM),
                     wo_b, g2, wup_b, wdn_b, gg, wg_b, wple_b, gf, rows=PROMPT_ROWS, apply_final=last)
        from_cm = lambda a: a.reshape(N_HEADS, HEAD_DIM, keep).transpose(2, 0, 1).reshape(nbp, keep, N_HEADS, HEAD_DIM)
        nk_p.append(from_cm(kf))
        nv_p.append(from_cm(vf))
        nvc_s.append(vn.reshape(nbs, t_new, GMLP_WIDTH))

    return (hp.reshape(nbp, seq, D_MODEL), hs.reshape(nbs, t_new, D_MODEL),
            jnp.stack(nk_p), jnp.stack(nv_p), jnp.stack(nk_s), jnp.stack(nv_s), jnp.stack(nvc_s))
```

```python
import functools

import jax
import jax.numpy as jnp
from jax import lax
from jax.experimental import pallas as pl
from jax.experimental.pallas import tpu as pltpu

D_MODEL = 1024
N_HEADS = 8
HEAD_DIM = 64
ATTN_WIDTH = N_HEADS * HEAD_DIM
GMLP_GROUPS = 8
GMLP_WIDTH = 512
CHUNK = 128
DILATIONS = ((128, 1), (512, 4), (2048, 16))
WINDOW_MAX = 2048
PAST_LEN = 16384
ROT_DIM = HEAD_DIM // 4
ROPE_THETA = 500000.0
D_FF = 4 * D_MODEL
PLE_DIM = 256
EPS = 1e-6
NEG = -1e30
Q_SCALE = HEAD_DIM ** -0.5 * 1.4426950408889634

LANES = 128
VMEM_LIMIT_BYTES = 56 * 1024 * 1024

N_CLASS = 16
DEINT = 4
SPAN = 128
PAIRS = ATTN_WIDTH // LANES
ATTN_STEP_LANES = 2 * LANES
ATTN_GROUP = 1
STAGE_LAG = 1
SUBLANES = 8
PACKED_ROWS = 16
FF_CHUNK = 1024
PROMPT_ROWS = 512

BF16 = jnp.bfloat16
F32 = jnp.float32


def _rms(x, g):
    ms = jnp.mean(x * x, axis=-1, keepdims=True)
    return x * lax.rsqrt(ms + EPS) * g


def _const_spec(shape):
    nd = len(shape)
    return pl.BlockSpec(shape, lambda *_: (0,) * nd, pipeline_mode=pl.Buffered(1))


def _project_kernel(x_ref, g1_ref, w_ref, invf_ref, lng_ref, lnb_ref, wsp_ref, bsp_ref, *rest,
                    rows, tail_from, tail_transposed, pos_base, pos_period, class_major, emit_vn, seq_rows, rider,
                    cast_weights):
    rest = list(rest)
    rider_in = [rest.pop(0) for _ in range(5)] if rider else None
    q_ref = rest.pop(0)
    k_ref, v_ref = (rest.pop(0), rest.pop(0)) if class_major else (None, None)
    kf_ref, vf_ref, gated_ref = (rest.pop(0) for _ in range(3))
    vn_ref = rest.pop(0) if emit_vn else None
    rider_out = rest.pop(0) if rider else None
    if cast_weights:
        w_bf16 = w_ref[...].astype(BF16)
        rest.pop(0)[...] = w_bf16
        wcols = lambda a, b: w_bf16[:, a:b]
    else:
        wcols = lambda a, b: w_ref[:, a:b]
    cos_off_ref, sin_off_ref = rest.pop(0), rest.pop(0)
    zs_ref, zq_ref = (rest.pop(0), rest.pop(0)) if class_major else (None, None)
    mix_ref = rest.pop(0) if seq_rows else None
    i = pl.program_id(0)

    @pl.when(i == 0)
    def _():
        off = (lax.broadcasted_iota(jnp.int32, (rows, LANES), 0) % pos_period).astype(F32)
        ang = off * invf_ref[...]
        cos_off_ref[...] = jnp.cos(ang)
        sin_off_ref[...] = jnp.sin(ang)

    tile_stride = rows if pos_period == rows else 0
    base = (i * tile_stride).astype(F32) + pos_base
    base_ang = base * invf_ref[...]
    cb, sb = jnp.cos(base_ang), jnp.sin(base_ang)
    co, so = cos_off_ref[...], sin_off_ref[...]
    cosf = cb * co - sb * so
    sint = sb * co + cb * so
    head_lane = lax.broadcasted_iota(jnp.int32, (1, LANES), 1) % HEAD_DIM
    sina = jnp.where(head_lane < ROT_DIM // 2, -sint, 0.0)
    sinb = jnp.where((head_lane >= ROT_DIM // 2) & (head_lane < ROT_DIM), sint, 0.0)

    def rope(z):
        return z * cosf + pltpu.roll(z, LANES - ROT_DIM // 2, 1) * sina + pltpu.roll(z, ROT_DIM // 2, 1) * sinb

    slabs =[slice(s * LANES, (s + 1) * LANES) for s in range(PAIRS)]
    o_u = 3 * ATTN_WIDTH
    xn = _rms(x_ref[...], g1_ref[...]).astype(BF16)
    zq = jnp.dot(xn, wcols(0, ATTN_WIDTH), preferred_element_type=F32)
    zk = jnp.dot(xn, wcols(ATTN_WIDTH, 2 * ATTN_WIDTH), preferred_element_type=F32)
    zv = jnp.dot(xn, wcols(2 * ATTN_WIDTH, 3 * ATTN_WIDTH), preferred_element_type=F32)
    qr = [rope(zq[:, sl]) * Q_SCALE for sl in slabs]
    kr = [rope(zk[:, sl]) for sl in slabs]
    for s, sl in enumerate(slabs):
        if class_major:
            zs_ref[s] = qr[s]
            zs_ref[PAIRS + s] = kr[s]
            zs_ref[2 * PAIRS + s] = zv[:, sl]
        else:
            q_ref[:, sl] = qr[s]
            kf_ref[:, sl] = kr[s]
            vf_ref[:, sl] = zv[:, sl]

    def tail_store(dst_ref, first_slab):
        @pl.when(i >= tail_from)
        def _():
            t = jnp.concatenate([zs_ref[first_slab + s] for s in range(PAIRS)], axis=-1)
            dst_ref[...] = t.T if tail_transposed else t

    if class_major:
        tail_store(kf_ref, PAIRS)
        per_class = rows // N_CLASS
        for n in range(3 * PAIRS):
            for c in range(DEINT):
                zq_ref[n, c] = zs_ref[n, pl.ds(c, rows // DEINT, stride=DEINT), :]
        for r in range(N_CLASS):
            for s, sl in enumerate(slabs):
                pick = lambda n: zq_ref[n, r % DEINT, pl.ds(r // DEINT, per_class, stride=DEINT), :].astype(BF16)
                q_ref[r, :, sl] = pick(s)
                k_ref[r, :, sl] = pick(PAIRS + s)
                v_ref[r, :, sl] = pick(2 * PAIRS + s)
        tail_store(vf_ref, 2 * PAIRS)

    u = jax.nn.gelu(jnp.dot(xn, wcols(o_u, o_u + GMLP_WIDTH), preferred_element_type=F32))
    vc = jax.nn.gelu(jnp.dot(xn, wcols(o_u + GMLP_WIDTH, o_u + 2 * GMLP_WIDTH), preferred_element_type=F32))
    mu = jnp.mean(vc, axis=-1, keepdims=True)
    cen = vc - mu
    var = jnp.mean(cen * cen, axis=-1, keepdims=True)
    vn = cen * lax.rsqrt(var + EPS) * lng_ref[...] + lnb_ref[...]
    if emit_vn:
        vn_ref[...] = vn
    if seq_rows:
        nseq = rows // seq_rows
        in_first = lax.broadcasted_iota(jnp.int32, (1, LANES), 1) < HEAD_DIM
        for s in range(GMLP_WIDTH // LANES):
            ga, gb = 2 * s, 2 * s + 1
            mix_ref[0, s] = vn[:, s * LANES:(s + 1) * LANES]
            xs = [mix_ref[0, s, pl.ds(j, nseq, stride=seq_rows), :] for j in range(seq_rows)]
            for r in range(seq_rows):
                acc = jnp.broadcast_to(
                    jnp.where(in_first, bsp_ref[ga * seq_rows + r], bsp_ref[gb * seq_rows + r]), (nseq, LANES))
                for j in range(r + 1):
                    w = jnp.where(in_first, wsp_ref[(ga * seq_rows + r) * seq_rows + j],
                                  wsp_ref[(gb * seq_rows + r) * seq_rows + j])
                    acc = acc + w * xs[j]
                mix_ref[1, s, pl.ds(r, nseq, stride=seq_rows), :] = acc
        mixed = jnp.concatenate([mix_ref[1, s] for s in range(GMLP_WIDTH // LANES)], axis=-1)
        gated_ref[...] = (u * mixed).astype(BF16)
    else:
        vnb = vn.astype(BF16)
        row = lax.broadcasted_iota(jnp.int32, (CHUNK, 2 * CHUNK), 0)
        col = lax.broadcasted_iota(jnp.int32, (CHUNK, 2 * CHUNK), 1)
        tril = (col % CHUNK) <= row
        first = lax.broadcasted_iota(jnp.int32, (CHUNK, LANES), 1) < HEAD_DIM
        zero = jnp.zeros((CHUNK, LANES), BF16)
        for s in range(GMLP_WIDTH // LANES):
            sl = slice(s * LANES, (s + 1) * LANES)
            wp = jnp.where(tril, wsp_ref[s], jnp.zeros((), BF16))
            bias = bsp_ref[:, sl]
            for c in range(rows // CHUNK):
                rs = slice(c * CHUNK, (c + 1) * CHUNK)
                vblk = vnb[rs, sl]
                rhs = jnp.concatenate([jnp.where(first, vblk, zero), jnp.where(first, zero, vblk)], axis=0)
                mixed = jnp.dot(wp, rhs, preferred_element_type=F32) + bias
                gated_ref[rs, sl] = (u[rs, sl] * mixed).astype(BF16)

    if rider:
        _sample_attn_kernel(*rider_in, rider_out, t_new=rider[0], lbuf=rider[1])


def _project(x, g1, w_in_b, ln_g, ln_b, wsp, bsp, *, rows, tail_rows, pos_base, pos_period, class_major, emit_vn,
             seq_rows=None, tail_transposed=False, sample_attn=None, cast_weights=False):
    n_rows = x.shape[0]
    grid = n_rows // rows
    tail_from = (n_rows - tail_rows) // rows
    row_spec = lambda width: pl.BlockSpec((rows, width), lambda i: (i, 0))
    if tail_transposed:
        tail_shape = jax.ShapeDtypeStruct((ATTN_WIDTH, tail_rows), F32)
        tail_spec = pl.BlockSpec((ATTN_WIDTH, rows), lambda i: (0, jnp.maximum(i - tail_from, 0)))
    else:
        tail_shape = jax.ShapeDtypeStruct((tail_rows, ATTN_WIDTH), F32)
        tail_spec = pl.BlockSpec((rows, ATTN_WIDTH), lambda i: (jnp.maximum(i - tail_from, 0), 0))
    if class_major:
        qkv_shapes = [jax.ShapeDtypeStruct((N_CLASS, n_rows // N_CLASS, ATTN_WIDTH), BF16)] * 3
        qkv_specs = [pl.BlockSpec((N_CLASS, rows // N_CLASS, ATTN_WIDTH), lambda i: (0, i, 0))] * 3
    else:
        qkv_shapes = [jax.ShapeDtypeStruct((n_rows, ATTN_WIDTH), F32)]
        qkv_specs = [row_spec(ATTN_WIDTH)]
    out_shape = qkv_shapes + [
        tail_shape,
        tail_shape,
        jax.ShapeDtypeStruct((n_rows, GMLP_WIDTH), BF16),
    ]
    out_specs = qkv_specs + [tail_spec] * 2 + [row_spec(GMLP_WIDTH)]
    if emit_vn:
        out_shape.append(jax.ShapeDtypeStruct((n_rows, GMLP_WIDTH), F32))
        out_specs.append(row_spec(GMLP_WIDTH))
    assert class_major or tail_rows == n_rows
    scratch = [pltpu.VMEM((rows, LANES), F32), pltpu.VMEM((rows, LANES), F32)]
    if class_major:
        scratch.append(pltpu.VMEM((3 * PAIRS, rows, LANES), F32))
        scratch.append(pltpu.VMEM((3 * PAIRS, DEINT, rows // DEINT, LANES), F32))
    if seq_rows:
        scratch.append(pltpu.VMEM((2, GMLP_WIDTH // LANES, rows, LANES), F32))
        spatial_specs = [pl.BlockSpec(memory_space=pltpu.SMEM)] * 2
    else:
        spatial_specs = [_const_spec(wsp.shape), _const_spec(bsp.shape)]
    rider, rider_specs, rider_args = None, [], []
    if sample_attn is not None:
        rider_args, rider_specs, rider_out_shape, rider_out_spec, rider = _sample_attention_specs(*sample_attn)
        assert rider_args[0].shape[0] == grid
        out_shape.append(rider_out_shape)
        out_specs.append(rider_out_spec)
    if cast_weights:
        out_shape.append(jax.ShapeDtypeStruct(w_in_b.shape, BF16))
        out_specs.append(pl.BlockSpec(w_in_b.shape, lambda i: (0, 0)))
    inv_freq = ROPE_THETA ** (-jnp.arange(0, ROT_DIM, 2, dtype=F32) / ROT_DIM)
    per_head = jnp.concatenate([inv_freq, inv_freq, jnp.zeros((HEAD_DIM - ROT_DIM,), F32)])
    invf = jnp.tile(per_head, LANES // HEAD_DIM).reshape(1, LANES)
    return pl.pallas_call(
        functools.partial(_project_kernel, rows=rows, tail_from=tail_from, tail_transposed=tail_transposed,
                          pos_base=float(pos_base),
                          pos_period=pos_period, class_major=class_major, emit_vn=emit_vn, seq_rows=seq_rows,
                          rider=rider, cast_weights=cast_weights),
        grid=(grid,),
        in_specs=[
            row_spec(D_MODEL),
            _const_spec((1, D_MODEL)),
            _const_spec(w_in_b.shape),
            _const_spec((1, LANES)),
            _const_spec((1, GMLP_WIDTH)),
            _const_spec((1, GMLP_WIDTH)),
            *spatial_specs,
            *rider_specs,
        ],
        out_specs=out_specs,
        out_shape=out_shape,
        scratch_shapes=scratch,
        compiler_params=pltpu.CompilerParams(
            dimension_semantics=("arbitrary",), vmem_limit_bytes=VMEM_LIMIT_BYTES),
        name="project",
    )(x, g1, w_in_b, invf, ln_g, ln_b, wsp, bsp, *rider_args)


def _prompt_attn_kernel(q_ref, k_ref, v_ref, o_ref, kc_ref, vc_ref, q32_ref, acc_ref, m_ref, l_ref,
                        s_ref, p_ref, ms_ref):
    i = pl.program_id(1)
    rows = q_ref.shape[1]
    width = q_ref.shape[2]
    npair = width // LANES
    half = SUBLANES

    @pl.when(i == 0)
    def _():
        kc_ref[0, :, 0:rows, :] = jnp.zeros((N_CLASS, rows, width), BF16)
        vc_ref[0, :, 0:rows, :] = jnp.zeros((N_CLASS, rows, width), BF16)

    @pl.when(i > 0)
    def _():
        kc_ref[0, :, 0:rows, :] = kc_ref[0, :, rows:2 * rows, :]
        vc_ref[0, :, 0:rows, :] = vc_ref[0, :, rows:2 * rows, :]

    kc_ref[0, :, rows:2 * rows, :] = k_ref[...]
    vc_ref[0, :, rows:2 * rows, :] = v_ref[...]
    lo = rows - PACKED_ROWS
    for ref in (kc_ref, vc_ref):
        tail = ref[0, :, lo:2 * rows, :].astype(F32)
        ref[1, :, lo:2 * rows - PACKED_ROWS, :] = tail[:, half:half + rows, :].astype(BF16)
    q32_ref[...] = q_ref[...].astype(F32)
    has_prev = i > 0

    def lanes(pr):
        return slice(pr * LANES, (pr + 1) * LANES)

    mq, nk = rows, 2 * rows
    olane = lax.broadcasted_iota(jnp.int32, (mq, LANES), 1) < HEAD_DIM
    qzero = jnp.zeros((mq, LANES), BF16)
    ones_v = jnp.ones((nk, LANES), BF16)
    nt = (((1,), (1,)), ((), ()))

    def run(n_units, q_of, k_of, v_of, mask_of, state_of, put, first, last):
        units = lambda t: [t * ATTN_GROUP + j for j in range(ATTN_GROUP)]

        def scores(t, slot):
            for j, u in enumerate(units(t)):
                for pr in range(npair):
                    q = q_of(u, pr)
                    qq = jnp.concatenate([jnp.where(olane, q, qzero), jnp.where(olane, qzero, q)], axis=0)
                    s_ref[slot, j, pr] = lax.dot_general(qq, k_of(u, pr), nt, preferred_element_type=F32)

        def softmax(t, slot):
            for j, u in enumerate(units(t)):
                mask = mask_of(u)
                for pr in range(npair):
                    ms = []
                    for hh in range(2):
                        hs = slice(hh * mq, (hh + 1) * mq)
                        s = jnp.where(mask, s_ref[slot, j, pr, hs, :], NEG)
                        m = jnp.max(s, axis=-1, keepdims=True)
                        p_ref[slot, j, pr, hs, :] = jnp.exp2(s - m).astype(BF16)
                        ms.append(m)
                    ms_ref[slot, j, pr] = jnp.where(olane, ms[0], ms[1])

        def values(t, slot):
            for j, u in enumerate(units(t)):
                for pr in range(npair):
                    vv = jnp.concatenate([v_of(u, pr), ones_v], axis=1)
                    r = jnp.dot(p_ref[slot, j, pr], vv, preferred_element_type=F32)
                    pv = jnp.where(olane, r[0:mq, 0:LANES], r[mq:2 * mq, 0:LANES])
                    l = jnp.where(olane, r[0:mq, LANES:2 * LANES], r[mq:2 * mq, LANES:2 * LANES])
                    m = ms_ref[slot, j, pr]
                    if not first:
                        acc0, m0, l0 = state_of(u, pr)
                        m_new = jnp.maximum(m0, m)
                        a = jnp.exp2(m0 - m_new)
                        b = jnp.exp2(m - m_new)
                        pv = acc0 * a + pv * b
                        l = l0 * a + l * b
                        m = m_new
                    put(u, pr, pv / l if last else (pv, m, l))

        n_trips = n_units // ATTN_GROUP
        slots = STAGE_LAG + 1
        for step in range(n_trips + 2 * STAGE_LAG):
            c, b, a = step - 2 * STAGE_LAG, step - STAGE_LAG, step
            if 0 <= c < n_trips:
                values(c, c % slots)
            if 0 <= b < n_trips:
                softmax(b, b % slots)
            if a < n_trips:
                scores(a, a % slots)

    def put_state(slabs_of, r0_of, n):
        def put(u, pr, res):
            for a, sb in enumerate(slabs_of(u)):
                rs = slice(a * n, (a + 1) * n)
                acc_ref[sb, pl.ds(r0_of(u), n), lanes(pr)] = res[0][rs]
                m_ref[sb, pl.ds(r0_of(u), n), lanes(pr)] = res[1][rs]
                l_ref[sb, pl.ds(r0_of(u), n), lanes(pr)] = res[2][rs]
        return put

    def gather(ref, slabs, r0, n, pr):
        return jnp.concatenate([ref[sb, pl.ds(r0, n), lanes(pr)] for sb in slabs], axis=0)

    state_refs = (acc_ref, m_ref, l_ref)
    kplain, vplain = kc_ref.at[0], vc_ref.at[0]

    qi = lax.broadcasted_iota(jnp.int32, (mq, nk), 0)
    kj = lax.broadcasted_iota(jnp.int32, (mq, nk), 1)
    diff = qi + rows - kj
    mask16 = (diff >= 0) & (diff <= SPAN) & ((kj >= rows) | has_prev)
    run(N_CLASS,
        q_of=lambda u, pr: q_ref[u, :, lanes(pr)],
        k_of=lambda u, pr: kplain[u, :, lanes(pr)],
        v_of=lambda u, pr: vplain[u, :, lanes(pr)],
        mask_of=lambda u: mask16, state_of=None,
        put=put_state(lambda u: [u], lambda u: 0, rows), first=True, last=False)

    n4 = N_CLASS // 4
    qb = rows // n4
    d4 = 4 * (qi % qb - kj % (2 * qb) + qb) + (qi // qb - kj // (2 * qb))
    band4 = (d4 >= 0) & (d4 <= SPAN)
    cur4 = kj % (2 * qb) >= qb
    slabs4 = lambda u: [u // n4 + 4 * a for a in range(n4)]
    q0_4 = lambda u: (u % n4) * qb
    k0_4 = lambda u: rows - qb + (u % n4) * qb
    run(N_CLASS,
        q_of=lambda u, pr: gather(q_ref, slabs4(u), q0_4(u), qb, pr),
        k_of=lambda u, pr: gather(kplain, slabs4(u), k0_4(u), 2 * qb, pr),
        v_of=lambda u, pr: gather(vplain, slabs4(u), k0_4(u), 2 * qb, pr),
        mask_of=lambda u: band4 if u % n4 > 0 else band4 & (cur4 | has_prev),
        state_of=lambda u, pr: tuple(gather(ref, slabs4(u), q0_4(u), qb, pr) for ref in state_refs),
        put=put_state(slabs4, q0_4, qb), first=False, last=False)

    d1 = N_CLASS * (qi % half - kj % (2 * half) + half) + (qi // half - kj // (2 * half))
    band1 = (d1 >= 0) & (d1 <= SPAN)
    cur1 = kj % (2 * half) >= half
    every = list(range(N_CLASS))
    q0_1 = lambda u: u * half
    kcopy = lambda u: 1 - u % 2
    k0_1 = lambda u: ((rows - half + u * half) // PACKED_ROWS) * PACKED_ROWS

    def put_out(u, pr, res):
        for sb in every:
            acc_ref[sb, pl.ds(q0_1(u), half), lanes(pr)] = res[sb * half:(sb + 1) * half]

    run(rows // half,
        q_of=lambda u, pr: gather(q32_ref, every, q0_1(u), half, pr).astype(BF16),
        k_of=lambda u, pr: gather(kc_ref.at[kcopy(u)], every, k0_1(u), 2 * half, pr),
        v_of=lambda u, pr: gather(vc_ref.at[kcopy(u)], every, k0_1(u), 2 * half, pr),
        mask_of=lambda u: band1 if u > 0 else band1 & (cur1 | has_prev),
        state_of=lambda u, pr: tuple(gather(ref, every, q0_1(u), half, pr) for ref in state_refs),
        put=put_out, first=False, last=True)
    o_ref[...] = acc_ref[...].astype(BF16)


def _prompt_attention(q_cm, k_cm, v_cm):
    n_slab_rows = q_cm.shape[1]
    npair = ATTN_STEP_LANES // LANES
    spec = pl.BlockSpec((N_CLASS, SPAN, ATTN_STEP_LANES), lambda h, i: (0, i, h))
    return pl.pallas_call(
        _prompt_attn_kernel,
        grid=(ATTN_WIDTH // ATTN_STEP_LANES, n_slab_rows // SPAN),
        in_specs=[spec, spec, spec],
        out_specs=spec,
        out_shape=jax.ShapeDtypeStruct(q_cm.shape, BF16),
        scratch_shapes=[
            pltpu.VMEM((2, N_CLASS, 2 * SPAN, ATTN_STEP_LANES), BF16),
            pltpu.VMEM((2, N_CLASS, 2 * SPAN, ATTN_STEP_LANES), BF16),
            pltpu.VMEM((N_CLASS, SPAN, ATTN_STEP_LANES), F32),
            pltpu.VMEM((N_CLASS, SPAN, ATTN_STEP_LANES), F32),
            pltpu.VMEM((N_CLASS, SPAN, ATTN_STEP_LANES), F32),
            pltpu.VMEM((N_CLASS, SPAN, ATTN_STEP_LANES), F32),
            pltpu.VMEM((STAGE_LAG + 1, ATTN_GROUP, npair, 2 * SPAN, 2 * SPAN), F32),
            pltpu.VMEM((STAGE_LAG + 1, ATTN_GROUP, npair, 2 * SPAN, 2 * SPAN), BF16),
            pltpu.VMEM((STAGE_LAG + 1, ATTN_GROUP, npair, SPAN, LANES), F32),
        ],
        compiler_params=pltpu.CompilerParams(
            dimension_semantics=("arbitrary", "arbitrary"), vmem_limit_bytes=VMEM_LIMIT_BYTES),
        name="prompt_attention",
    )(q_cm, k_cm, v_cm)


def _sample_attn_kernel(q_ref, kn_ref, vn_ref, kt_ref, vt_ref, o_ref, *, t_new, lbuf):
    width = ATTN_WIDTH
    nrow = t_new * N_HEADS
    q = q_ref[...]
    kn = kn_ref[...]
    vnew = vn_ref[...]

    row = lax.broadcasted_iota(jnp.int32, (nrow, width), 0)
    lane = lax.broadcasted_iota(jnp.int32, (nrow, width), 1)
    hmask = (lane // HEAD_DIM) == (row % N_HEADS)
    qtok = jnp.concatenate([jnp.broadcast_to(q[j:j + 1, :], (N_HEADS, width)) for j in range(t_new)], axis=0)
    qrows = jnp.where(hmask, qtok, 0.0)

    kt = kt_ref[...].reshape(width, lbuf).astype(BF16)
    vt = vt_ref[...].reshape(width, lbuf).astype(BF16)
    s = jnp.dot(qrows.astype(BF16), kt, preferred_element_type=F32)

    def reach_count(delta):
        cnt = jnp.zeros(delta.shape, F32)
        for window, dil in DILATIONS:
            hit = (delta >= 0) & (delta % dil == 0) & (delta <= window)
            cnt = cnt + jnp.where(hit, 1.0, 0.0)
        return cnt

    pos = lax.broadcasted_iota(jnp.int32, (nrow, lbuf), 1)
    tok = lax.broadcasted_iota(jnp.int32, (nrow, lbuf), 0) // N_HEADS
    cnt = reach_count(lbuf + tok - pos)
    tok1 = tok[:, 0:1]
    cnew = [reach_count(tok1 - j) for j in range(t_new)]
    snew = [jnp.sum(qrows * kn[j:j + 1, :], axis=-1, keepdims=True) for j in range(t_new)]

    s = jnp.where(cnt > 0.0, s, NEG)
    m = jnp.max(s, axis=-1, keepdims=True)
    for j in range(t_new):
        m = jnp.maximum(m, jnp.where(cnew[j] > 0.0, snew[j], NEG))
    e = jnp.exp2(s - m) * cnt
    l = jnp.sum(e, axis=-1, keepdims=True)
    o = lax.dot_general(e.astype(BF16), vt, (((1,), (1,)), ((), ())), preferred_element_type=F32)
    for j in range(t_new):
        ej = jnp.where(cnew[j] > 0.0, jnp.exp2(snew[j] - m), 0.0) * cnew[j]
        l = l + ej
        o = o + ej * vnew[j:j + 1, :]
    o = jnp.where(hmask, o / l, 0.0)
    for j in range(t_new):
        o_ref[j:j + 1, :] = jnp.sum(o[j * N_HEADS:(j + 1) * N_HEADS, :], axis=0, keepdims=True)


def _sample_attention_specs(q, k_new, v_new, cache_kt, cache_vt):
    nb, t_new, width = q.shape
    lbuf = cache_kt.shape[-1]
    new_spec = pl.BlockSpec((None, t_new, width), lambda b: (b, 0, 0))
    cache_spec = pl.BlockSpec((None, N_HEADS, HEAD_DIM, lbuf), lambda b: (b, 0, 0, 0))
    return ([q, k_new, v_new, cache_kt, cache_vt],
            [new_spec, new_spec, new_spec, cache_spec, cache_spec],
            jax.ShapeDtypeStruct((nb, t_new, width), F32),
            pl.BlockSpec((None, t_new, width), lambda b: (b, 0, 0)),
            (t_new, lbuf))


def _sample_attention(q, k_new, v_new, cache_kt, cache_vt):
    args, in_specs, out_shape, out_spec, (t_new, lbuf) = _sample_attention_specs(q, k_new, v_new, cache_kt, cache_vt)
    return pl.pallas_call(
        functools.partial(_sample_attn_kernel, t_new=t_new, lbuf=lbuf),
        grid=(q.shape[0],),
        in_specs=in_specs,
        out_specs=out_spec,
        out_shape=out_shape,
        compiler_params=pltpu.CompilerParams(
            dimension_semantics=("arbitrary",), vmem_limit_bytes=VMEM_LIMIT_BYTES),
        name="sample_attention",
    )(*args)


def _finish_kernel(x_ref, attn_ref, gated_ref, p_ref, wo_ref, g2_ref, wup_ref, wdn_ref,
                   gg_ref, wg_ref, wple_ref, gf_ref, y_ref, nat_ref, *, apply_final):
    per_class = attn_ref.shape[1]
    for r in range(N_CLASS):
        for s in range(PAIRS):
            nat_ref[s, pl.ds(r, per_class, stride=N_CLASS), :] = attn_ref[r, :, s * LANES:(s + 1) * LANES].astype(F32)
    attn = jnp.concatenate([nat_ref[s] for s in range(PAIRS)], axis=-1).astype(BF16)
    mix = jnp.concatenate([attn, gated_ref[...]], axis=-1)
    h = x_ref[...] + jnp.dot(mix, wo_ref[...], preferred_element_type=F32)
    n2 = _rms(h, g2_ref[...]).astype(BF16)
    f = jnp.zeros_like(h)
    for c in range(D_FF // FF_CHUNK):
        cs = slice(c * FF_CHUNK, (c + 1) * FF_CHUNK)
        up = jnp.dot(n2, wup_ref[:, cs], preferred_element_type=F32)
        act = jnp.square(jnp.maximum(up, 0.0)).astype(BF16)
        f = f + jnp.dot(act, wdn_ref[cs, :], preferred_element_type=F32)
    h = h + f
    gate = jax.nn.sigmoid(jnp.dot(_rms(h, gg_ref[...]).astype(BF16), wg_ref[...], preferred_element_type=F32))
    ple = jnp.dot(p_ref[...].astype(BF16), wple_ref[...], preferred_element_type=F32)
    h = h + gate * ple
    y_ref[...] = _rms(h, gf_ref[...]) if apply_final else h


def _finish(x, attn, gated, p, wo, g2, wup, wdn, gg, wg, wple, gf, *, rows, apply_final):
    n_rows = x.shape[0]
    row_spec = lambda width: pl.BlockSpec((rows, width), lambda i: (i, 0))
    attn_spec = pl.BlockSpec((N_CLASS, rows // N_CLASS, ATTN_WIDTH), lambda i: (0, i, 0))
    scratch = [pltpu.VMEM((PAIRS, rows, LANES), F32)]
    return pl.pallas_call(
        functools.partial(_finish_kernel, apply_final=apply_final),
        grid=(n_rows // rows,),
        in_specs=[
            row_spec(D_MODEL), attn_spec, row_spec(GMLP_WIDTH), row_spec(PLE_DIM),
            _const_spec(wo.shape), _const_spec((1, D_MODEL)), _const_spec(wup.shape), _const_spec(wdn.shape),
            _const_spec((1, D_MODEL)), _const_spec(wg.shape), _const_spec(wple.shape), _const_spec((1, D_MODEL)),
        ],
        out_specs=row_spec(D_MODEL),
        out_shape=jax.ShapeDtypeStruct((n_rows, D_MODEL), F32),
        scratch_shapes=scratch,
        compiler_params=pltpu.CompilerParams(
            dimension_semantics=("arbitrary",), vmem_limit_bytes=VMEM_LIMIT_BYTES),
        name="finish",
    )(x, attn, gated, p, wo, g2, wup, wdn, gg, wg, wple, gf)


def _finish_casting_kernel(x_ref, attn_ref, gated_ref, p_ref, wo_ref, g2_ref, wup_ref, wdn_ref,
                           gg_ref, wg_ref, wple_ref, gf_ref,
                           y_ref, wo_b_ref, wup_b_ref, wdn_b_ref, wg_b_ref, wple_b_ref,
                           h_ref, n2_ref, f_ref, *, apply_final):
    c = pl.program_id(0)

    @pl.when(c == 0)
    def _():
        wo_b = wo_ref[...].astype(BF16)
        wo_b_ref[...] = wo_b
        mix = jnp.concatenate([attn_ref[...], gated_ref[...]], axis=-1)
        h = x_ref[...] + jnp.dot(mix, wo_b, preferred_element_type=F32)
        h_ref[...] = h
        n2_ref[...] = _rms(h, g2_ref[...]).astype(BF16)
        f_ref[...] = jnp.zeros(f_ref.shape, F32)

    wup_b = wup_ref[...].astype(BF16)
    wdn_b = wdn_ref[...].astype(BF16)
    wup_b_ref[...] = wup_b
    wdn_b_ref[...] = wdn_b
    up = jnp.dot(n2_ref[...], wup_b, preferred_element_type=F32)
    act = jnp.square(jnp.maximum(up, 0.0)).astype(BF16)
    f_ref[...] += jnp.dot(act, wdn_b, preferred_element_type=F32)

    @pl.when(c == pl.num_programs(0) - 1)
    def _():
        wg_b = wg_ref[...].astype(BF16)
        wple_b = wple_ref[...].astype(BF16)
        wg_b_ref[...] = wg_b
        wple_b_ref[...] = wple_b
        h = h_ref[...] + f_ref[...]
        gate = jax.nn.sigmoid(jnp.dot(_rms(h, gg_ref[...]).astype(BF16), wg_b, preferred_element_type=F32))
        ple = jnp.dot(p_ref[...].astype(BF16), wple_b, preferred_element_type=F32)
        h = h + gate * ple
        y_ref[...] = _rms(h, gf_ref[...]) if apply_final else h


def _finish_casting(x, attn, gated, p, wo, g2, wup, wdn, gg, wg, wple, gf, *, apply_final):
    rows = x.shape[0]
    bf = lambda w: jax.ShapeDtypeStruct(w.shape, BF16)
    up_spec = pl.BlockSpec((D_MODEL, FF_CHUNK), lambda c: (0, c))
    dn_spec = pl.BlockSpec((FF_CHUNK, D_MODEL), lambda c: (c, 0))
    whole = lambda a: _const_spec(a.shape)
    held = lambda a: pl.BlockSpec(a.shape, lambda c: (0,) * a.ndim)
    return pl.pallas_call(
        functools.partial(_finish_casting_kernel, apply_final=apply_final),
        grid=(D_FF // FF_CHUNK,),
        in_specs=[whole(x), whole(attn), whole(gated), whole(p), whole(wo), whole(g2), up_spec, dn_spec,
                  whole(gg), whole(wg), whole(wple), whole(gf)],
        out_specs=[held(x), held(wo), up_spec, dn_spec, held(wg), held(wple)],
        out_shape=[jax.ShapeDtypeStruct((rows, D_MODEL), F32), bf(wo), bf(wup), bf(wdn), bf(wg), bf(wple)],
        scratch_shapes=[pltpu.VMEM((rows, D_MODEL), F32), pltpu.VMEM((rows, D_MODEL), BF16),
                        pltpu.VMEM((rows, D_MODEL), F32)],
        compiler_params=pltpu.CompilerParams(
            dimension_semantics=("arbitrary",), vmem_limit_bytes=VMEM_LIMIT_BYTES),
        name="finish_casting",
    )(x, attn, gated, p, wo, g2, wup, wdn, gg, wg, wple, gf)


def _pair_spatial(w):
    g, l, _ = w.shape
    return w.reshape(g // 2, 2, l, l).transpose(0, 2, 1, 3).reshape(g // 2, l, 2 * l).astype(BF16)


def _bias_lanes(b):
    return jnp.repeat(b.T, GMLP_WIDTH // GMLP_GROUPS, axis=1)


def kernel(x_prompt, x_sample, cache_k, cache_v, p_prompt, p_sample, norm1_g, w_in, ln_v_g, ln_v_b,
           w_spatial, b_spatial, w_out, norm2_g, w_up, w_down, gate_norm_g, w_gate, w_ple, final_g):
    depth = w_in.shape[0]
    nbp, seq, _ = x_prompt.shape
    nbs, t_new, _ = x_sample.shape
    keep = min(WINDOW_MAX, seq)
    assert nbp == 1 and seq % (N_CLASS * SPAN) == 0 and (nbs * t_new) % CHUNK == 0 and t_new <= CHUNK
    n_s = nbs * t_new

    row2 = lambda a: a.reshape(1, -1)

    hp = x_prompt.reshape(seq, D_MODEL)
    hs = x_sample.reshape(n_s, D_MODEL)
    nk_p, nv_p, nk_s, nv_s, nvc_s = [], [], [], [], []
    for i in range(depth):
        last = i == depth - 1
        g2, gg, gf = row2(norm2_g[i]), row2(gate_norm_g[i]), row2(final_g)
        ln_g, ln_b = row2(ln_v_g[i]), row2(ln_v_b[i])

        wsp_s = w_spatial[i][:, :t_new, :t_new].reshape(-1)
        bsp_s = b_spatial[i][:, :t_new].reshape(-1)
        q_s, kf_s, vf_s, gated_s, vn, w_in_b = _project(hs, row2(norm1_g[i]), w_in[i], ln_g, ln_b, wsp_s, bsp_s,
                                                        rows=n_s, tail_rows=n_s, pos_base=PAST_LEN, pos_period=t_new,
                                                        class_major=False, emit_vn=True, seq_rows=t_new,
                                                        cast_weights=True)
        r3 = lambda a: a.reshape(nbs, t_new, ATTN_WIDTH)
        sample_ops = (r3(q_s), r3(kf_s), r3(vf_s),
                      cache_k[i].transpose(0, 2, 3, 1), cache_v[i].transpose(0, 2, 3, 1))

        ride = seq // PROMPT_ROWS == nbs
        wsp_p = _pair_spatial(w_spatial[i])
        bsp_p = _bias_lanes(b_spatial[i])
        q, k, v, kf, vf, gated, *rode = _project(hp, row2(norm1_g[i]), w_in_b, ln_g, ln_b, wsp_p, bsp_p,
                                                 rows=PROMPT_ROWS, tail_rows=keep, pos_base=0, pos_period=PROMPT_ROWS,
                                                 class_major=True, emit_vn=False, tail_transposed=True,
                                                 sample_attn=sample_ops if ride else None)

        attn_s = rode[0] if ride else _sample_attention(*sample_ops)
        hs, wo_b, wup_b, wdn_b, wg_b, wple_b = _finish_casting(
            hs, attn_s.reshape(n_s, ATTN_WIDTH).astype(BF16), gated_s, p_sample[i].reshape(n_s, PLE_DIM),
            w_out[i], g2, w_up[i], w_down[i], gg, w_gate[i], w_ple[i], gf, apply_final=last)
        nk_s.append(kf_s.reshape(nbs, t_new, N_HEADS, HEAD_DIM))
        nv_s.append(vf_s.reshape(nbs, t_new, N_HEADS, HEAD_DIM))

        attn_cm = _prompt_attention(q, k, v)
        hp = _finish(hp, attn_cm, gated, p_prompt[i].reshape(seq, PLE_DIM),
                     wo_b, g2, wup_b, wdn_b, gg, wg_b, wple_b, gf, rows=PROMPT_ROWS, apply_final=last)
        from_cm = lambda a: a.reshape(N_HEADS, HEAD_DIM, keep).transpose(2, 0, 1).reshape(nbp, keep, N_HEADS, HEAD_DIM)
        nk_p.append(from_cm(kf))
        nv_p.append(from_cm(vf))
        nvc_s.append(vn.reshape(nbs, t_new, GMLP_WIDTH))

    return (hp.reshape(nbp, seq, D_MODEL), hs.reshape(nbs, t_new, D_MODEL),
            jnp.stack(nk_p), jnp.stack(nv_p), jnp.stack(nk_s), jnp.stack(nv_s), jnp.stack(nvc_s))
```
